```python
import jax, jax.numpy as jnp
from jax import lax
import numpy as np

D_MODEL = 1024
BATCH = 8
SEQ = 8192
DEPTH = 4

CHUNK = 64
CONV_CH = 512
CONV_K = 31
RET_HEADS = 4
RET_DK = 64
RET_DV = 128
RET_QK_W = RET_HEADS * RET_DK
RET_V_W = RET_HEADS * RET_DV
MIX_W = CONV_CH + RET_V_W
IN_W = 2 * CONV_CH + 2 * RET_QK_W + 2 * RET_V_W
ROPE_BASE = 10000.0
D_FF = 3584
N_EXPERTS = 8
TOP_K = 2
MOE_BLOCK = 256
N_DENSE = (DEPTH + 1) // 2
N_MOE = DEPTH // 2
ALPHA = (2.0 * DEPTH) ** 0.25
BETA = (8.0 * DEPTH) ** -0.25
LN_EPS = 1e-5

kernel_name = 'hybrid_conv_retention_moe_block'


def layer_norm(x, g, b):
    xf = x.astype(jnp.float32)
    mu = jnp.mean(xf, axis=-1, keepdims=True)
    var = jnp.mean(jnp.square(xf - mu), axis=-1, keepdims=True)
    return ((xf - mu) * lax.rsqrt(var + LN_EPS) * g + b).astype(x.dtype)


def rotary(x, pos):
    half = x.shape[-1] // 2
    inv_freq = ROPE_BASE ** (-jnp.arange(half, dtype=jnp.float32) / half)
    ang = pos.astype(jnp.float32)[:, None] * inv_freq[None, :]
    cos = jnp.cos(ang)[None, :, None, :]
    sin = jnp.sin(ang)[None, :, None, :]
    x1, x2 = x[..., :half], x[..., half:]
    return jnp.concatenate([x1 * cos - x2 * sin, x1 * sin + x2 * cos], axis=-1)


def retention_chunkwise(q, k, v):
    b, s, h, dk = q.shape
    dv = v.shape[-1]
    nc = s // CHUNK
    log_gamma = jnp.log1p(-(2.0 ** (-5.0 - jnp.arange(h, dtype=jnp.float32))))
    qc = q.reshape(b, nc, CHUNK, h, dk)
    kc = k.reshape(b, nc, CHUNK, h, dk)
    vc = v.reshape(b, nc, CHUNK, h, dv)
    idx = jnp.arange(CHUNK, dtype=jnp.float32)
    intra_decay = jnp.exp(log_gamma[:, None, None] * jnp.abs(idx[:, None] - idx[None, :]))
    scores = jnp.einsum('bnihd,bnjhd->bnhij', qc, kc) * intra_decay
    intra = jnp.einsum('bnhij,bnjhe->bnihe', scores, vc)
    k_decay = jnp.exp(log_gamma[None, :] * (CHUNK - 1.0 - idx)[:, None])
    kv = jnp.einsum('bnjhd,bnjhe->nbhde', kc * k_decay[:, :, None], vc)
    chunk_decay = jnp.exp(log_gamma * CHUNK)[None, :, None, None]

    def step(state, kv_n):
        return state * chunk_decay + kv_n, state

    _, prev = lax.scan(step, jnp.zeros((b, h, dk, dv), jnp.float32), kv)
    q_decay = jnp.exp(log_gamma[None, :] * (idx + 1.0)[:, None])
    cross = jnp.einsum('bnihd,nbhde->bnihe', qc * q_decay[:, :, None], prev)
    return (intra + cross).reshape(b, s, h, dv)


def hybrid_mixer(x, w_in, w_out, conv_w, conv_b, conv_ln_g, conv_ln_b, ret_gn_g, pos):
    b, s, _ = x.shape
    proj = jnp.einsum('bsd,df->bsf', x, w_in)
    cuts = (CONV_CH, 2 * CONV_CH, 2 * CONV_CH + RET_QK_W,
            2 * CONV_CH + 2 * RET_QK_W, 2 * CONV_CH + 2 * RET_QK_W + RET_V_W)
    glu_a, glu_b, q, k, v, g = jnp.split(proj, cuts, axis=-1)
    u = glu_a * jax.nn.sigmoid(glu_b)
    u = lax.conv_general_dilated(
        u, conv_w[:, None, :].astype(u.dtype), window_strides=(1,),
        padding=[(CONV_K - 1, 0)], dimension_numbers=('NWC', 'WIO', 'NWC'),
        feature_group_count=CONV_CH) + conv_b
    u = jax.nn.silu(layer_norm(u, conv_ln_g, conv_ln_b))
    qh = rotary(q.astype(jnp.float32).reshape(b, s, RET_HEADS, RET_DK), pos)
    kh = rotary(k.astype(jnp.float32).reshape(b, s, RET_HEADS, RET_DK), pos) * (RET_DK ** -0.5)
    vh = v.astype(jnp.float32).reshape(b, s, RET_HEADS, RET_DV)
    r = retention_chunkwise(qh, kh, vh)
    mu = jnp.mean(r, axis=-1, keepdims=True)
    var = jnp.mean(jnp.square(r - mu), axis=-1, keepdims=True)
    r = ((r - mu) * lax.rsqrt(var + LN_EPS)).reshape(b, s, RET_V_W) * ret_gn_g
    r = (jax.nn.silu(g.astype(jnp.float32)) * r).astype(x.dtype)
    mixed = jnp.concatenate([u.astype(x.dtype), r], axis=-1)
    return jnp.einsum('bsm,md->bsd', mixed, w_out)


def swiglu(h, w1, w3, w2):
    a = jnp.einsum('bsd,df->bsf', h, w1)
    c = jnp.einsum('bsd,df->bsf', h, w3)
    return jnp.einsum('bsf,fd->bsd', jax.nn.silu(a) * c, w2)


def moe_swiglu(h, router, w1, w3, w2):
    n_tok, d = h.shape
    n_assign = n_tok * TOP_K
    n_slots = -(-n_assign // MOE_BLOCK) * MOE_BLOCK + N_EXPERTS * MOE_BLOCK
    n_blocks = n_slots // MOE_BLOCK
    logits = jnp.einsum('nd,de->ne', h, router).astype(jnp.float32)
    top_logit, top_e = lax.top_k(logits, TOP_K)
    gate = jax.nn.softmax(top_logit, axis=-1)
    flat_e = top_e.reshape(-1).astype(jnp.int32)
    flat_tok = jnp.arange(n_assign, dtype=jnp.int32) // TOP_K
    flat_gate = gate.reshape(-1)
    order = jnp.argsort(flat_e)
    sorted_e = flat_e[order]
    counts = jnp.bincount(flat_e, length=N_EXPERTS).astype(jnp.int32)
    padded = (counts + MOE_BLOCK - 1) // MOE_BLOCK * MOE_BLOCK
    start = jnp.cumsum(counts) - counts
    pad_end = jnp.cumsum(padded)
    pad_start = pad_end - padded
    slot = pad_start[sorted_e] + jnp.arange(n_assign, dtype=jnp.int32) - start[sorted_e]
    slot_tok = jnp.zeros((n_slots,), jnp.int32).at[slot].set(flat_tok[order])
    slot_gate = jnp.zeros((n_slots,), jnp.float32).at[slot].set(flat_gate[order])
    block_start = jnp.arange(n_blocks, dtype=jnp.int32) * MOE_BLOCK
    block_e = jnp.minimum(jnp.searchsorted(pad_end, block_start, side='right'), N_EXPERTS - 1)
    xb = h[slot_tok].reshape(n_blocks, MOE_BLOCK, d)

    def expert_block(args):
        xe, e = args
        a = xe @ w1[e]
        c = xe @ w3[e]
        return (jax.nn.silu(a) * c) @ w2[e]

    yb = lax.map(expert_block, (xb, block_e)).reshape(n_slots, d)
    yb = yb * slot_gate[:, None].astype(yb.dtype)
    return jax.ops.segment_sum(yb, slot_tok, num_segments=n_tok).astype(h.dtype)


def setup_inputs(seed: int = 0) -> dict:
    key = jax.random.key(seed)
    ks = jax.random.split(key, 20)

    def nrm(k, shape, scale):
        return jax.random.normal(k, shape, jnp.float32) * scale

    return {
        'x': nrm(ks[0], (BATCH, SEQ, D_MODEL), 1.0),
        'w_in': nrm(ks[1], (DEPTH, D_MODEL, IN_W), D_MODEL ** -0.5),
        'w_out': nrm(ks[2], (DEPTH, MIX_W, D_MODEL), BETA * MIX_W ** -0.5),
        'conv_w': nrm(ks[3], (DEPTH, CONV_K, CONV_CH), CONV_K ** -0.5),
        'conv_b': nrm(ks[4], (DEPTH, CONV_CH), 0.02),
        'conv_ln_g': 1.0 + nrm(ks[5], (DEPTH, CONV_CH), 0.02),
        'conv_ln_b': nrm(ks[6], (DEPTH, CONV_CH), 0.02),
        'ret_gn_g': 1.0 + nrm(ks[7], (DEPTH, RET_V_W), 0.02),
        'ln1_g': 1.0 + nrm(ks[8], (DEPTH, D_MODEL), 0.02),
        'ln1_b': nrm(ks[9], (DEPTH, D_MODEL), 0.02),
        'ln2_g': 1.0 + nrm(ks[10], (DEPTH, D_MODEL), 0.02),
        'ln2_b': nrm(ks[11], (DEPTH, D_MODEL), 0.02),
        'dense_w1': nrm(ks[12], (N_DENSE, D_MODEL, D_FF), D_MODEL ** -0.5),
        'dense_w3': nrm(ks[13], (N_DENSE, D_MODEL, D_FF), D_MODEL ** -0.5),
        'dense_w2': nrm(ks[14], (N_DENSE, D_FF, D_MODEL), BETA * D_FF ** -0.5),
        'moe_router': nrm(ks[15], (N_MOE, D_MODEL, N_EXPERTS), D_MODEL ** -0.5),
        'moe_w1': nrm(ks[16], (N_MOE, N_EXPERTS, D_MODEL, D_FF), D_MODEL ** -0.5),
        'moe_w3': nrm(ks[17], (N_MOE, N_EXPERTS, D_MODEL, D_FF), D_MODEL ** -0.5),
        'moe_w2': nrm(ks[18], (N_MOE, N_EXPERTS, D_FF, D_MODEL), BETA * D_FF ** -0.5),
    }


def reference(x, w_in, w_out, conv_w, conv_b, conv_ln_g, conv_ln_b, ret_gn_g,
              ln1_g, ln1_b, ln2_g, ln2_b, dense_w1, dense_w3, dense_w2,
              moe_router, moe_w1, moe_w3, moe_w2):
    b, s, d = x.shape
    pos = jnp.arange(s, dtype=jnp.int32)
    for l in range(DEPTH):
        mix = hybrid_mixer(x, w_in[l], w_out[l], conv_w[l], conv_b[l],
                           conv_ln_g[l], conv_ln_b[l], ret_gn_g[l], pos)
        x = layer_norm(ALPHA * x + mix, ln1_g[l], ln1_b[l])
        i = l // 2
        if l % 2 == 0:
            f = swiglu(x, dense_w1[i], dense_w3[i], dense_w2[i])
        else:
            f = moe_swiglu(x.reshape(b * s, d), moe_router[i], moe_w1[i],
                           moe_w3[i], moe_w2[i]).reshape(b, s, d)
        x = layer_norm(ALPHA * x + f, ln2_g[l], ln2_b[l])
    return x
```

```python
import functools

import jax
import jax.numpy as jnp
from jax import lax
from jax.experimental import pallas as pl
from jax.experimental.pallas import tpu as pltpu

CHUNK = 64
CONV_CH = 512
CONV_K = 31
RET_HEADS = 4
RET_DK = 64
RET_DV = 128
RET_QK_W = RET_HEADS * RET_DK
RET_V_W = RET_HEADS * RET_DV
ROPE_BASE = 10000.0
N_EXPERTS = 8
LN_EPS = 1e-5

LANES = 128
SEQ_TILE = 512
RET_BLOCK = 128
CONV_ROWS = 16
HALO = 32
FFN_TILE = 512
FFN_CHUNK = 512
ROUTE_TILE = 1024
MOVE_TILE = 256
VMEM_LIMIT = 56 * 1024 * 1024

_BF = jnp.bfloat16
_F32 = jnp.float32


def _dot(a, b):
    return jnp.dot(a, b, preferred_element_type=_F32)


def _layer_norm(v, g, b):
    mu = jnp.mean(v, axis=-1, keepdims=True)
    d = v - mu
    var = jnp.mean(d * d, axis=-1, keepdims=True)
    return d * lax.rsqrt(var + LN_EPS) * g + b


def _sigmoid(v):
    return 1.0 / (1.0 + jnp.exp(-v))


def _const_spec(shape):
    nd = len(shape)
    return pl.BlockSpec(shape, lambda *_: (0,) * nd, pipeline_mode=pl.Buffered(1))


def _mixer_kernel(alpha, x_ref, w_in_ref, w_out_ref, conv_w_ref, conv_b_ref, cg_ref, cb_ref,
                  gn_ref, lg_ref, lb_ref, cos_ref, sin_ref, dmask_ref, qdec_ref, kdec_ref,
                  sdec_ref, bd_ref, o_ref, ubuf, state, rbuf):
    t = x_ref.shape[0]

    @pl.when(pl.program_id(1) == 0)
    def _():
        ubuf[0:HALO, :] = jnp.zeros((HALO, CONV_CH), _F32)
        state[...] = jnp.zeros_like(state)

    x = x_ref[...]
    xb = x.astype(_BF)

    ab = _dot(xb, w_in_ref[:, 0:2 * CONV_CH])
    ubuf[HALO:HALO + t, :] = ab[:, :CONV_CH] * _sigmoid(ab[:, CONV_CH:])
    off = HALO - (CONV_K - 1)
    conv_b = conv_b_ref[...]
    blocks = []
    for r0 in range(0, t, CONV_ROWS):
        acc = jnp.broadcast_to(conv_b, (CONV_ROWS, CONV_CH))
        for j in range(CONV_K):
            acc = acc + conv_w_ref[j:j + 1, :] * ubuf[off + r0 + j:off + r0 + j + CONV_ROWS, :]
        blocks.append(acc)
    conv = jnp.concatenate(blocks, axis=0)
    ubuf[0:HALO, :] = ubuf[t:t + HALO, :]
    un = _layer_norm(conv, cg_ref[...], cb_ref[...])
    u_out = un * _sigmoid(un)

    c0 = 2 * CONV_CH
    qk = _dot(xb, w_in_ref[:, c0:c0 + 2 * RET_QK_W])
    v = _dot(xb, w_in_ref[:, c0 + 2 * RET_QK_W:c0 + 2 * RET_QK_W + RET_V_W])
    g = _dot(xb, w_in_ref[:, c0 + 2 * RET_QK_W + RET_V_W:])

    cos_t = cos_ref[...]
    sin_t = sin_ref[...]
    lane = lax.broadcasted_iota(jnp.int32, (1, LANES), 1)
    first_half = (lane % RET_DK) < (RET_DK // 2)

    def rope(z):
        parts = []
        for c in range(0, RET_QK_W, LANES):
            zc = z[:, c:c + LANES]
            up = pltpu.roll(zc, LANES - RET_DK // 2, axis=1)
            dn = pltpu.roll(zc, RET_DK // 2, axis=1)
            parts.append(jnp.where(first_half, up, dn))
        return z * cos_t + jnp.concatenate(parts, axis=1) * sin_t

    q = rope(qk[:, :RET_QK_W])
    k = rope(qk[:, RET_QK_W:])

    lane_qk = lax.broadcasted_iota(jnp.int32, (1, RET_QK_W), 1)
    dmask = dmask_ref[...]
    qdec = qdec_ref[...]
    kdec = kdec_ref[...]
    nr = RET_BLOCK
    for s in range(t // nr):
        rows = slice(s * nr, (s + 1) * nr)
        q_s, k_s, v_s = q[rows], k[rows], v[rows]
        v_b = v_s.astype(_BF)
        kbd = jnp.concatenate(
            [jnp.where(lane_qk // RET_DK == h, k_s, 0.0).astype(_BF) for h in range(RET_HEADS)],
            axis=0)
        sc = lax.dot_general(q_s.astype(_BF), kbd, (((1,), (1,)), ((), ())),
                             preferred_element_type=_F32)
        p = (sc * dmask).astype(_BF)
        intra = jnp.concatenate(
            [_dot(p[:, h * nr:(h + 1) * nr], v_b[:, h * RET_DV:(h + 1) * RET_DV])
             for h in range(RET_HEADS)], axis=1)
        st = state[...]
        cross = _dot((q_s * qdec).astype(_BF), st.astype(_BF))
        kv = lax.dot_general((k_s * kdec).astype(_BF), v_b, (((0,), (0,)), ((), ())),
                             preferred_element_type=_F32)
        state[...] = st * sdec_ref[...] + kv * bd_ref[...]
        rbuf[rows, :] = intra + cross

    r = rbuf[...]
    gate = g * _sigmoid(g)
    gn = gn_ref[...]
    r_parts = []
    for h in range(RET_HEADS):
        hs = slice(h * RET_DV, (h + 1) * RET_DV)
        rh = r[:, hs]
        mu = jnp.mean(rh, axis=-1, keepdims=True)
        d = rh - mu
        var = jnp.mean(d * d, axis=-1, keepdims=True)
        r_parts.append(gate[:, hs] * (d * lax.rsqrt(var + LN_EPS) * gn[:, hs]))

    mixed = jnp.concatenate([u_out] + r_parts, axis=1).astype(_BF)
    mix = _dot(mixed, w_out_ref[...])
    o_ref[...] = _layer_norm(alpha * x + mix, lg_ref[...], lb_ref[...])


def _retention_tables(seq):
    half = RET_DK // 2
    inv_freq = ROPE_BASE ** (-jnp.arange(half, dtype=_F32) / half)
    ang = jnp.arange(seq, dtype=jnp.int32).astype(_F32)[:, None] * inv_freq[None, :]
    cos, sin = jnp.cos(ang), jnp.sin(ang)
    cos_t = jnp.tile(jnp.concatenate([cos, cos], axis=1), (1, RET_HEADS))
    sin_t = jnp.tile(jnp.concatenate([-sin, sin], axis=1), (1, RET_HEADS))

    nr = RET_BLOCK
    log_gamma = jnp.log1p(-(2.0 ** (-5.0 - jnp.arange(RET_HEADS, dtype=_F32))))
    idx = jnp.arange(nr, dtype=_F32)
    dist = jnp.abs(idx[:, None] - idx[None, :])
    visible = (jnp.arange(nr)[None, :] // CHUNK) <= (jnp.arange(nr)[:, None] // CHUNK)
    scale = RET_DK ** -0.5
    dm = jnp.exp(log_gamma[:, None, None] * dist[None]) * visible[None] * scale
    dmask = jnp.transpose(dm, (1, 0, 2)).reshape(nr, RET_HEADS * nr)
    qdec = jnp.repeat(jnp.exp(log_gamma[None, :] * (idx + 1.0)[:, None]), RET_DK, axis=1)
    kdec = jnp.repeat(jnp.exp(log_gamma[None, :] * (nr - 1.0 - idx)[:, None]), RET_DK, axis=1) * scale
    sdec = jnp.broadcast_to(jnp.repeat(jnp.exp(log_gamma * nr), RET_DK)[:, None],
                            (RET_QK_W, RET_V_W))
    bd = (jnp.arange(RET_QK_W)[:, None] // RET_DK == jnp.arange(RET_V_W)[None, :] // RET_DV)
    return cos_t, sin_t, dmask, qdec, kdec, sdec, bd.astype(_F32)


def _mixer(x, w_in, w_out, conv_w, conv_b, cg, cb, gn, lg, lb, tables, alpha):
    b, s, d = x.shape
    t = min(SEQ_TILE, s)
    assert s % t == 0 and t % RET_BLOCK == 0 and RET_BLOCK % CHUNK == 0 and t % CONV_ROWS == 0
    cos_t, sin_t, dmask, qdec, kdec, sdec, bd = tables
    row = lambda a: a.reshape(1, -1)
    conv_w = jnp.pad(conv_w, ((0, HALO - CONV_K), (0, 0)))
    consts = [w_in, w_out, conv_w, row(conv_b), row(cg), row(cb), row(gn), row(lg), row(lb)]
    tail = [dmask, qdec, kdec, sdec, bd]
    in_specs = ([pl.BlockSpec((None, t, d), lambda i, j: (i, j, 0))]
                + [_const_spec(a.shape) for a in consts]
                + [pl.BlockSpec((t, RET_QK_W), lambda i, j: (j, 0))] * 2
                + [_const_spec(a.shape) for a in tail])
    return pl.pallas_call(
        functools.partial(_mixer_kernel, alpha),
        grid=(b, s // t),
        in_specs=in_specs,
        out_specs=pl.BlockSpec((None, t, d), lambda i, j: (i, j, 0)),
        out_shape=jax.ShapeDtypeStruct((b, s, d), _F32),
        scratch_shapes=[pltpu.VMEM((t + HALO, CONV_CH), _F32),
                        pltpu.VMEM((RET_QK_W, RET_V_W), _F32),
                        pltpu.VMEM((t, RET_V_W), _F32)],
        compiler_params=pltpu.CompilerParams(
            dimension_semantics=("arbitrary", "arbitrary"), vmem_limit_bytes=VMEM_LIMIT),
        name="mixer",
    )(x, *consts, cos_t, sin_t, *tail)


def _swiglu_tile(xb, w1_ref, w3_ref, w2_ref):
    d_ff = w1_ref.shape[1]
    acc = None
    for c in range(0, d_ff, FFN_CHUNK):
        a = _dot(xb, w1_ref[:, c:c + FFN_CHUNK])
        g = _dot(xb, w3_ref[:, c:c + FFN_CHUNK])
        h = (a * _sigmoid(a) * g).astype(_BF)
        part = _dot(h, w2_ref[c:c + FFN_CHUNK, :])
        acc = part if acc is None else acc + part
    return acc


def _dense_ffn_kernel(alpha, x_ref, w1_ref, w3_ref, w2_ref, lg_ref, lb_ref, o_ref):
    x = x_ref[...]
    f = _swiglu_tile(x.astype(_BF), w1_ref, w3_ref, w2_ref)
    o_ref[...] = _layer_norm(alpha * x + f, lg_ref[...], lb_ref[...])


def _dense_ffn(x, w1, w3, w2, lg, lb, alpha):
    n, d = x.shape
    tm = min(FFN_TILE, n)
    assert n % tm == 0 and w1.shape[1] % FFN_CHUNK == 0
    consts = [w1, w3, w2, lg.reshape(1, -1), lb.reshape(1, -1)]
    return pl.pallas_call(
        functools.partial(_dense_ffn_kernel, alpha),
        grid=(n // tm,),
        in_specs=[pl.BlockSpec((tm, d), lambda i: (i, 0))] + [_const_spec(a.shape) for a in consts],
        out_specs=pl.BlockSpec((tm, d), lambda i: (i, 0)),
        out_shape=jax.ShapeDtypeStruct((n, d), _F32),
        compiler_params=pltpu.CompilerParams(
            dimension_semantics=("arbitrary",), vmem_limit_bytes=VMEM_LIMIT),
        name="dense_ffn",
    )(x, *consts)


def _router_kernel(x_ref, wr_ref, tri_ref, ri_ref, rf_ref, cnt_ref, carry):
    @pl.when(pl.program_id(0) == 0)
    def _():
        carry[...] = jnp.zeros_like(carry)

    tr = x_ref.shape[0]
    logits = lax.dot_general(wr_ref[...], x_ref[...].astype(_BF), (((1,), (1,)), ((), ())),
                             preferred_element_type=_F32)
    eidx = lax.broadcasted_iota(jnp.int32, (N_EXPERTS, tr), 0)
    m0 = jnp.max(logits, axis=0, keepdims=True)
    e0 = jnp.min(jnp.where(logits == m0, eidx, N_EXPERTS), axis=0, keepdims=True)
    rest = jnp.where(eidx == e0, -jnp.inf, logits)
    m1 = jnp.max(rest, axis=0, keepdims=True)
    e1 = jnp.min(jnp.where(rest == m1, eidx, N_EXPERTS), axis=0, keepdims=True)
    tt = jnp.exp(m1 - m0)
    g0 = 1.0 / (1.0 + tt)
    g1 = tt / (1.0 + tt)

    oh0 = eidx == e0
    oh1 = eidx == e1
    member = jnp.where(oh0 | oh1, 1.0, 0.0)
    before = _dot(member.astype(_BF), tri_ref[...]) + carry[:, 0:1]
    rank0 = jnp.sum(jnp.where(oh0, before, 0.0), axis=0, keepdims=True)
    rank1 = jnp.sum(jnp.where(oh1, before, 0.0), axis=0, keepdims=True)
    carry[...] = carry[...] + jnp.sum(member, axis=1, keepdims=True)

    zi = jnp.zeros((N_EXPERTS - 4, tr), jnp.int32)
    ri_ref[...] = jnp.concatenate(
        [e0, e1, rank0.astype(jnp.int32), rank1.astype(jnp.int32), zi], axis=0)
    rf_ref[...] = jnp.concatenate([g0, g1, jnp.zeros((N_EXPERTS - 2, tr), _F32)], axis=0)
    cnt_ref[...] = carry[...]


def _router(x, wr_t):
    n, d = x.shape
    tr = min(ROUTE_TILE, n)
    assert n % tr == 0
    tri = (jnp.arange(tr)[:, None] < jnp.arange(tr)[None, :]).astype(_BF)
    return pl.pallas_call(
        _router_kernel,
        grid=(n // tr,),
        in_specs=[pl.BlockSpec((tr, d), lambda i: (i, 0)),
                  _const_spec(wr_t.shape), _const_spec(tri.shape)],
        out_specs=[pl.BlockSpec((N_EXPERTS, tr), lambda i: (0, i)),
                   pl.BlockSpec((N_EXPERTS, tr), lambda i: (0, i)),
                   pl.BlockSpec((N_EXPERTS, LANES), lambda i: (0, 0))],
        out_shape=[jax.ShapeDtypeStruct((N_EXPERTS, n), jnp.int32),
                   jax.ShapeDtypeStruct((N_EXPERTS, n), _F32),
                   jax.ShapeDtypeStruct((N_EXPERTS, LANES), _F32)],
        scratch_shapes=[pltpu.VMEM((N_EXPERTS, LANES), _F32)],
        compiler_params=pltpu.CompilerParams(
            dimension_semantics=("arbitrary",), vmem_limit_bytes=VMEM_LIMIT),
        name="router",
    )(x, wr_t, tri)


def _row_copy(src_ref, src_row, dst_ref, dst_row, sem):
    return pltpu.make_async_copy(src_ref.at[pl.ds(src_row, 1)], dst_ref.at[pl.ds(dst_row, 1)], sem)


def _dispatch_kernel(pos_ref, x_ref, xs_in_ref, xs_ref, sem):
    del xs_in_ref
    tm = x_ref.shape[0]

    def start(i, c):
        _row_copy(x_ref, i, xs_ref, pos_ref[0, i], sem.at[0]).start()
        _row_copy(x_ref, i, xs_ref, pos_ref[1, i], sem.at[1]).start()
        return c

    def wait(i, c):
        _row_copy(x_ref, i, xs_ref, pos_ref[0, i], sem.at[0]).wait()
        _row_copy(x_ref, i, xs_ref, pos_ref[1, i], sem.at[1]).wait()
        return c

    lax.fori_loop(0, tm, start, 0)
    lax.fori_loop(0, tm, wait, 0)


def _dispatch(x, pos, n_slots):
    n, d = x.shape
    tm = min(MOVE_TILE, n)
    assert n % tm == 0
    pos3 = pos.reshape(2, n // tm, tm).transpose(1, 0, 2)
    return pl.pallas_call(
        _dispatch_kernel,
        grid=(n // tm,),
        in_specs=[pl.BlockSpec((None, 2, tm), lambda i: (i, 0, 0), memory_space=pltpu.SMEM),
                  pl.BlockSpec((tm, d), lambda i: (i, 0)),
                  pl.BlockSpec(memory_space=pl.ANY)],
        out_specs=pl.BlockSpec(memory_space=pl.ANY),
        out_shape=jax.ShapeDtypeStruct((n_slots, d), _F32),
        scratch_shapes=[pltpu.SemaphoreType.DMA((2,))],
        input_output_aliases={2: 0},
        compiler_params=pltpu.CompilerParams(
            dimension_semantics=("arbitrary",), vmem_limit_bytes=VMEM_LIMIT,
            has_side_effects=True),
        name="dispatch",
    )(pos3, x, jnp.zeros((n_slots, d), _F32))


def _expert_kernel(te_ref, na_ref, xs_ref, w1_ref, w3_ref, w2_ref, ys_ref):
    del te_ref
    active = pl.program_id(0) < na_ref[0]

    @pl.when(active)
    def _():
        ys_ref[...] = _swiglu_tile(xs_ref[...].astype(_BF), w1_ref, w3_ref, w2_ref)

    @pl.when(jnp.logical_not(active))
    def _():
        ys_ref[...] = jnp.zeros_like(ys_ref)


def _expert_ffn(xs, w1, w3, w2, tile_expert, n_active):
    n_slots, d = xs.shape
    tm = FFN_TILE
    d_ff = w1.shape[2]
    n_tiles = n_slots // tm
    w_in_spec = pl.BlockSpec((None, d, d_ff), lambda i, te, na: (te[i], 0, 0),
                             pipeline_mode=pl.Buffered(1))
    w_out_spec = pl.BlockSpec((None, d_ff, d), lambda i, te, na: (te[i], 0, 0),
                              pipeline_mode=pl.Buffered(1))
    return pl.pallas_call(
        _expert_kernel,
        grid_spec=pltpu.PrefetchScalarGridSpec(
            num_scalar_prefetch=2,
            grid=(n_tiles,),
            in_specs=[pl.BlockSpec((tm, d), lambda i, te, na: (jnp.minimum(i, na[0] - 1), 0)),
                      w_in_spec, w_in_spec, w_out_spec],
            out_specs=pl.BlockSpec((tm, d), lambda i, te, na: (i, 0)),
        ),
        out_shape=jax.ShapeDtypeStruct((n_slots, d), _F32),
        compiler_params=pltpu.CompilerParams(
            dimension_semantics=("arbitrary",), vmem_limit_bytes=VMEM_LIMIT),
        name="expert_ffn",
    )(tile_expert, n_active, xs, w1, w3, w2)


def _combine_kernel(alpha, pos_ref, x_ref, gate_ref, lg_ref, lb_ref, ys_ref, o_ref, buf0, buf1, sem):
    tm = x_ref.shape[0]

    def start(i, c):
        _row_copy(ys_ref, pos_ref[0, i], buf0, i, sem.at[0]).start()
        _row_copy(ys_ref, pos_ref[1, i], buf1, i, sem.at[1]).start()
        return c

    def wait(i, c):
        _row_copy(ys_ref, pos_ref[0, i], buf0, i, sem.at[0]).wait()
        _row_copy(ys_ref, pos_ref[1, i], buf1, i, sem.at[1]).wait()
        return c

    lax.fori_loop(0, tm, start, 0)
    lax.fori_loop(0, tm, wait, 0)
    gate = gate_ref[...]
    f = buf0[...] * gate[:, 0:1] + buf1[...] * gate[:, 1:2]
    o_ref[...] = _layer_norm(alpha * x_ref[...] + f, lg_ref[...], lb_ref[...])


def _combine(x, ys, pos, gates, lg, lb, alpha):
    n, d = x.shape
    tm = min(MOVE_TILE, n)
    assert n % tm == 0
    pos3 = pos.reshape(2, n // tm, tm).transpose(1, 0, 2)
    return pl.pallas_call(
        functools.partial(_combine_kernel, alpha),
        grid=(n // tm,),
        in_specs=[pl.BlockSpec((None, 2, tm), lambda i: (i, 0, 0), memory_space=pltpu.SMEM),
                  pl.BlockSpec((tm, d), lambda i: (i, 0)),
                  pl.BlockSpec((tm, 2), lambda i: (i, 0)),
                  _const_spec((1, d)), _const_spec((1, d)),
                  pl.BlockSpec(memory_space=pl.ANY)],
        out_specs=pl.BlockSpec((tm, d), lambda i: (i, 0)),
        out_shape=jax.ShapeDtypeStruct((n, d), _F32),
        scratch_shapes=[pltpu.VMEM((tm, d), _F32), pltpu.VMEM((tm, d), _F32),
                        pltpu.SemaphoreType.DMA((2,))],
        compiler_params=pltpu.CompilerParams(
            dimension_semantics=("arbitrary",), vmem_limit_bytes=VMEM_LIMIT),
        name="combine",
    )(pos3, x, gates, lg.reshape(1, -1), lb.reshape(1, -1), ys)


def _moe_ffn(x, router, w1, w3, w2, lg, lb, alpha):
    n, d = x.shape
    tm = FFN_TILE
    n_tiles = -(-(2 * n) // tm) + N_EXPERTS
    n_slots = n_tiles * tm

    ri, rf, cnt = _router(x, router.T.astype(_BF))
    counts = cnt[:, 0].astype(jnp.int32)
    padded = (counts + tm - 1) // tm * tm
    group_end = jnp.cumsum(padded)
    group_start = group_end - padded
    experts = jnp.arange(N_EXPERTS, dtype=jnp.int32)[:, None]

    def slot(e, rank):
        return jnp.sum(jnp.where(e[None, :] == experts, group_start[:, None], 0), axis=0) + rank

    pos = jnp.stack([slot(ri[0], ri[2]), slot(ri[1], ri[3])])
    tile_start = jnp.arange(n_tiles, dtype=jnp.int32) * tm
    tile_expert = jnp.minimum(
        jnp.sum(tile_start[:, None] >= group_end[None, :], axis=1), N_EXPERTS - 1).astype(jnp.int32)
    n_active = (group_end[-1:] // tm).astype(jnp.int32)

    xs = _dispatch(x, pos, n_slots)
    ys = _expert_ffn(xs, w1, w3, w2, tile_expert, n_active)
    return _combine(x, ys, pos, rf[:2].T, lg, lb, alpha)


def kernel(x, w_in, w_out, conv_w, conv_b, conv_ln_g, conv_ln_b, ret_gn_g, ln1_g, ln1_b, ln2_g,
           ln2_b, dense_w1, dense_w3, dense_w2, moe_router, moe_w1, moe_w3, moe_w2):
    b, s, d = x.shape
    depth = w_in.shape[0]
    alpha = (2.0 * depth) ** 0.25
    tables = _retention_tables(s)
    for l in range(depth):
        x = _mixer(x, w_in[l].astype(_BF), w_out[l].astype(_BF), conv_w[l], conv_b[l],
                   conv_ln_g[l], conv_ln_b[l], ret_gn_g[l], ln1_g[l], ln1_b[l], tables, alpha)
        h = x.reshape(b * s, d)
        i = l // 2
        if l % 2 == 0:
            h = _dense_ffn(h, dense_w1[i].astype(_BF), dense_w3[i].astype(_BF),
                           dense_w2[i].astype(_BF), ln2_g[l], ln2_b[l], alpha)
        else:
            h = _moe_ffn(h, moe_router[i], moe_w1[i].astype(_BF), moe_w3[i].astype(_BF),
                         moe_w2[i].astype(_BF), ln2_g[l], ln2_b[l], alpha)
        x = h.reshape(b, s, d)
    return x
```

```python
import functools

import jax
import jax.numpy as jnp
from jax import lax
from jax.experimental import pallas as pl
from jax.experimental.pallas import tpu as pltpu

CHUNK = 64
CONV_CH = 512
CONV_K = 31
RET_HEADS = 4
RET_DK = 64
RET_DV = 128
RET_QK_W = RET_HEADS * RET_DK
RET_V_W = RET_HEADS * RET_DV
ROPE_BASE = 10000.0
N_EXPERTS = 8
LN_EPS = 1e-5

LANES = 128
SUBLANES = 8
SEQ_TILE = 512
RET_BLOCK = 128
CONV_ROWS = 32
HALO = 32
FFN_TILE = 512
FFN_CHUNK = 512
ROUTE_TILE = 1024
MOVE_TILE = 512
VMEM_LIMIT = 56 * 1024 * 1024

_BF = jnp.bfloat16
_F32 = jnp.float32


def _dot(a, b):
    return jnp.dot(a, b, preferred_element_type=_F32)


def _layer_norm(v, g, b):
    mu = jnp.mean(v, axis=-1, keepdims=True)
    d = v - mu
    var = jnp.mean(d * d, axis=-1, keepdims=True)
    return d * lax.rsqrt(var + LN_EPS) * g + b


def _sigmoid(v):
    return 1.0 / (1.0 + jnp.exp(-v))


def _const_spec(shape):
    nd = len(shape)
    return pl.BlockSpec(shape, lambda *_: (0,) * nd, pipeline_mode=pl.Buffered(1))


def _mixer_kernel(alpha, x_ref, w_in_ref, w_out_ref, conv_w_ref, conv_b_ref, cg_ref, cb_ref,
                  gn_ref, lg_ref, lb_ref, cos_ref, sin_ref, dmask_ref, qdec_ref, kdec_ref,
                  sdec_ref, bd_ref, o_ref, ush, state, rbuf):
    t = x_ref.shape[0]
    ubuf = ush.at[0]

    @pl.when(pl.program_id(1) == 0)
    def _():
        ubuf[0:HALO, :] = jnp.zeros((HALO, CONV_CH), _F32)
        state[...] = jnp.zeros_like(state)

    x = x_ref[...]
    xb = x.astype(_BF)

    ab = _dot(xb, w_in_ref[:, 0:2 * CONV_CH])
    ubuf[HALO:HALO + t, :] = ab[:, :CONV_CH] * _sigmoid(ab[:, CONV_CH:])
    span = t + HALO - SUBLANES
    for r in range(1, SUBLANES):
        ush[r, 0:span, :] = ubuf[r:r + span, :]
    off = HALO - (CONV_K - 1)
    conv_b = conv_b_ref[...]
    blocks = []
    for r0 in range(0, t, CONV_ROWS):
        acc = jnp.broadcast_to(conv_b, (CONV_ROWS, CONV_CH))
        for r in range(SUBLANES):
            taps = [j for j in range(CONV_K) if (j + off) % SUBLANES == r]
            reach = max((j + off) // SUBLANES for j in taps) * SUBLANES
            seg = ush[r, r0:r0 + reach + CONV_ROWS, :]
            for j in taps:
                a = (j + off) // SUBLANES * SUBLANES
                acc = acc + conv_w_ref[j:j + 1, :] * seg[a:a + CONV_ROWS]
        blocks.append(acc)
    conv = jnp.concatenate(blocks, axis=0)
    ubuf[0:HALO, :] = ubuf[t:t + HALO, :]
    un = _layer_norm(conv, cg_ref[...], cb_ref[...])
    u_out = un * _sigmoid(un)

    c0 = 2 * CONV_CH
    qk = _dot(xb, w_in_ref[:, c0:c0 + 2 * RET_QK_W])
    v = _dot(xb, w_in_ref[:, c0 + 2 * RET_QK_W:c0 + 2 * RET_QK_W + RET_V_W])
    g = _dot(xb, w_in_ref[:, c0 + 2 * RET_QK_W + RET_V_W:])

    cos_t = cos_ref[...]
    sin_t = sin_ref[...]
    lane = lax.broadcasted_iota(jnp.int32, (1, LANES), 1)
    first_half = (lane % RET_DK) < (RET_DK // 2)

    def rope(z):
        parts = []
        for c in range(0, RET_QK_W, LANES):
            zc = z[:, c:c + LANES]
            up = pltpu.roll(zc, LANES - RET_DK // 2, axis=1)
            dn = pltpu.roll(zc, RET_DK // 2, axis=1)
            parts.append(jnp.where(first_half, up, dn))
        return z * cos_t + jnp.concatenate(parts, axis=1) * sin_t

    q = rope(qk[:, :RET_QK_W])
    k = rope(qk[:, RET_QK_W:])

    lane_qk = lax.broadcasted_iota(jnp.int32, (1, RET_QK_W), 1)
    dmask = dmask_ref[...]
    qdec = qdec_ref[...]
    kdec = kdec_ref[...]
    nr = RET_BLOCK
    for s in range(t // nr):
        rows = slice(s * nr, (s + 1) * nr)
        q_s, k_s, v_s = q[rows], k[rows], v[rows]
        v_b = v_s.astype(_BF)
        kbd = jnp.concatenate(
            [jnp.where(lane_qk // RET_DK == h, k_s, 0.0).astype(_BF) for h in range(RET_HEADS)],
            axis=0)
        sc = lax.dot_general(q_s.astype(_BF), kbd, (((1,), (1,)), ((), ())),
                             preferred_element_type=_F32)
        p = (sc * dmask).astype(_BF)
        intra = jnp.concatenate(
            [_dot(p[:, h * nr:(h + 1) * nr], v_b[:, h * RET_DV:(h + 1) * RET_DV])
             for h in range(RET_HEADS)], axis=1)
        st = state[...]
        cross = _dot((q_s * qdec).astype(_BF), st.astype(_BF))
        kv = lax.dot_general((k_s * kdec).astype(_BF), v_b, (((0,), (0,)), ((), ())),
                             preferred_element_type=_F32)
        state[...] = st * sdec_ref[...] + kv * bd_ref[...]
        rbuf[rows, :] = intra + cross

    r = rbuf[...]
    gate = g * _sigmoid(g)
    gn = gn_ref[...]
    r_parts = []
    for h in range(RET_HEADS):
        hs = slice(h * RET_DV, (h + 1) * RET_DV)
        rh = r[:, hs]
        mu = jnp.mean(rh, axis=-1, keepdims=True)
        d = rh - mu
        var = jnp.mean(d * d, axis=-1, keepdims=True)
        r_parts.append(gate[:, hs] * (d * lax.rsqrt(var + LN_EPS) * gn[:, hs]))

    mixed = jnp.concatenate([u_out] + r_parts, axis=1).astype(_BF)
    mix = _dot(mixed, w_out_ref[...])
    o_ref[...] = _layer_norm(alpha * x + mix, lg_ref[...], lb_ref[...])


def _retention_tables(seq):
    half = RET_DK // 2
    inv_freq = ROPE_BASE ** (-jnp.arange(half, dtype=_F32) / half)
    ang = jnp.arange(seq, dtype=jnp.int32).astype(_F32)[:, None] * inv_freq[None, :]
    cos, sin = jnp.cos(ang), jnp.sin(ang)
    cos_t = jnp.tile(jnp.concatenate([cos, cos], axis=1), (1, RET_HEADS))
    sin_t = jnp.tile(jnp.concatenate([-sin, sin], axis=1), (1, RET_HEADS))

    nr = RET_BLOCK
    log_gamma = jnp.log1p(-(2.0 ** (-5.0 - jnp.arange(RET_HEADS, dtype=_F32))))
    idx = jnp.arange(nr, dtype=_F32)
    dist = jnp.abs(idx[:, None] - idx[None, :])
    visible = (jnp.arange(nr)[None, :] // CHUNK) <= (jnp.arange(nr)[:, None] // CHUNK)
    scale = RET_DK ** -0.5
    dm = jnp.exp(log_gamma[:, None, None] * dist[None]) * visible[None] * scale
    dmask = jnp.transpose(dm, (1, 0, 2)).reshape(nr, RET_HEADS * nr)
    qdec = jnp.repeat(jnp.exp(log_gamma[None, :] * (idx + 1.0)[:, None]), RET_DK, axis=1)
    kdec = jnp.repeat(jnp.exp(log_gamma[None, :] * (nr - 1.0 - idx)[:, None]), RET_DK, axis=1) * scale
    sdec = jnp.broadcast_to(jnp.repeat(jnp.exp(log_gamma * nr), RET_DK)[:, None],
                            (RET_QK_W, RET_V_W))
    bd = (jnp.arange(RET_QK_W)[:, None] // RET_DK == jnp.arange(RET_V_W)[None, :] // RET_DV)
    return cos_t, sin_t, dmask, qdec, kdec, sdec, bd.astype(_F32)


def _mixer(x, w_in, w_out, conv_w, conv_b, cg, cb, gn, lg, lb, tables, alpha):
    b, s, d = x.shape
    t = min(SEQ_TILE, s)
    assert s % t == 0 and t % RET_BLOCK == 0 and RET_BLOCK % CHUNK == 0 and t % CONV_ROWS == 0
    cos_t, sin_t, dmask, qdec, kdec, sdec, bd = tables
    row = lambda a: a.reshape(1, -1)
    conv_w = jnp.pad(conv_w, ((0, HALO - CONV_K), (0, 0)))
    consts = [w_in, w_out, conv_w, row(conv_b), row(cg), row(cb), row(gn), row(lg), row(lb)]
    tail = [dmask, qdec, kdec, sdec, bd]
    in_specs = ([pl.BlockSpec((None, t, d), lambda i, j: (i, j, 0))]
                + [_const_spec(a.shape) for a in consts]
                + [pl.BlockSpec((t, RET_QK_W), lambda i, j: (j, 0))] * 2
                + [_const_spec(a.shape) for a in tail])
    return pl.pallas_call(
        functools.partial(_mixer_kernel, alpha),
        grid=(b, s // t),
        in_specs=in_specs,
        out_specs=pl.BlockSpec((None, t, d), lambda i, j: (i, j, 0)),
        out_shape=jax.ShapeDtypeStruct((b, s, d), _F32),
        scratch_shapes=[pltpu.VMEM((SUBLANES, t + HALO, CONV_CH), _F32),
                        pltpu.VMEM((RET_QK_W, RET_V_W), _F32),
                        pltpu.VMEM((t, RET_V_W), _F32)],
        compiler_params=pltpu.CompilerParams(
            dimension_semantics=("arbitrary", "arbitrary"), vmem_limit_bytes=VMEM_LIMIT),
        name="mixer",
    )(x, *consts, cos_t, sin_t, *tail)


def _swiglu_tile(xb, w1_ref, w3_ref, w2_ref):
    d_ff = w1_ref.shape[1]
    acc = None
    for c in range(0, d_ff, FFN_CHUNK):
        a = _dot(xb, w1_ref[:, c:c + FFN_CHUNK])
        g = _dot(xb, w3_ref[:, c:c + FFN_CHUNK])
        h = (a * _sigmoid(a) * g).astype(_BF)
        part = _dot(h, w2_ref[c:c + FFN_CHUNK, :])
        acc = part if acc is None else acc + part
    return acc


def _dense_ffn_kernel(alpha, x_ref, w1_ref, w3_ref, w2_ref, lg_ref, lb_ref, o_ref):
    x = x_ref[...]
    f = _swiglu_tile(x.astype(_BF), w1_ref, w3_ref, w2_ref)
    o_ref[...] = _layer_norm(alpha * x + f, lg_ref[...], lb_ref[...])


def _dense_ffn(x, w1, w3, w2, lg, lb, alpha):
    n, d = x.shape
    tm = min(FFN_TILE, n)
    assert n % tm == 0 and w1.shape[1] % FFN_CHUNK == 0
    consts = [w1, w3, w2, lg.reshape(1, -1), lb.reshape(1, -1)]
    return pl.pallas_call(
        functools.partial(_dense_ffn_kernel, alpha),
        grid=(n // tm,),
        in_specs=[pl.BlockSpec((tm, d), lambda i: (i, 0))] + [_const_spec(a.shape) for a in consts],
        out_specs=pl.BlockSpec((tm, d), lambda i: (i, 0)),
        out_shape=jax.ShapeDtypeStruct((n, d), _F32),
        compiler_params=pltpu.CompilerParams(
            dimension_semantics=("arbitrary",), vmem_limit_bytes=VMEM_LIMIT),
        name="dense_ffn",
    )(x, *consts)


def _router_kernel(x_ref, wr_ref, tri_ref, ri_ref, rf_ref, cnt_ref, carry):
    @pl.when(pl.program_id(0) == 0)
    def _():
        carry[...] = jnp.zeros_like(carry)

    tr = x_ref.shape[0]
    logits = lax.dot_general(wr_ref[...], x_ref[...].astype(_BF), (((1,), (1,)), ((), ())),
                             preferred_element_type=_F32)
    eidx = lax.broadcasted_iota(jnp.int32, (N_EXPERTS, tr), 0)
    m0 = jnp.max(logits, axis=0, keepdims=True)
    e0 = jnp.min(jnp.where(logits == m0, eidx, N_EXPERTS), axis=0, keepdims=True)
    rest = jnp.where(eidx == e0, -jnp.inf, logits)
    m1 = jnp.max(rest, axis=0, keepdims=True)
    e1 = jnp.min(jnp.where(rest == m1, eidx, N_EXPERTS), axis=0, keepdims=True)
    tt = jnp.exp(m1 - m0)
    g0 = 1.0 / (1.0 + tt)
    g1 = tt / (1.0 + tt)

    oh0 = eidx == e0
    oh1 = eidx == e1
    member = jnp.where(oh0 | oh1, 1.0, 0.0)
    before = _dot(member.astype(_BF), tri_ref[...]) + carry[:, 0:1]
    rank0 = jnp.sum(jnp.where(oh0, before, 0.0), axis=0, keepdims=True)
    rank1 = jnp.sum(jnp.where(oh1, before, 0.0), axis=0, keepdims=True)
    carry[...] = carry[...] + jnp.sum(member, axis=1, keepdims=True)

    zi = jnp.zeros((N_EXPERTS - 4, tr), jnp.int32)
    ri_ref[...] = jnp.concatenate(
        [e0, e1, rank0.astype(jnp.int32), rank1.astype(jnp.int32), zi], axis=0)
    rf_ref[...] = jnp.concatenate([g0, g1, jnp.zeros((N_EXPERTS - 2, tr), _F32)], axis=0)
    cnt_ref[...] = carry[...]


def _router(x, wr_t):
    n, d = x.shape
    tr = min(ROUTE_TILE, n)
    assert n % tr == 0
    tri = (jnp.arange(tr)[:, None] < jnp.arange(tr)[None, :]).astype(_BF)
    return pl.pallas_call(
        _router_kernel,
        grid=(n // tr,),
        in_specs=[pl.BlockSpec((tr, d), lambda i: (i, 0)),
                  _const_spec(wr_t.shape), _const_spec(tri.shape)],
        out_specs=[pl.BlockSpec((N_EXPERTS, tr), lambda i: (0, i)),
                   pl.BlockSpec((N_EXPERTS, tr), lambda i: (0, i)),
                   pl.BlockSpec((N_EXPERTS, LANES), lambda i: (0, 0))],
        out_shape=[jax.ShapeDtypeStruct((N_EXPERTS, n), jnp.int32),
                   jax.ShapeDtypeStruct((N_EXPERTS, n), _F32),
                   jax.ShapeDtypeStruct((N_EXPERTS, LANES), _F32)],
        scratch_shapes=[pltpu.VMEM((N_EXPERTS, LANES), _F32)],
        compiler_params=pltpu.CompilerParams(
            dimension_semantics=("arbitrary",), vmem_limit_bytes=VMEM_LIMIT),
        name="router",
    )(x, wr_t, tri)


def _for_each_row(n_rows, fn):
    def body(c, carry):
        for u in range(SUBLANES):
            fn(c, u)
        return carry

    lax.fori_loop(0, n_rows // SUBLANES, body, 0)


def _wait_rows(hbm_ref, n_rows, sem):
    rows = hbm_ref.at[pl.ds(0, n_rows)]
    pltpu.make_async_copy(rows, rows, sem).wait()


def _dispatch_kernel(pad_ref, pos_ref, x_ref, xs_ref, zbuf, sem):
    tm = x_ref.shape[0] * SUBLANES

    @pl.when(pl.program_id(0) == 0)
    def _():
        zbuf[...] = jnp.zeros_like(zbuf)
        for e in range(pad_ref.shape[0]):
            fill = pltpu.make_async_copy(
                zbuf, xs_ref.at[pl.ds(pl.multiple_of(pad_ref[e], SUBLANES), zbuf.shape[0])],
                sem.at[0])
            fill.start()
            fill.wait()

    def start(c, u):
        i = c * SUBLANES + u
        src = x_ref.at[c, pl.ds(u, 1)]
        pltpu.make_async_copy(src, xs_ref.at[pl.ds(pos_ref[0, i], 1)], sem.at[0]).start(priority=0)
        pltpu.make_async_copy(src, xs_ref.at[pl.ds(pos_ref[0, tm + i], 1)], sem.at[1]).start(priority=1)

    _for_each_row(tm, start)
    _wait_rows(xs_ref, tm, sem.at[0])
    _wait_rows(xs_ref, tm, sem.at[1])


def _move_tile(n):
    tm = min(MOVE_TILE, n)
    assert n % tm == 0 and tm % SUBLANES == 0
    return tm


def _tile_positions(pos, tm):
    n = pos.shape[1]
    return pos.reshape(2, n // tm, tm).transpose(1, 0, 2).reshape(n // tm, 1, 2 * tm)


def _dispatch(x, pos, pad_start, n_slots, pad_rows):
    n, d = x.shape
    tm = _move_tile(n)
    return pl.pallas_call(
        _dispatch_kernel,
        grid_spec=pltpu.PrefetchScalarGridSpec(
            num_scalar_prefetch=1,
            grid=(n // tm,),
            in_specs=[pl.BlockSpec((None, 1, 2 * tm), lambda i, pad: (i, 0, 0),
                                   memory_space=pltpu.SMEM),
                      pl.BlockSpec((tm // SUBLANES, SUBLANES, d), lambda i, pad: (i, 0, 0))],
            out_specs=pl.BlockSpec(memory_space=pl.ANY),
            scratch_shapes=[pltpu.VMEM((pad_rows, d), _F32), pltpu.SemaphoreType.DMA((2,))],
        ),
        out_shape=jax.ShapeDtypeStruct((n_slots, d), _F32),
        compiler_params=pltpu.CompilerParams(
            dimension_semantics=("arbitrary",), vmem_limit_bytes=VMEM_LIMIT,
            has_side_effects=True),
        name="dispatch",
    )(pad_start, _tile_positions(pos, tm), x.reshape(n // SUBLANES, SUBLANES, d))


def _expert_kernel(te_ref, na_ref, xs_ref, w1_ref, w3_ref, w2_ref, ys_ref):
    del te_ref
    active = pl.program_id(0) < na_ref[0]

    @pl.when(active)
    def _():
        ys_ref[...] = _swiglu_tile(xs_ref[...].astype(_BF), w1_ref, w3_ref, w2_ref)

    @pl.when(jnp.logical_not(active))
    def _():
        ys_ref[...] = jnp.zeros_like(ys_ref)


def _expert_ffn(xs, w1, w3, w2, tile_expert, n_active):
    d = xs.shape[1]
    tm = FFN_TILE
    d_ff = w1.shape[2]
    n_tiles = tile_expert.shape[0]
    n_slots = n_tiles * tm
    w_in_spec = pl.BlockSpec((None, d, d_ff), lambda i, te, na: (te[i], 0, 0),
                             pipeline_mode=pl.Buffered(1))
    w_out_spec = pl.BlockSpec((None, d_ff, d), lambda i, te, na: (te[i], 0, 0),
                              pipeline_mode=pl.Buffered(1))
    return pl.pallas_call(
        _expert_kernel,
        grid_spec=pltpu.PrefetchScalarGridSpec(
            num_scalar_prefetch=2,
            grid=(n_tiles,),
            in_specs=[pl.BlockSpec((tm, d), lambda i, te, na: (jnp.minimum(i, na[0] - 1), 0)),
                      w_in_spec, w_in_spec, w_out_spec],
            out_specs=pl.BlockSpec((tm, d), lambda i, te, na: (i, 0)),
        ),
        out_shape=jax.ShapeDtypeStruct((n_slots, d), _F32),
        compiler_params=pltpu.CompilerParams(
            dimension_semantics=("arbitrary",), vmem_limit_bytes=VMEM_LIMIT),
        name="expert_ffn",
    )(tile_expert, n_active, xs, w1, w3, w2)


def _combine_kernel(alpha, pos_ref, x_ref, gate_ref, lg_ref, lb_ref, ys_ref, o_ref, buf0, buf1, sem):
    tm, d = x_ref.shape

    def start(c, u):
        i = c * SUBLANES + u
        pltpu.make_async_copy(ys_ref.at[pl.ds(pos_ref[0, i], 1)], buf0.at[c, pl.ds(u, 1)],
                              sem.at[0]).start(priority=0)
        pltpu.make_async_copy(ys_ref.at[pl.ds(pos_ref[0, tm + i], 1)], buf1.at[c, pl.ds(u, 1)],
                              sem.at[1]).start(priority=1)

    _for_each_row(tm, start)
    _wait_rows(ys_ref, tm, sem.at[0])
    _wait_rows(ys_ref, tm, sem.at[1])
    gate = gate_ref[...]
    f = buf0[...].reshape(tm, d) * gate[:, 0:1] + buf1[...].reshape(tm, d) * gate[:, 1:2]
    o_ref[...] = _layer_norm(alpha * x_ref[...] + f, lg_ref[...], lb_ref[...])


def _combine(x, ys, pos, gates, lg, lb, alpha):
    n, d = x.shape
    tm = _move_tile(n)
    buf = pltpu.VMEM((tm // SUBLANES, SUBLANES, d), _F32)
    return pl.pallas_call(
        functools.partial(_combine_kernel, alpha),
        grid=(n // tm,),
        in_specs=[pl.BlockSpec((None, 1, 2 * tm), lambda i: (i, 0, 0), memory_space=pltpu.SMEM),
                  pl.BlockSpec((tm, d), lambda i: (i, 0)),
                  pl.BlockSpec((tm, 2), lambda i: (i, 0)),
                  _const_spec((1, d)), _const_spec((1, d)),
                  pl.BlockSpec(memory_space=pl.ANY)],
        out_specs=pl.BlockSpec((tm, d), lambda i: (i, 0)),
        out_shape=jax.ShapeDtypeStruct((n, d), _F32),
        scratch_shapes=[buf, buf, pltpu.SemaphoreType.DMA((2,))],
        compiler_params=pltpu.CompilerParams(
            dimension_semantics=("arbitrary",), vmem_limit_bytes=VMEM_LIMIT),
        name="combine",
    )(_tile_positions(pos, tm), x, gates, lg.reshape(1, -1), lb.reshape(1, -1), ys)


def _moe_ffn(x, router, w1, w3, w2, lg, lb, alpha):
    n, d = x.shape
    tm = FFN_TILE
    n_tiles = -(-(2 * n) // tm) + N_EXPERTS
    n_slots = n_tiles * tm

    ri, rf, cnt = _router(x, router.T.astype(_BF))
    counts = cnt[:, 0].astype(jnp.int32)
    padded = (counts + tm - 1) // tm * tm
    group_end = jnp.cumsum(padded)
    group_start = group_end - padded
    experts = jnp.arange(N_EXPERTS, dtype=jnp.int32)[:, None]

    def slot(e, rank):
        return jnp.sum(jnp.where(e[None, :] == experts, group_start[:, None], 0), axis=0) + rank

    pos = jnp.stack([slot(ri[0], ri[2]), slot(ri[1], ri[3])])
    tile_start = jnp.arange(n_tiles, dtype=jnp.int32) * tm
    tile_expert = jnp.minimum(
        jnp.sum(tile_start[:, None] >= group_end[None, :], axis=1), N_EXPERTS - 1).astype(jnp.int32)
    n_active = (group_end[-1:] // tm).astype(jnp.int32)

    zrows = tm + SUBLANES
    pad_start = (group_start + counts) // SUBLANES * SUBLANES
    tail = jnp.minimum(pad_start[-1] + tm * jnp.arange(1, N_EXPERTS + 2, dtype=jnp.int32),
                       n_slots + tm - zrows)
    xs = _dispatch(x, pos, jnp.concatenate([pad_start, tail]), n_slots + tm, zrows)
    ys = _expert_ffn(xs, w1, w3, w2, tile_expert, n_active)
    return _combine(x, ys, pos, rf[:2].T, lg, lb, alpha)


def kernel(x, w_in, w_out, conv_w, conv_b, conv_ln_g, conv_ln_b, ret_gn_g, ln1_g, ln1_b, ln2_g,
           ln2_b, dense_w1, dense_w3, dense_w2, moe_router, moe_w1, moe_w3, moe_w2):
    b, s, d = x.shape
    depth = w_in.shape[0]
    alpha = (2.0 * depth) ** 0.25
    tables = _retention_tables(s)
    for l in range(depth):
        x = _mixer(x, w_in[l].astype(_BF), w_out[l].astype(_BF), conv_w[l], conv_b[l],
                   conv_ln_g[l], conv_ln_b[l], ret_gn_g[l], ln1_g[l], ln1_b[l], tables, alpha)
        h = x.reshape(b * s, d)
        i = l // 2
        if l % 2 == 0:
            h = _dense_ffn(h, dense_w1[i].astype(_BF), dense_w3[i].astype(_BF),
                           dense_w2[i].astype(_BF), ln2_g[l], ln2_b[l], alpha)
        else:
            h = _moe_ffn(h, moe_router[i], moe_w1[i].astype(_BF), moe_w3[i].astype(_BF),
                         moe_w2[i].astype(_BF), ln2_g[l], ln2_b[l], alpha)
        x = h.reshape(b, s, d)
    return x
```

```python
import functools

import jax
import jax.numpy as jnp
from jax import lax
from jax.experimental import pallas as pl
from jax.experimental.pallas import tpu as pltpu
from jax.experimental.pallas import tpu_sc as plsc

CHUNK = 64
CONV_CH = 512
CONV_K = 31
RET_HEADS = 4
RET_DK = 64
RET_DV = 128
RET_QK_W = RET_HEADS * RET_DK
RET_V_W = RET_HEADS * RET_DV
ROPE_BASE = 10000.0
N_EXPERTS = 8
LN_EPS = 1e-5

LANES = 128
SUBLANES = 8
SEQ_TILE = 512
MIXER_STREAMS = 1
MIXER_PHASES = 4
RET_BLOCK = 128
CONV_ROWS = 32
HALO = 32
FFN_TILE = 512
FFN_CHUNK = 512
ROUTE_TILE = 1024
MOVE_TILE = 512
SC_CORES = 2
SC_SUBCORES = 16
SC_ROWS = 32
VMEM_LIMIT = 56 * 1024 * 1024

_BF = jnp.bfloat16
_F32 = jnp.float32


def _dot(a, b):
    return jnp.dot(a, b, preferred_element_type=_F32)


def _layer_norm(v, g, b):
    mu = jnp.mean(v, axis=-1, keepdims=True)
    d = v - mu
    var = jnp.mean(d * d, axis=-1, keepdims=True)
    return d * lax.rsqrt(var + LN_EPS) * g + b


def _sigmoid(v):
    return 1.0 / (1.0 + jnp.exp(-v))


def _const_spec(shape):
    nd = len(shape)
    return pl.BlockSpec(shape, lambda *_: (0,) * nd, pipeline_mode=pl.Buffered(1))


def _mixer_kernel(alpha, x_ref, *refs):
    *shared, o_ref, ush, state, rbuf = refs

    @pl.when(pl.program_id(1) == 0)
    def _():
        ush[:, 0, 0:HALO, :] = jnp.zeros((ush.shape[0], HALO, CONV_CH), _F32)
        state[...] = jnp.zeros_like(state)

    ns = x_ref.shape[0]
    streams = [_mixer_stream(alpha, x_ref.at[p], *shared, o_ref.at[p], ush.at[p], state.at[p],
                             rbuf.at[p]) for p in range(ns)]
    for step in range(MIXER_PHASES + ns - 1):
        for p in range(ns):
            if 0 <= step - p < MIXER_PHASES:
                next(streams[p], None)


def _mixer_stream(alpha, x_ref, w_in_ref, w_out_ref, conv_w_ref, conv_b_ref, cg_ref, cb_ref,
                  gn_ref, lg_ref, lb_ref, cos_ref, sin_ref, dmask_ref, qdec_ref, kdec_ref,
                  sdec_ref, bd_ref, o_ref, ush, state, rbuf):
    t = x_ref.shape[0]
    ubuf = ush.at[0]
    x = x_ref[...]
    xb = x.astype(_BF)

    c0 = 2 * CONV_CH
    ab = _dot(xb, w_in_ref[:, 0:c0])
    qk = _dot(xb, w_in_ref[:, c0:c0 + 2 * RET_QK_W])
    v = _dot(xb, w_in_ref[:, c0 + 2 * RET_QK_W:c0 + 2 * RET_QK_W + RET_V_W])
    g = _dot(xb, w_in_ref[:, c0 + 2 * RET_QK_W + RET_V_W:])
    yield

    ubuf[HALO:HALO + t, :] = ab[:, :CONV_CH] * _sigmoid(ab[:, CONV_CH:])
    span = t + HALO - SUBLANES
    for r in range(1, SUBLANES):
        ush[r, 0:span, :] = ubuf[r:r + span, :]
    off = HALO - (CONV_K - 1)
    conv_b = conv_b_ref[...]
    blocks = []
    for r0 in range(0, t, CONV_ROWS):
        acc = jnp.broadcast_to(conv_b, (CONV_ROWS, CONV_CH))
        for r in range(SUBLANES):
            taps = [j for j in range(CONV_K) if (j + off) % SUBLANES == r]
            reach = max((j + off) // SUBLANES for j in taps) * SUBLANES
            seg = ush[r, r0:r0 + reach + CONV_ROWS, :]
            for j in taps:
                a = (j + off) // SUBLANES * SUBLANES
                acc = acc + conv_w_ref[j:j + 1, :] * seg[a:a + CONV_ROWS]
        blocks.append(acc)
    conv = jnp.concatenate(blocks, axis=0)
    ubuf[0:HALO, :] = ubuf[t:t + HALO, :]
    un = _layer_norm(conv, cg_ref[...], cb_ref[...])
    u_out = un * _sigmoid(un)

    yield

    cos_t = cos_ref[...]
    sin_t = sin_ref[...]
    lane = lax.broadcasted_iota(jnp.int32, (1, LANES), 1)
    first_half = (lane % RET_DK) < (RET_DK // 2)

    def rope(z):
        parts = []
        for c in range(0, RET_QK_W, LANES):
            zc = z[:, c:c + LANES]
            up = pltpu.roll(zc, LANES - RET_DK // 2, axis=1)
            dn = pltpu.roll(zc, RET_DK // 2, axis=1)
            parts.append(jnp.where(first_half, up, dn))
        return z * cos_t + jnp.concatenate(parts, axis=1) * sin_t

    q = rope(qk[:, :RET_QK_W])
    k = rope(qk[:, RET_QK_W:])

    lane_qk = lax.broadcasted_iota(jnp.int32, (1, RET_QK_W), 1)
    dmask = dmask_ref[...]
    qdec = qdec_ref[...]
    kdec = kdec_ref[...]
    nr = RET_BLOCK
    for s in range(t // nr):
        rows = slice(s * nr, (s + 1) * nr)
        q_s, k_s, v_s = q[rows], k[rows], v[rows]
        v_b = v_s.astype(_BF)
        kbd = jnp.concatenate(
            [jnp.where(lane_qk // RET_DK == h, k_s, 0.0).astype(_BF) for h in range(RET_HEADS)],
            axis=0)
        sc = lax.dot_general(q_s.astype(_BF), kbd, (((1,), (1,)), ((), ())),
                             preferred_element_type=_F32)
        p = (sc * dmask).astype(_BF)
        intra = jnp.concatenate(
            [_dot(p[:, h * nr:(h + 1) * nr], v_b[:, h * RET_DV:(h + 1) * RET_DV])
             for h in range(RET_HEADS)], axis=1)
        st = state[...]
        cross = _dot((q_s * qdec).astype(_BF), st.astype(_BF))
        kv = lax.dot_general((k_s * kdec).astype(_BF), v_b, (((0,), (0,)), ((), ())),
                             preferred_element_type=_F32)
        state[...] = st * sdec_ref[...] + kv * bd_ref[...]
        rbuf[rows, :] = intra + cross

    yield

    r = rbuf[...]
    gate = g * _sigmoid(g)
    gn = gn_ref[...]
    r_parts = []
    for h in range(RET_HEADS):
        hs = slice(h * RET_DV, (h + 1) * RET_DV)
        rh = r[:, hs]
        mu = jnp.mean(rh, axis=-1, keepdims=True)
        d = rh - mu
        var = jnp.mean(d * d, axis=-1, keepdims=True)
        r_parts.append(gate[:, hs] * (d * lax.rsqrt(var + LN_EPS) * gn[:, hs]))

    mixed = jnp.concatenate([u_out] + r_parts, axis=1).astype(_BF)
    mix = _dot(mixed, w_out_ref[...])
    o_ref[...] = _layer_norm(alpha * x + mix, lg_ref[...], lb_ref[...])


def _retention_tables(seq):
    half = RET_DK // 2
    inv_freq = ROPE_BASE ** (-jnp.arange(half, dtype=_F32) / half)
    ang = jnp.arange(seq, dtype=jnp.int32).astype(_F32)[:, None] * inv_freq[None, :]
    cos, sin = jnp.cos(ang), jnp.sin(ang)
    cos_t = jnp.tile(jnp.concatenate([cos, cos], axis=1), (1, RET_HEADS))
    sin_t = jnp.tile(jnp.concatenate([-sin, sin], axis=1), (1, RET_HEADS))

    nr = RET_BLOCK
    log_gamma = jnp.log1p(-(2.0 ** (-5.0 - jnp.arange(RET_HEADS, dtype=_F32))))
    idx = jnp.arange(nr, dtype=_F32)
    dist = jnp.abs(idx[:, None] - idx[None, :])
    visible = (jnp.arange(nr)[None, :] // CHUNK) <= (jnp.arange(nr)[:, None] // CHUNK)
    scale = RET_DK ** -0.5
    dm = jnp.exp(log_gamma[:, None, None] * dist[None]) * visible[None] * scale
    dmask = jnp.transpose(dm, (1, 0, 2)).reshape(nr, RET_HEADS * nr)
    qdec = jnp.repeat(jnp.exp(log_gamma[None, :] * (idx + 1.0)[:, None]), RET_DK, axis=1)
    kdec = jnp.repeat(jnp.exp(log_gamma[None, :] * (nr - 1.0 - idx)[:, None]), RET_DK, axis=1) * scale
    sdec = jnp.broadcast_to(jnp.repeat(jnp.exp(log_gamma * nr), RET_DK)[:, None],
                            (RET_QK_W, RET_V_W))
    bd = (jnp.arange(RET_QK_W)[:, None] // RET_DK == jnp.arange(RET_V_W)[None, :] // RET_DV)
    return cos_t, sin_t, dmask, qdec, kdec, sdec, bd.astype(_F32)


def _mixer(x, w_in, w_out, conv_w, conv_b, cg, cb, gn, lg, lb, tables, alpha):
    b, s, d = x.shape
    t = min(SEQ_TILE, s)
    ns = MIXER_STREAMS if b % MIXER_STREAMS == 0 else 1
    assert s % t == 0 and t % RET_BLOCK == 0 and RET_BLOCK % CHUNK == 0 and t % CONV_ROWS == 0
    cos_t, sin_t, dmask, qdec, kdec, sdec, bd = tables
    row = lambda a: a.reshape(1, -1)
    conv_w = jnp.pad(conv_w, ((0, HALO - CONV_K), (0, 0)))
    consts = [w_in, w_out, conv_w, row(conv_b), row(cg), row(cb), row(gn), row(lg), row(lb)]
    tail = [dmask, qdec, kdec, sdec, bd]
    x_spec = pl.BlockSpec((ns, None, t, d), lambda i, j: (0, i, j, 0))
    in_specs = ([x_spec]
                + [_const_spec(a.shape) for a in consts]
                + [pl.BlockSpec((t, RET_QK_W), lambda i, j: (j, 0))] * 2
                + [_const_spec(a.shape) for a in tail])
    out = pl.pallas_call(
        functools.partial(_mixer_kernel, alpha),
        grid=(b // ns, s // t),
        in_specs=in_specs,
        out_specs=x_spec,
        out_shape=jax.ShapeDtypeStruct((ns, b // ns, s, d), _F32),
        scratch_shapes=[pltpu.VMEM((ns, SUBLANES, t + HALO, CONV_CH), _F32),
                        pltpu.VMEM((ns, RET_QK_W, RET_V_W), _F32),
                        pltpu.VMEM((ns, t, RET_V_W), _F32)],
        compiler_params=pltpu.CompilerParams(
            dimension_semantics=("arbitrary", "arbitrary"), vmem_limit_bytes=VMEM_LIMIT),
        name="mixer",
    )(x.reshape(ns, b // ns, s, d), *consts, cos_t, sin_t, *tail)
    return out.reshape(b, s, d)


def _swiglu_tile(xb, w1_ref, w3_ref, w2_ref):
    d_ff = w1_ref.shape[1]
    acc = None
    for c in range(0, d_ff, FFN_CHUNK):
        a = _dot(xb, w1_ref[:, c:c + FFN_CHUNK])
        g = _dot(xb, w3_ref[:, c:c + FFN_CHUNK])
        h = (a * _sigmoid(a) * g).astype(_BF)
        part = _dot(h, w2_ref[c:c + FFN_CHUNK, :])
        acc = part if acc is None else acc + part
    return acc


def _dense_ffn_kernel(alpha, x_ref, w1_ref, w3_ref, w2_ref, lg_ref, lb_ref, o_ref):
    x = x_ref[...]
    f = _swiglu_tile(x.astype(_BF), w1_ref, w3_ref, w2_ref)
    o_ref[...] = _layer_norm(alpha * x + f, lg_ref[...], lb_ref[...])


def _dense_ffn(x, w1, w3, w2, lg, lb, alpha):
    n, d = x.shape
    tm = min(FFN_TILE, n)
    assert n % tm == 0 and w1.shape[1] % FFN_CHUNK == 0
    consts = [w1, w3, w2, lg.reshape(1, -1), lb.reshape(1, -1)]
    return pl.pallas_call(
        functools.partial(_dense_ffn_kernel, alpha),
        grid=(n // tm,),
        in_specs=[pl.BlockSpec((tm, d), lambda i: (i, 0))] + [_const_spec(a.shape) for a in consts],
        out_specs=pl.BlockSpec((tm, d), lambda i: (i, 0)),
        out_shape=jax.ShapeDtypeStruct((n, d), _F32),
        compiler_params=pltpu.CompilerParams(
            dimension_semantics=("arbitrary",), vmem_limit_bytes=VMEM_LIMIT),
        name="dense_ffn",
    )(x, *consts)


def _router_kernel(x_ref, wr_ref, tri_ref, ri_ref, rf_ref, cnt_ref, carry):
    @pl.when(pl.program_id(0) == 0)
    def _():
        carry[...] = jnp.zeros_like(carry)

    tr = x_ref.shape[0]
    logits = lax.dot_general(wr_ref[...], x_ref[...].astype(_BF), (((1,), (1,)), ((), ())),
                             preferred_element_type=_F32)
    eidx = lax.broadcasted_iota(jnp.int32, (N_EXPERTS, tr), 0)
    m0 = jnp.max(logits, axis=0, keepdims=True)
    e0 = jnp.min(jnp.where(logits == m0, eidx, N_EXPERTS), axis=0, keepdims=True)
    rest = jnp.where(eidx == e0, -jnp.inf, logits)
    m1 = jnp.max(rest, axis=0, keepdims=True)
    e1 = jnp.min(jnp.where(rest == m1, eidx, N_EXPERTS), axis=0, keepdims=True)
    tt = jnp.exp(m1 - m0)
    g0 = 1.0 / (1.0 + tt)
    g1 = tt / (1.0 + tt)

    oh0 = eidx == e0
    oh1 = eidx == e1
    member = jnp.where(oh0 | oh1, 1.0, 0.0)
    before = _dot(member.astype(_BF), tri_ref[...]) + carry[:, 0:1]
    rank0 = jnp.sum(jnp.where(oh0, before, 0.0), axis=0, keepdims=True)
    rank1 = jnp.sum(jnp.where(oh1, before, 0.0), axis=0, keepdims=True)
    carry[...] = carry[...] + jnp.sum(member, axis=1, keepdims=True)

    zi = jnp.zeros((N_EXPERTS - 4, tr), jnp.int32)
    ri_ref[...] = jnp.concatenate(
        [e0, e1, rank0.astype(jnp.int32), rank1.astype(jnp.int32), zi], axis=0)
    rf_ref[...] = jnp.concatenate([g0, g1, jnp.zeros((N_EXPERTS - 2, tr), _F32)], axis=0)
    cnt_ref[...] = carry[...]


def _router(x, wr_t):
    n, d = x.shape
    tr = min(ROUTE_TILE, n)
    assert n % tr == 0
    tri = (jnp.arange(tr)[:, None] < jnp.arange(tr)[None, :]).astype(_BF)
    return pl.pallas_call(
        _router_kernel,
        grid=(n // tr,),
        in_specs=[pl.BlockSpec((tr, d), lambda i: (i, 0)),
                  _const_spec(wr_t.shape), _const_spec(tri.shape)],
        out_specs=[pl.BlockSpec((N_EXPERTS, tr), lambda i: (0, i)),
                   pl.BlockSpec((N_EXPERTS, tr), lambda i: (0, i)),
                   pl.BlockSpec((N_EXPERTS, LANES), lambda i: (0, 0))],
        out_shape=[jax.ShapeDtypeStruct((N_EXPERTS, n), jnp.int32),
                   jax.ShapeDtypeStruct((N_EXPERTS, n), _F32),
                   jax.ShapeDtypeStruct((N_EXPERTS, LANES), _F32)],
        scratch_shapes=[pltpu.VMEM((N_EXPERTS, LANES), _F32)],
        compiler_params=pltpu.CompilerParams(
            dimension_semantics=("arbitrary",), vmem_limit_bytes=VMEM_LIMIT),
        name="router",
    )(x, wr_t, tri)


def _for_each_row(n_rows, fn):
    def body(c, carry):
        for u in range(SUBLANES):
            fn(c, u)
        return carry

    lax.fori_loop(0, n_rows // SUBLANES, body, 0)


def _wait_rows(hbm_ref, n_rows, sem):
    rows = hbm_ref.at[pl.ds(0, n_rows)]
    pltpu.make_async_copy(rows, rows, sem).wait()


def _dispatch_kernel(pad_ref, pos_ref, x_ref, xs_ref, zbuf, sem):
    tm = x_ref.shape[0] * SUBLANES

    @pl.when(pl.program_id(0) == 0)
    def _():
        zbuf[...] = jnp.zeros_like(zbuf)
        for e in range(pad_ref.shape[0]):
            fill = pltpu.make_async_copy(
                zbuf, xs_ref.at[pl.ds(pl.multiple_of(pad_ref[e], SUBLANES), zbuf.shape[0])],
                sem.at[0])
            fill.start()
            fill.wait()

    def start(c, u):
        i = c * SUBLANES + u
        src = x_ref.at[c, pl.ds(u, 1)]
        pltpu.make_async_copy(src, xs_ref.at[pl.ds(pos_ref[0, i], 1)], sem.at[0]).start(priority=0)
        pltpu.make_async_copy(src, xs_ref.at[pl.ds(pos_ref[0, tm + i], 1)], sem.at[1]).start(priority=1)

    _for_each_row(tm, start)
    _wait_rows(xs_ref, tm, sem.at[0])
    _wait_rows(xs_ref, tm, sem.at[1])


def _move_tile(n):
    tm = min(MOVE_TILE, n)
    assert n % tm == 0 and tm % SUBLANES == 0
    return tm


def _tile_positions(pos, tm):
    n = pos.shape[1]
    return pos.reshape(2, n // tm, tm).transpose(1, 0, 2).reshape(n // tm, 1, 2 * tm)


def _dispatch(x, pos, pad_start, n_slots, pad_rows):
    n, d = x.shape
    tm = _move_tile(n)
    return pl.pallas_call(
        _dispatch_kernel,
        grid_spec=pltpu.PrefetchScalarGridSpec(
            num_scalar_prefetch=1,
            grid=(n // tm,),
            in_specs=[pl.BlockSpec((None, 1, 2 * tm), lambda i, pad: (i, 0, 0),
                                   memory_space=pltpu.SMEM),
                      pl.BlockSpec((tm // SUBLANES, SUBLANES, d), lambda i, pad: (i, 0, 0))],
            out_specs=pl.BlockSpec(memory_space=pl.ANY),
            scratch_shapes=[pltpu.VMEM((pad_rows, d), _F32), pltpu.SemaphoreType.DMA((2,))],
        ),
        out_shape=jax.ShapeDtypeStruct((n_slots, d), _F32),
        compiler_params=pltpu.CompilerParams(
            dimension_semantics=("arbitrary",), vmem_limit_bytes=VMEM_LIMIT,
            has_side_effects=True),
        name="dispatch",
    )(pad_start, _tile_positions(pos, tm), x.reshape(n // SUBLANES, SUBLANES, d))


def _expert_kernel(te_ref, na_ref, xs_ref, w1_ref, w3_ref, w2_ref, ys_ref):
    del te_ref
    active = pl.program_id(0) < na_ref[0]

    @pl.when(active)
    def _():
        ys_ref[...] = _swiglu_tile(xs_ref[...].astype(_BF), w1_ref, w3_ref, w2_ref)

    @pl.when(jnp.logical_not(active))
    def _():
        ys_ref[...] = jnp.zeros_like(ys_ref)


def _expert_ffn(xs, w1, w3, w2, tile_expert, n_active):
    d = xs.shape[1]
    tm = FFN_TILE
    d_ff = w1.shape[2]
    n_tiles = tile_expert.shape[0]
    n_slots = n_tiles * tm
    w_in_spec = pl.BlockSpec((None, d, d_ff), lambda i, te, na: (te[i], 0, 0),
                             pipeline_mode=pl.Buffered(1))
    w_out_spec = pl.BlockSpec((None, d_ff, d), lambda i, te, na: (te[i], 0, 0),
                              pipeline_mode=pl.Buffered(1))
    return pl.pallas_call(
        _expert_kernel,
        grid_spec=pltpu.PrefetchScalarGridSpec(
            num_scalar_prefetch=2,
            grid=(n_tiles,),
            in_specs=[pl.BlockSpec((tm, d), lambda i, te, na: (jnp.minimum(i, na[0] - 1), 0)),
                      w_in_spec, w_in_spec, w_out_spec],
            out_specs=pl.BlockSpec((tm, d), lambda i, te, na: (i, 0)),
        ),
        out_shape=jax.ShapeDtypeStruct((n_slots, d), _F32),
        compiler_params=pltpu.CompilerParams(
            dimension_semantics=("arbitrary",), vmem_limit_bytes=VMEM_LIMIT),
        name="expert_ffn",
    )(tile_expert, n_active, xs, w1, w3, w2)


def _sc_gather(table, idx):
    m, d = idx.shape[0], table.shape[1]
    workers = SC_CORES * SC_SUBCORES
    per_worker = m // workers
    assert m % (workers * SC_ROWS) == 0
    mesh = plsc.VectorSubcoreMesh(core_axis_name="c", subcore_axis_name="s",
                                  num_cores=SC_CORES, num_subcores=SC_SUBCORES)

    def body(table_hbm, idx_hbm, out_hbm, idx_v, rows_v, sem):
        base = (lax.axis_index("s") * SC_CORES + lax.axis_index("c")) * per_worker

        @pl.loop(0, per_worker // SC_ROWS)
        def _(c):
            off = pl.multiple_of(base + c * SC_ROWS, SC_ROWS)
            pltpu.sync_copy(idx_hbm.at[pl.ds(off, SC_ROWS)], idx_v)
            pltpu.async_copy(table_hbm.at[idx_v], rows_v, sem).wait()
            pltpu.sync_copy(rows_v, out_hbm.at[pl.ds(off, SC_ROWS)])

    return pl.kernel(
        body, out_type=jax.ShapeDtypeStruct((m, d), table.dtype), mesh=mesh,
        scratch_types=[pltpu.VMEM((SC_ROWS,), jnp.int32), pltpu.VMEM((SC_ROWS, d), table.dtype),
                       pltpu.SemaphoreType.DMA],
        name="sc_gather")(table, idx)


def _combine_dense_kernel(alpha, x_ref, y0_ref, y1_ref, gate_ref, lg_ref, lb_ref, o_ref):
    gate = gate_ref[...]
    f = y0_ref[...] * gate[:, 0:1] + y1_ref[...] * gate[:, 1:2]
    o_ref[...] = _layer_norm(alpha * x_ref[...] + f, lg_ref[...], lb_ref[...])


def _combine_dense(x, y01, gates, lg, lb, alpha):
    n, d = x.shape
    tm = _move_tile(n)
    return pl.pallas_call(
        functools.partial(_combine_dense_kernel, alpha),
        grid=(n // tm,),
        in_specs=[pl.BlockSpec((tm, d), lambda i: (i, 0)),
                  pl.BlockSpec((tm, d), lambda i: (i, 0)),
                  pl.BlockSpec((tm, d), lambda i: (i + n // tm, 0)),
                  pl.BlockSpec((tm, 2), lambda i: (i, 0)),
                  _const_spec((1, d)), _const_spec((1, d))],
        out_specs=pl.BlockSpec((tm, d), lambda i: (i, 0)),
        out_shape=jax.ShapeDtypeStruct((n, d), _F32),
        compiler_params=pltpu.CompilerParams(
            dimension_semantics=("arbitrary",), vmem_limit_bytes=VMEM_LIMIT),
        name="combine_dense",
    )(x, y01, y01, gates, lg.reshape(1, -1), lb.reshape(1, -1))


def _combine_kernel(alpha, pos_ref, x_ref, gate_ref, lg_ref, lb_ref, ys_ref, o_ref, buf0, buf1, sem):
    tm, d = x_ref.shape

    def start(c, u):
        i = c * SUBLANES + u
        pltpu.make_async_copy(ys_ref.at[pl.ds(pos_ref[0, i], 1)], buf0.at[c, pl.ds(u, 1)],
                              sem.at[0]).start(priority=0)
        pltpu.make_async_copy(ys_ref.at[pl.ds(pos_ref[0, tm + i], 1)], buf1.at[c, pl.ds(u, 1)],
                              sem.at[1]).start(priority=1)

    _for_each_row(tm, start)
    _wait_rows(ys_ref, tm, sem.at[0])
    _wait_rows(ys_ref, tm, sem.at[1])
    gate = gate_ref[...]
    f = buf0[...].reshape(tm, d) * gate[:, 0:1] + buf1[...].reshape(tm, d) * gate[:, 1:2]
    o_ref[...] = _layer_norm(alpha * x_ref[...] + f, lg_ref[...], lb_ref[...])


def _combine(x, ys, pos, gates, lg, lb, alpha):
    n, d = x.shape
    tm = _move_tile(n)
    buf = pltpu.VMEM((tm // SUBLANES, SUBLANES, d), _F32)
    return pl.pallas_call(
        functools.partial(_combine_kernel, alpha),
        grid=(n // tm,),
        in_specs=[pl.BlockSpec((None, 1, 2 * tm), lambda i: (i, 0, 0), memory_space=pltpu.SMEM),
                  pl.BlockSpec((tm, d), lambda i: (i, 0)),
                  pl.BlockSpec((tm, 2), lambda i: (i, 0)),
                  _const_spec((1, d)), _const_spec((1, d)),
                  pl.BlockSpec(memory_space=pl.ANY)],
        out_specs=pl.BlockSpec((tm, d), lambda i: (i, 0)),
        out_shape=jax.ShapeDtypeStruct((n, d), _F32),
        scratch_shapes=[buf, buf, pltpu.SemaphoreType.DMA((2,))],
        compiler_params=pltpu.CompilerParams(
            dimension_semantics=("arbitrary",), vmem_limit_bytes=VMEM_LIMIT),
        name="combine",
    )(_tile_positions(pos, tm), x, gates, lg.reshape(1, -1), lb.reshape(1, -1), ys)


def _moe_ffn(x, router, w1, w3, w2, lg, lb, alpha):
    n, d = x.shape
    tm = FFN_TILE
    n_tiles = -(-(2 * n) // tm) + N_EXPERTS
    n_slots = n_tiles * tm

    ri, rf, cnt = _router(x, router.T.astype(_BF))
    counts = cnt[:, 0].astype(jnp.int32)
    padded = (counts + tm - 1) // tm * tm
    group_end = jnp.cumsum(padded)
    group_start = group_end - padded
    experts = jnp.arange(N_EXPERTS, dtype=jnp.int32)[:, None]

    def slot(e, rank):
        return jnp.sum(jnp.where(e[None, :] == experts, group_start[:, None], 0), axis=0) + rank

    pos = jnp.stack([slot(ri[0], ri[2]), slot(ri[1], ri[3])])
    tile_start = jnp.arange(n_tiles, dtype=jnp.int32) * tm
    tile_expert = jnp.minimum(
        jnp.sum(tile_start[:, None] >= group_end[None, :], axis=1), N_EXPERTS - 1).astype(jnp.int32)
    n_active = (group_end[-1:] // tm).astype(jnp.int32)

    zrows = tm + SUBLANES
    pad_start = (group_start + counts) // SUBLANES * SUBLANES
    tail = jnp.minimum(pad_start[-1] + tm * jnp.arange(1, N_EXPERTS + 2, dtype=jnp.int32),
                       n_slots + tm - zrows)
    xs = _dispatch(x, pos, jnp.concatenate([pad_start, tail]), n_slots + tm, zrows)
    ys = _expert_ffn(xs, w1, w3, w2, tile_expert, n_active)
    y01 = _sc_gather(ys, pos.reshape(-1))
    return _combine_dense(x, y01, rf[:2].T, lg, lb, alpha)


def kernel(x, w_in, w_out, conv_w, conv_b, conv_ln_g, conv_ln_b, ret_gn_g, ln1_g, ln1_b, ln2_g,
           ln2_b, dense_w1, dense_w3, dense_w2, moe_router, moe_w1, moe_w3, moe_w2):
    b, s, d = x.shape
    depth = w_in.shape[0]
    alpha = (2.0 * depth) ** 0.25
    tables = _retention_tables(s)
    for l in range(depth):
        x = _mixer(x, w_in[l].astype(_BF), w_out[l].astype(_BF), conv_w[l], conv_b[l],
                   conv_ln_g[l], conv_ln_b[l], ret_gn_g[l], ln1_g[l], ln1_b[l], tables, alpha)
        h = x.reshape(b * s, d)
        i = l // 2
        if l % 2 == 0:
            h = _dense_ffn(h, dense_w1[i].astype(_BF), dense_w3[i].astype(_BF),
                           dense_w2[i].astype(_BF), ln2_g[l], ln2_b[l], alpha)
        else:
            h = _moe_ffn(h, moe_router[i], moe_w1[i].astype(_BF), moe_w3[i].astype(_BF),
                         moe_w2[i].astype(_BF), ln2_g[l], ln2_b[l], alpha)
        x = h.reshape(b, s, d)
    return x
```

```python
import functools

import jax
import jax.numpy as jnp
from jax import lax
from jax.experimental import pallas as pl
from jax.experimental.pallas import tpu as pltpu
from jax.experimental.pallas import tpu_sc as plsc

CHUNK = 64
CONV_CH = 512
CONV_K = 31
RET_HEADS = 4
RET_DK = 64
RET_DV = 128
RET_QK_W = RET_HEADS * RET_DK
RET_V_W = RET_HEADS * RET_DV
ROPE_BASE = 10000.0
N_EXPERTS = 8
LN_EPS = 1e-5

LANES = 128
SUBLANES = 8
SEQ_TILE = 512
MIXER_STREAMS = 1
MIXER_PHASES = 4
RET_BLOCK = 128
CONV_ROWS = 32
HALO = 32
FFN_TILE = 512
FFN_CHUNK = 512
ROUTE_TILE = 1024
MOVE_TILE = 512
SC_CORES = 2
SC_SUBCORES = 16
SC_ROWS = 32
VMEM_LIMIT = 56 * 1024 * 1024

_BF = jnp.bfloat16
_F32 = jnp.float32


def _dot(a, b):
    return jnp.dot(a, b, preferred_element_type=_F32)


def _layer_norm(v, g, b):
    mu = jnp.mean(v, axis=-1, keepdims=True)
    d = v - mu
    var = jnp.mean(d * d, axis=-1, keepdims=True)
    return d * lax.rsqrt(var + LN_EPS) * g + b


def _sigmoid(v):
    return 1.0 / (1.0 + jnp.exp(-v))


def _const_spec(shape):
    nd = len(shape)
    return pl.BlockSpec(shape, lambda *_: (0,) * nd, pipeline_mode=pl.Buffered(1))


def _mixer_kernel(alpha, x_ref, *refs):
    *shared, o_ref, ush, state, rbuf = refs

    @pl.when(pl.program_id(1) == 0)
    def _():
        ush[:, 0, 0:HALO, :] = jnp.zeros((ush.shape[0], HALO, CONV_CH), _F32)
        state[...] = jnp.zeros_like(state)

    ns = x_ref.shape[0]
    streams = [_mixer_stream(alpha, x_ref.at[p], *shared, o_ref.at[p], ush.at[p], state.at[p],
                             rbuf.at[p]) for p in range(ns)]
    for step in range(MIXER_PHASES + ns - 1):
        for p in range(ns):
            if 0 <= step - p < MIXER_PHASES:
                next(streams[p], None)


def _mixer_stream(alpha, x_ref, w_in_ref, w_out_ref, conv_w_ref, conv_b_ref, cg_ref, cb_ref,
                  gn_ref, lg_ref, lb_ref, cos_ref, sin_ref, dmask_ref, qdec_ref, kdec_ref,
                  sdec_ref, bd_ref, o_ref, ush, state, rbuf):
    t = x_ref.shape[0]
    ubuf = ush.at[0]
    x = x_ref[...]
    xb = x.astype(_BF)

    c0 = 2 * CONV_CH
    ab = _dot(xb, w_in_ref[:, 0:c0])
    qk = _dot(xb, w_in_ref[:, c0:c0 + 2 * RET_QK_W])
    v = _dot(xb, w_in_ref[:, c0 + 2 * RET_QK_W:c0 + 2 * RET_QK_W + RET_V_W])
    g = _dot(xb, w_in_ref[:, c0 + 2 * RET_QK_W + RET_V_W:])
    yield

    ubuf[HALO:HALO + t, :] = ab[:, :CONV_CH] * _sigmoid(ab[:, CONV_CH:])
    span = t + HALO - SUBLANES
    for r in range(1, SUBLANES):
        ush[r, 0:span, :] = ubuf[r:r + span, :]
    off = HALO - (CONV_K - 1)
    conv_b = conv_b_ref[...]
    blocks = []
    for r0 in range(0, t, CONV_ROWS):
        acc = jnp.broadcast_to(conv_b, (CONV_ROWS, CONV_CH))
        for r in range(SUBLANES):
            taps = [j for j in range(CONV_K) if (j + off) % SUBLANES == r]
            reach = max((j + off) // SUBLANES for j in taps) * SUBLANES
            seg = ush[r, r0:r0 + reach + CONV_ROWS, :]
            for j in taps:
                a = (j + off) // SUBLANES * SUBLANES
                acc = acc + conv_w_ref[j:j + 1, :] * seg[a:a + CONV_ROWS]
        blocks.append(acc)
    conv = jnp.concatenate(blocks, axis=0)
    ubuf[0:HALO, :] = ubuf[t:t + HALO, :]
    un = _layer_norm(conv, cg_ref[...], cb_ref[...])
    u_out = un * _sigmoid(un)

    yield

    cos_t = cos_ref[...]
    sin_t = sin_ref[...]
    lane = lax.broadcasted_iota(jnp.int32, (1, LANES), 1)
    first_half = (lane % RET_DK) < (RET_DK // 2)

    def rope(z):
        parts = []
        for c in range(0, RET_QK_W, LANES):
            zc = z[:, c:c + LANES]
            up = pltpu.roll(zc, LANES - RET_DK // 2, axis=1)
            dn = pltpu.roll(zc, RET_DK // 2, axis=1)
            parts.append(jnp.where(first_half, up, dn))
        return z * cos_t + jnp.concatenate(parts, axis=1) * sin_t

    q = rope(qk[:, :RET_QK_W])
    k = rope(qk[:, RET_QK_W:])

    lane_qk = lax.broadcasted_iota(jnp.int32, (1, RET_QK_W), 1)
    dmask = dmask_ref[...]
    qdec = qdec_ref[...]
    kdec = kdec_ref[...]
    nr = RET_BLOCK
    for s in range(t // nr):
        rows = slice(s * nr, (s + 1) * nr)
        q_s, k_s, v_s = q[rows], k[rows], v[rows]
        v_b = v_s.astype(_BF)
        kbd = jnp.concatenate(
            [jnp.where(lane_qk // RET_DK == h, k_s, 0.0).astype(_BF) for h in range(RET_HEADS)],
            axis=0)
        sc = lax.dot_general(q_s.astype(_BF), kbd, (((1,), (1,)), ((), ())),
                             preferred_element_type=_F32)
        p = (sc * dmask).astype(_BF)
        intra = jnp.concatenate(
            [_dot(p[:, h * nr:(h + 1) * nr], v_b[:, h * RET_DV:(h + 1) * RET_DV])
             for h in range(RET_HEADS)], axis=1)
        st = state[...]
        cross = _dot((q_s * qdec).astype(_BF), st.astype(_BF))
        kv = lax.dot_general((k_s * kdec).astype(_BF), v_b, (((0,), (0,)), ((), ())),
                             preferred_element_type=_F32)
        state[...] = st * sdec_ref[...] + kv * bd_ref[...]
        rbuf[rows, :] = intra + cross

    yield

    r = rbuf[...]
    gate = g * _sigmoid(g)
    gn = gn_ref[...]
    r_parts = []
    for h in range(RET_HEADS):
        hs = slice(h * RET_DV, (h + 1) * RET_DV)
        rh = r[:, hs]
        mu = jnp.mean(rh, axis=-1, keepdims=True)
        d = rh - mu
        var = jnp.mean(d * d, axis=-1, keepdims=True)
        r_parts.append(gate[:, hs] * (d * lax.rsqrt(var + LN_EPS) * gn[:, hs]))

    mixed = jnp.concatenate([u_out] + r_parts, axis=1).astype(_BF)
    mix = _dot(mixed, w_out_ref[...])
    o_ref[...] = _layer_norm(alpha * x + mix, lg_ref[...], lb_ref[...])


def _retention_tables(seq):
    half = RET_DK // 2
    inv_freq = ROPE_BASE ** (-jnp.arange(half, dtype=_F32) / half)
    ang = jnp.arange(seq, dtype=jnp.int32).astype(_F32)[:, None] * inv_freq[None, :]
    cos, sin = jnp.cos(ang), jnp.sin(ang)
    cos_t = jnp.tile(jnp.concatenate([cos, cos], axis=1), (1, RET_HEADS))
    sin_t = jnp.tile(jnp.concatenate([-sin, sin], axis=1), (1, RET_HEADS))

    nr = RET_BLOCK
    log_gamma = jnp.log1p(-(2.0 ** (-5.0 - jnp.arange(RET_HEADS, dtype=_F32))))
    idx = jnp.arange(nr, dtype=_F32)
    dist = jnp.abs(idx[:, None] - idx[None, :])
    visible = (jnp.arange(nr)[None, :] // CHUNK) <= (jnp.arange(nr)[:, None] // CHUNK)
    scale = RET_DK ** -0.5
    dm = jnp.exp(log_gamma[:, None, None] * dist[None]) * visible[None] * scale
    dmask = jnp.transpose(dm, (1, 0, 2)).reshape(nr, RET_HEADS * nr)
    qdec = jnp.repeat(jnp.exp(log_gamma[None, :] * (idx + 1.0)[:, None]), RET_DK, axis=1)
    kdec = jnp.repeat(jnp.exp(log_gamma[None, :] * (nr - 1.0 - idx)[:, None]), RET_DK, axis=1) * scale
    sdec = jnp.broadcast_to(jnp.repeat(jnp.exp(log_gamma * nr), RET_DK)[:, None],
                            (RET_QK_W, RET_V_W))
    bd = (jnp.arange(RET_QK_W)[:, None] // RET_DK == jnp.arange(RET_V_W)[None, :] // RET_DV)
    return cos_t, sin_t, dmask, qdec, kdec, sdec, bd.astype(_F32)


def _mixer(x, w_in, w_out, conv_w, conv_b, cg, cb, gn, lg, lb, tables, alpha):
    b, s, d = x.shape
    t = min(SEQ_TILE, s)
    ns = MIXER_STREAMS if b % MIXER_STREAMS == 0 else 1
    assert s % t == 0 and t % RET_BLOCK == 0 and RET_BLOCK % CHUNK == 0 and t % CONV_ROWS == 0
    cos_t, sin_t, dmask, qdec, kdec, sdec, bd = tables
    row = lambda a: a.reshape(1, -1)
    conv_w = jnp.pad(conv_w, ((0, HALO - CONV_K), (0, 0)))
    consts = [w_in, w_out, conv_w, row(conv_b), row(cg), row(cb), row(gn), row(lg), row(lb)]
    tail = [dmask, qdec, kdec, sdec, bd]
    x_spec = pl.BlockSpec((ns, None, t, d), lambda i, j: (0, i, j, 0))
    in_specs = ([x_spec]
                + [_const_spec(a.shape) for a in consts]
                + [pl.BlockSpec((t, RET_QK_W), lambda i, j: (j, 0))] * 2
                + [_const_spec(a.shape) for a in tail])
    out = pl.pallas_call(
        functools.partial(_mixer_kernel, alpha),
        grid=(b // ns, s // t),
        in_specs=in_specs,
        out_specs=x_spec,
        out_shape=jax.ShapeDtypeStruct((ns, b // ns, s, d), _F32),
        scratch_shapes=[pltpu.VMEM((ns, SUBLANES, t + HALO, CONV_CH), _F32),
                        pltpu.VMEM((ns, RET_QK_W, RET_V_W), _F32),
                        pltpu.VMEM((ns, t, RET_V_W), _F32)],
        compiler_params=pltpu.CompilerParams(
            dimension_semantics=("arbitrary", "arbitrary"), vmem_limit_bytes=VMEM_LIMIT),
        name="mixer",
    )(x.reshape(ns, b // ns, s, d), *consts, cos_t, sin_t, *tail)
    return out.reshape(b, s, d)


def _swiglu_tile(xb, w1_ref, w3_ref, w2_ref):
    d_ff = w1_ref.shape[1]
    acc = None
    for c in range(0, d_ff, FFN_CHUNK):
        a = _dot(xb, w1_ref[:, c:c + FFN_CHUNK])
        g = _dot(xb, w3_ref[:, c:c + FFN_CHUNK])
        h = (a * _sigmoid(a) * g).astype(_BF)
        part = _dot(h, w2_ref[c:c + FFN_CHUNK, :])
        acc = part if acc is None else acc + part
    return acc


def _dense_ffn_kernel(alpha, x_ref, w1_ref, w3_ref, w2_ref, lg_ref, lb_ref, o_ref):
    x = x_ref[...]
    f = _swiglu_tile(x.astype(_BF), w1_ref, w3_ref, w2_ref)
    o_ref[...] = _layer_norm(alpha * x + f, lg_ref[...], lb_ref[...])


def _dense_ffn(x, w1, w3, w2, lg, lb, alpha):
    n, d = x.shape
    tm = min(FFN_TILE, n)
    assert n % tm == 0 and w1.shape[1] % FFN_CHUNK == 0
    consts = [w1, w3, w2, lg.reshape(1, -1), lb.reshape(1, -1)]
    return pl.pallas_call(
        functools.partial(_dense_ffn_kernel, alpha),
        grid=(n // tm,),
        in_specs=[pl.BlockSpec((tm, d), lambda i: (i, 0))] + [_const_spec(a.shape) for a in consts],
        out_specs=pl.BlockSpec((tm, d), lambda i: (i, 0)),
        out_shape=jax.ShapeDtypeStruct((n, d), _F32),
        compiler_params=pltpu.CompilerParams(
            dimension_semantics=("arbitrary",), vmem_limit_bytes=VMEM_LIMIT),
        name="dense_ffn",
    )(x, *consts)


def _router_kernel(x_ref, wr_ref, tri_ref, ri_ref, rf_ref, cnt_ref, carry):
    @pl.when(pl.program_id(0) == 0)
    def _():
        carry[...] = jnp.zeros_like(carry)

    tr = x_ref.shape[0]
    logits = lax.dot_general(wr_ref[...], x_ref[...].astype(_BF), (((1,), (1,)), ((), ())),
                             preferred_element_type=_F32)
    eidx = lax.broadcasted_iota(jnp.int32, (N_EXPERTS, tr), 0)
    m0 = jnp.max(logits, axis=0, keepdims=True)
    e0 = jnp.min(jnp.where(logits == m0, eidx, N_EXPERTS), axis=0, keepdims=True)
    rest = jnp.where(eidx == e0, -jnp.inf, logits)
    m1 = jnp.max(rest, axis=0, keepdims=True)
    e1 = jnp.min(jnp.where(rest == m1, eidx, N_EXPERTS), axis=0, keepdims=True)
    tt = jnp.exp(m1 - m0)
    g0 = 1.0 / (1.0 + tt)
    g1 = tt / (1.0 + tt)

    oh0 = eidx == e0
    oh1 = eidx == e1
    member = jnp.where(oh0 | oh1, 1.0, 0.0)
    before = _dot(member.astype(_BF), tri_ref[...]) + carry[:, 0:1]
    rank0 = jnp.sum(jnp.where(oh0, before, 0.0), axis=0, keepdims=True)
    rank1 = jnp.sum(jnp.where(oh1, before, 0.0), axis=0, keepdims=True)
    carry[...] = carry[...] + jnp.sum(member, axis=1, keepdims=True)

    zi = jnp.zeros((N_EXPERTS - 4, tr), jnp.int32)
    ri_ref[...] = jnp.concatenate(
        [e0, e1, rank0.astype(jnp.int32), rank1.astype(jnp.int32), zi], axis=0)
    rf_ref[...] = jnp.concatenate([g0, g1, jnp.zeros((N_EXPERTS - 2, tr), _F32)], axis=0)
    cnt_ref[...] = carry[...]


def _router(x, wr_t):
    n, d = x.shape
    tr = min(ROUTE_TILE, n)
    assert n % tr == 0
    tri = (jnp.arange(tr)[:, None] < jnp.arange(tr)[None, :]).astype(_BF)
    return pl.pallas_call(
        _router_kernel,
        grid=(n // tr,),
        in_specs=[pl.BlockSpec((tr, d), lambda i: (i, 0)),
                  _const_spec(wr_t.shape), _const_spec(tri.shape)],
        out_specs=[pl.BlockSpec((N_EXPERTS, tr), lambda i: (0, i)),
                   pl.BlockSpec((N_EXPERTS, tr), lambda i: (0, i)),
                   pl.BlockSpec((N_EXPERTS, LANES), lambda i: (0, 0))],
        out_shape=[jax.ShapeDtypeStruct((N_EXPERTS, n), jnp.int32),
                   jax.ShapeDtypeStruct((N_EXPERTS, n), _F32),
                   jax.ShapeDtypeStruct((N_EXPERTS, LANES), _F32)],
        scratch_shapes=[pltpu.VMEM((N_EXPERTS, LANES), _F32)],
        compiler_params=pltpu.CompilerParams(
            dimension_semantics=("arbitrary",), vmem_limit_bytes=VMEM_LIMIT),
        name="router",
    )(x, wr_t, tri)


def _for_each_row(n_rows, fn):
    def body(c, carry):
        for u in range(SUBLANES):
            fn(c, u)
        return carry

    lax.fori_loop(0, n_rows // SUBLANES, body, 0)


def _wait_rows(hbm_ref, n_rows, sem):
    rows = hbm_ref.at[pl.ds(0, n_rows)]
    pltpu.make_async_copy(rows, rows, sem).wait()


def _dispatch_kernel(pad_ref, pos_ref, x_ref, xs_ref, zbuf, sem):
    tm = x_ref.shape[0] * SUBLANES

    @pl.when(pl.program_id(0) == 0)
    def _():
        zbuf[...] = jnp.zeros_like(zbuf)
        for e in range(pad_ref.shape[0]):
            fill = pltpu.make_async_copy(
                zbuf, xs_ref.at[pl.ds(pl.multiple_of(pad_ref[e], SUBLANES), zbuf.shape[0])],
                sem.at[0])
            fill.start()
            fill.wait()

    def start(c, u):
        i = c * SUBLANES + u
        src = x_ref.at[c, pl.ds(u, 1)]
        pltpu.make_async_copy(src, xs_ref.at[pl.ds(pos_ref[0, i], 1)], sem.at[0]).start(priority=0)
        pltpu.make_async_copy(src, xs_ref.at[pl.ds(pos_ref[0, tm + i], 1)], sem.at[1]).start(priority=1)

    _for_each_row(tm, start)
    _wait_rows(xs_ref, tm, sem.at[0])
    _wait_rows(xs_ref, tm, sem.at[1])


def _move_tile(n):
    tm = min(MOVE_TILE, n)
    assert n % tm == 0 and tm % SUBLANES == 0
    return tm


def _tile_positions(pos, tm):
    n = pos.shape[1]
    return pos.reshape(2, n // tm, tm).transpose(1, 0, 2).reshape(n // tm, 1, 2 * tm)


def _dispatch(x, pos, pad_start, n_slots, pad_rows):
    n, d = x.shape
    tm = _move_tile(n)
    return pl.pallas_call(
        _dispatch_kernel,
        grid_spec=pltpu.PrefetchScalarGridSpec(
            num_scalar_prefetch=1,
            grid=(n // tm,),
            in_specs=[pl.BlockSpec((None, 1, 2 * tm), lambda i, pad: (i, 0, 0),
                                   memory_space=pltpu.SMEM),
                      pl.BlockSpec((tm // SUBLANES, SUBLANES, d), lambda i, pad: (i, 0, 0))],
            out_specs=pl.BlockSpec(memory_space=pl.ANY),
            scratch_shapes=[pltpu.VMEM((pad_rows, d), _F32), pltpu.SemaphoreType.DMA((2,))],
        ),
        out_shape=jax.ShapeDtypeStruct((n_slots, d), _F32),
        compiler_params=pltpu.CompilerParams(
            dimension_semantics=("arbitrary",), vmem_limit_bytes=VMEM_LIMIT,
            has_side_effects=True),
        name="dispatch",
    )(pad_start, _tile_positions(pos, tm), x.reshape(n // SUBLANES, SUBLANES, d))


def _expert_kernel(te_ref, na_ref, xs_ref, w1_ref, w3_ref, w2_ref, ys_ref):
    del te_ref
    active = pl.program_id(0) < na_ref[0]

    @pl.when(active)
    def _():
        ys_ref[...] = _swiglu_tile(xs_ref[...].astype(_BF), w1_ref, w3_ref, w2_ref)

    @pl.when(jnp.logical_not(active))
    def _():
        ys_ref[...] = jnp.zeros_like(ys_ref)


def _expert_ffn(xs, w1, w3, w2, tile_expert, n_active):
    d = xs.shape[1]
    tm = FFN_TILE
    d_ff = w1.shape[2]
    n_tiles = tile_expert.shape[0]
    n_slots = n_tiles * tm
    w_in_spec = pl.BlockSpec((None, d, d_ff), lambda i, te, na: (te[i], 0, 0),
                             pipeline_mode=pl.Buffered(1))
    w_out_spec = pl.BlockSpec((None, d_ff, d), lambda i, te, na: (te[i], 0, 0),
                              pipeline_mode=pl.Buffered(1))
    return pl.pallas_call(
        _expert_kernel,
        grid_spec=pltpu.PrefetchScalarGridSpec(
            num_scalar_prefetch=2,
            grid=(n_tiles,),
            in_specs=[pl.BlockSpec((tm, d), lambda i, te, na: (jnp.minimum(i, na[0] - 1), 0)),
                      w_in_spec, w_in_spec, w_out_spec],
            out_specs=pl.BlockSpec((tm, d), lambda i, te, na: (i, 0)),
        ),
        out_shape=jax.ShapeDtypeStruct((n_slots, d), _F32),
        compiler_params=pltpu.CompilerParams(
            dimension_semantics=("arbitrary",), vmem_limit_bytes=VMEM_LIMIT),
        name="expert_ffn",
    )(tile_expert, n_active, xs, w1, w3, w2)


def _sc_gather(table, idx):
    m, d = idx.shape[0], table.shape[1]
    workers = SC_CORES * SC_SUBCORES
    per_worker = m // workers
    assert m % (workers * SC_ROWS) == 0
    mesh = plsc.VectorSubcoreMesh(core_axis_name="c", subcore_axis_name="s",
                                  num_cores=SC_CORES, num_subcores=SC_SUBCORES)

    def body(table_hbm, idx_hbm, out_hbm, idx_v, rows_v, sem):
        base = (lax.axis_index("s") * SC_CORES + lax.axis_index("c")) * per_worker

        @pl.loop(0, per_worker // SC_ROWS)
        def _(c):
            off = pl.multiple_of(base + c * SC_ROWS, SC_ROWS)
            pltpu.sync_copy(idx_hbm.at[pl.ds(off, SC_ROWS)], idx_v)
            pltpu.async_copy(table_hbm.at[idx_v], rows_v, sem).wait()
            pltpu.sync_copy(rows_v, out_hbm.at[pl.ds(off, SC_ROWS)])

    return pl.kernel(
        body, out_type=jax.ShapeDtypeStruct((m, d), table.dtype), mesh=mesh,
        scratch_types=[pltpu.VMEM((SC_ROWS,), jnp.int32), pltpu.VMEM((SC_ROWS, d), table.dtype),
                       pltpu.SemaphoreType.DMA],
        name="sc_gather")(table, idx)


def _combine_dense_kernel(alpha, x_ref, y0_ref, y1_ref, gate_ref, lg_ref, lb_ref, o_ref):
    gate = gate_ref[...]
    f = y0_ref[...] * gate[:, 0:1] + y1_ref[...] * gate[:, 1:2]
    o_ref[...] = _layer_norm(alpha * x_ref[...] + f, lg_ref[...], lb_ref[...])


def _combine_dense(x, y01, gates, lg, lb, alpha):
    n, d = x.shape
    tm = _move_tile(n)
    return pl.pallas_call(
        functools.partial(_combine_dense_kernel, alpha),
        grid=(n // tm,),
        in_specs=[pl.BlockSpec((tm, d), lambda i: (i, 0)),
                  pl.BlockSpec((tm, d), lambda i: (i, 0)),
                  pl.BlockSpec((tm, d), lambda i: (i + n // tm, 0)),
                  pl.BlockSpec((tm, 2), lambda i: (i, 0)),
                  _const_spec((1, d)), _const_spec((1, d))],
        out_specs=pl.BlockSpec((tm, d), lambda i: (i, 0)),
        out_shape=jax.ShapeDtypeStruct((n, d), _F32),
        compiler_params=pltpu.CompilerParams(
            dimension_semantics=("arbitrary",), vmem_limit_bytes=VMEM_LIMIT),
        name="combine_dense",
    )(x, y01, y01, gates, lg.reshape(1, -1), lb.reshape(1, -1))


def _combine_kernel(alpha, pos_ref, x_ref, gate_ref, lg_ref, lb_ref, ys_ref, o_ref, buf0, buf1, sem):
    tm, d = x_ref.shape

    def start(c, u):
        i = c * SUBLANES + u
        pltpu.make_async_copy(ys_ref.at[pl.ds(pos_ref[0, i], 1)], buf0.at[c, pl.ds(u, 1)],
                              sem.at[0]).start(priority=0)
        pltpu.make_async_copy(ys_ref.at[pl.ds(pos_ref[0, tm + i], 1)], buf1.at[c, pl.ds(u, 1)],
                              sem.at[1]).start(priority=1)

    _for_each_row(tm, start)
    _wait_rows(ys_ref, tm, sem.at[0])
    _wait_rows(ys_ref, tm, sem.at[1])
    gate = gate_ref[...]
    f = buf0[...].reshape(tm, d) * gate[:, 0:1] + buf1[...].reshape(tm, d) * gate[:, 1:2]
    o_ref[...] = _layer_norm(alpha * x_ref[...] + f, lg_ref[...], lb_ref[...])


def _combine(x, ys, pos, gates, lg, lb, alpha):
    n, d = x.shape
    tm = _move_tile(n)
    buf = pltpu.VMEM((tm // SUBLANES, SUBLANES, d), _F32)
    return pl.pallas_call(
        functools.partial(_combine_kernel, alpha),
        grid=(n // tm,),
        in_specs=[pl.BlockSpec((None, 1, 2 * tm), lambda i: (i, 0, 0), memory_space=pltpu.SMEM),
                  pl.BlockSpec((tm, d), lambda i: (i, 0)),
                  pl.BlockSpec((tm, 2), lambda i: (i, 0)),
                  _const_spec((1, d)), _const_spec((1, d)),
                  pl.BlockSpec(memory_space=pl.ANY)],
        out_specs=pl.BlockSpec((tm, d), lambda i: (i, 0)),
        out_shape=jax.ShapeDtypeStruct((n, d), _F32),
        scratch_shapes=[buf, buf, pltpu.SemaphoreType.DMA((2,))],
        compiler_params=pltpu.CompilerParams(
            dimension_semantics=("arbitrary",), vmem_limit_bytes=VMEM_LIMIT),
        name="combine",
    )(_tile_positions(pos, tm), x, gates, lg.reshape(1, -1), lb.reshape(1, -1), ys)


def _moe_ffn(x, router, w1, w3, w2, lg, lb, alpha):
    n, d = x.shape
    tm = FFN_TILE
    n_tiles = -(-(2 * n) // tm) + N_EXPERTS
    n_slots = n_tiles * tm

    ri, rf, cnt = _router(x, router.T.astype(_BF))
    counts = cnt[:, 0].astype(jnp.int32)
    padded = (counts + tm - 1) // tm * tm
    group_end = jnp.cumsum(padded)
    group_start = group_end - padded
    experts = jnp.arange(N_EXPERTS, dtype=jnp.int32)[:, None]

    def slot(e, rank):
        return jnp.sum(jnp.where(e[None, :] == experts, group_start[:, None], 0), axis=0) + rank

    pos = jnp.stack([slot(ri[0], ri[2]), slot(ri[1], ri[3])])
    tile_start = jnp.arange(n_tiles, dtype=jnp.int32) * tm
    tile_expert = jnp.minimum(
        jnp.sum(tile_start[:, None] >= group_end[None, :], axis=1), N_EXPERTS - 1).astype(jnp.int32)
    n_active = (group_end[-1:] // tm).astype(jnp.int32)

    tok = jnp.arange(n, dtype=jnp.int32)
    slot_tok = (jnp.arange(n_slots, dtype=jnp.int32) % n).at[pos.reshape(-1)].set(
        jnp.concatenate([tok, tok]), unique_indices=True)
    xs = _sc_gather(x, slot_tok)
    ys = _expert_ffn(xs, w1, w3, w2, tile_expert, n_active)
    y01 = _sc_gather(ys, pos.reshape(-1))
    return _combine_dense(x, y01, rf[:2].T, lg, lb, alpha)


def kernel(x, w_in, w_out, conv_w, conv_b, conv_ln_g, conv_ln_b, ret_gn_g, ln1_g, ln1_b, ln2_g,
           ln2_b, dense_w1, dense_w3, dense_w2, moe_router, moe_w1, moe_w3, moe_w2):
    b, s, d = x.shape
    depth = w_in.shape[0]
    alpha = (2.0 * depth) ** 0.25
    tables = _retention_tables(s)
    for l in range(depth):
        x = _mixer(x, w_in[l].astype(_BF), w_out[l].astype(_BF), conv_w[l], conv_b[l],
                   conv_ln_g[l], conv_ln_b[l], ret_gn_g[l], ln1_g[l], ln1_b[l], tables, alpha)
        h = x.reshape(b * s, d)
        i = l // 2
        if l % 2 == 0:
            h = _dense_ffn(h, dense_w1[i].astype(_BF), dense_w3[i].astype(_BF),
                           dense_w2[i].astype(_BF), ln2_g[l], ln2_b[l], alpha)
        else:
            h = _moe_ffn(h, moe_router[i], moe_w1[i].astype(_BF), moe_w3[i].astype(_BF),
                         moe_w2[i].astype(_BF), ln2_g[l], ln2_b[l], alpha)
        x = h.reshape(b, s, d)
    return x
```

```python
import functools

import jax
import jax.numpy as jnp
from jax import lax
from jax.experimental import pallas as pl
from jax.experimental.pallas import tpu as pltpu
from jax.experimental.pallas import tpu_sc as plsc

CHUNK = 64
CONV_CH = 512
CONV_K = 31
RET_HEADS = 4
RET_DK = 64
RET_DV = 128
RET_QK_W = RET_HEADS * RET_DK
RET_V_W = RET_HEADS * RET_DV
ROPE_BASE = 10000.0
N_EXPERTS = 8
LN_EPS = 1e-5

LANES = 128
SUBLANES = 8
SEQ_TILE = 512
MIXER_STREAMS = 1
MIXER_PHASES = 4
RET_BLOCK = 128
CONV_ROWS = 32
HALO = 32
FFN_TILE = 512
FFN_CHUNK = 512
ROUTE_TILE = 1024
MOVE_TILE = 512
SC_CORES = 2
SC_SUBCORES = 16
SC_ROWS = 32
BATCH_GROUPS = 2
VMEM_LIMIT = 56 * 1024 * 1024

_BF = jnp.bfloat16
_F32 = jnp.float32


def _dot(a, b):
    return jnp.dot(a, b, preferred_element_type=_F32)


def _layer_norm(v, g, b):
    mu = jnp.mean(v, axis=-1, keepdims=True)
    d = v - mu
    var = jnp.mean(d * d, axis=-1, keepdims=True)
    return d * lax.rsqrt(var + LN_EPS) * g + b


def _sigmoid(v):
    return 1.0 / (1.0 + jnp.exp(-v))


def _const_spec(shape):
    nd = len(shape)
    return pl.BlockSpec(shape, lambda *_: (0,) * nd, pipeline_mode=pl.Buffered(1))


def _mixer_kernel(alpha, x_ref, *refs):
    *shared, o_ref, ush, state, rbuf = refs

    @pl.when(pl.program_id(1) == 0)
    def _():
        ush[:, 0, 0:HALO, :] = jnp.zeros((ush.shape[0], HALO, CONV_CH), _F32)
        state[...] = jnp.zeros_like(state)

    ns = x_ref.shape[0]
    streams = [_mixer_stream(alpha, x_ref.at[p], *shared, o_ref.at[p], ush.at[p], state.at[p],
                             rbuf.at[p]) for p in range(ns)]
    for step in range(MIXER_PHASES + ns - 1):
        for p in range(ns):
            if 0 <= step - p < MIXER_PHASES:
                next(streams[p], None)


def _mixer_stream(alpha, x_ref, w_in_ref, w_out_ref, conv_w_ref, conv_b_ref, cg_ref, cb_ref,
                  gn_ref, lg_ref, lb_ref, cos_ref, sin_ref, dmask_ref, qdec_ref, kdec_ref,
                  sdec_ref, bd_ref, o_ref, ush, state, rbuf):
    t = x_ref.shape[0]
    ubuf = ush.at[0]
    x = x_ref[...]
    xb = x.astype(_BF)

    c0 = 2 * CONV_CH
    ab = _dot(xb, w_in_ref[:, 0:c0])
    qk = _dot(xb, w_in_ref[:, c0:c0 + 2 * RET_QK_W])
    v = _dot(xb, w_in_ref[:, c0 + 2 * RET_QK_W:c0 + 2 * RET_QK_W + RET_V_W])
    g = _dot(xb, w_in_ref[:, c0 + 2 * RET_QK_W + RET_V_W:])
    yield

    ubuf[HALO:HALO + t, :] = ab[:, :CONV_CH] * _sigmoid(ab[:, CONV_CH:])
    span = t + HALO - SUBLANES
    for r in range(1, SUBLANES):
        ush[r, 0:span, :] = ubuf[r:r + span, :]
    off = HALO - (CONV_K - 1)
    conv_b = conv_b_ref[...]
    blocks = []
    for r0 in range(0, t, CONV_ROWS):
        acc = jnp.broadcast_to(conv_b, (CONV_ROWS, CONV_CH))
        for r in range(SUBLANES):
            taps = [j for j in range(CONV_K) if (j + off) % SUBLANES == r]
            reach = max((j + off) // SUBLANES for j in taps) * SUBLANES
            seg = ush[r, r0:r0 + reach + CONV_ROWS, :]
            for j in taps:
                a = (j + off) // SUBLANES * SUBLANES
                acc = acc + conv_w_ref[j:j + 1, :] * seg[a:a + CONV_ROWS]
        blocks.append(acc)
    conv = jnp.concatenate(blocks, axis=0)
    ubuf[0:HALO, :] = ubuf[t:t + HALO, :]
    un = _layer_norm(conv, cg_ref[...], cb_ref[...])
    u_out = un * _sigmoid(un)

    yield

    cos_t = cos_ref[...]
    sin_t = sin_ref[...]
    lane = lax.broadcasted_iota(jnp.int32, (1, LANES), 1)
    first_half = (lane % RET_DK) < (RET_DK // 2)

    def rope(z):
        parts = []
        for c in range(0, RET_QK_W, LANES):
            zc = z[:, c:c + LANES]
            up = pltpu.roll(zc, LANES - RET_DK // 2, axis=1)
            dn = pltpu.roll(zc, RET_DK // 2, axis=1)
            parts.append(jnp.where(first_half, up, dn))
        return z * cos_t + jnp.concatenate(parts, axis=1) * sin_t

    q = rope(qk[:, :RET_QK_W])
    k = rope(qk[:, RET_QK_W:])

    lane_qk = lax.broadcasted_iota(jnp.int32, (1, RET_QK_W), 1)
    dmask = dmask_ref[...]
    qdec = qdec_ref[...]
    kdec = kdec_ref[...]
    nr = RET_BLOCK
    for s in range(t // nr):
        rows = slice(s * nr, (s + 1) * nr)
        q_s, k_s, v_s = q[rows], k[rows], v[rows]
        v_b = v_s.astype(_BF)
        kbd = jnp.concatenate(
            [jnp.where(lane_qk // RET_DK == h, k_s, 0.0).astype(_BF) for h in range(RET_HEADS)],
            axis=0)
        sc = lax.dot_general(q_s.astype(_BF), kbd, (((1,), (1,)), ((), ())),
                             preferred_element_type=_F32)
        p = (sc * dmask).astype(_BF)
        intra = jnp.concatenate(
            [_dot(p[:, h * nr:(h + 1) * nr], v_b[:, h * RET_DV:(h + 1) * RET_DV])
             for h in range(RET_HEADS)], axis=1)
        st = state[...]
        cross = _dot((q_s * qdec).astype(_BF), st.astype(_BF))
        kv = lax.dot_general((k_s * kdec).astype(_BF), v_b, (((0,), (0,)), ((), ())),
                             preferred_element_type=_F32)
        state[...] = st * sdec_ref[...] + kv * bd_ref[...]
        rbuf[rows, :] = intra + cross

    yield

    r = rbuf[...]
    gate = g * _sigmoid(g)
    gn = gn_ref[...]
    r_parts = []
    for h in range(RET_HEADS):
        hs = slice(h * RET_DV, (h + 1) * RET_DV)
        rh = r[:, hs]
        mu = jnp.mean(rh, axis=-1, keepdims=True)
        d = rh - mu
        var = jnp.mean(d * d, axis=-1, keepdims=True)
        r_parts.append(gate[:, hs] * (d * lax.rsqrt(var + LN_EPS) * gn[:, hs]))

    mixed = jnp.concatenate([u_out] + r_parts, axis=1).astype(_BF)
    mix = _dot(mixed, w_out_ref[...])
    o_ref[...] = _layer_norm(alpha * x + mix, lg_ref[...], lb_ref[...])


def _retention_tables(seq):
    half = RET_DK // 2
    inv_freq = ROPE_BASE ** (-jnp.arange(half, dtype=_F32) / half)
    ang = jnp.arange(seq, dtype=jnp.int32).astype(_F32)[:, None] * inv_freq[None, :]
    cos, sin = jnp.cos(ang), jnp.sin(ang)
    cos_t = jnp.tile(jnp.concatenate([cos, cos], axis=1), (1, RET_HEADS))
    sin_t = jnp.tile(jnp.concatenate([-sin, sin], axis=1), (1, RET_HEADS))

    nr = RET_BLOCK
    log_gamma = jnp.log1p(-(2.0 ** (-5.0 - jnp.arange(RET_HEADS, dtype=_F32))))
    idx = jnp.arange(nr, dtype=_F32)
    dist = jnp.abs(idx[:, None] - idx[None, :])
    visible = (jnp.arange(nr)[None, :] // CHUNK) <= (jnp.arange(nr)[:, None] // CHUNK)
    scale = RET_DK ** -0.5
    dm = jnp.exp(log_gamma[:, None, None] * dist[None]) * visible[None] * scale
    dmask = jnp.transpose(dm, (1, 0, 2)).reshape(nr, RET_HEADS * nr)
    qdec = jnp.repeat(jnp.exp(log_gamma[None, :] * (idx + 1.0)[:, None]), RET_DK, axis=1)
    kdec = jnp.repeat(jnp.exp(log_gamma[None, :] * (nr - 1.0 - idx)[:, None]), RET_DK, axis=1) * scale
    sdec = jnp.broadcast_to(jnp.repeat(jnp.exp(log_gamma * nr), RET_DK)[:, None],
                            (RET_QK_W, RET_V_W))
    bd = (jnp.arange(RET_QK_W)[:, None] // RET_DK == jnp.arange(RET_V_W)[None, :] // RET_DV)
    return cos_t, sin_t, dmask, qdec, kdec, sdec, bd.astype(_F32)


def _mixer(x, w_in, w_out, conv_w, conv_b, cg, cb, gn, lg, lb, tables, alpha):
    b, s, d = x.shape
    t = min(SEQ_TILE, s)
    ns = MIXER_STREAMS if b % MIXER_STREAMS == 0 else 1
    assert s % t == 0 and t % RET_BLOCK == 0 and RET_BLOCK % CHUNK == 0 and t % CONV_ROWS == 0
    cos_t, sin_t, dmask, qdec, kdec, sdec, bd = tables
    row = lambda a: a.reshape(1, -1)
    conv_w = jnp.pad(conv_w, ((0, HALO - CONV_K), (0, 0)))
    consts = [w_in, w_out, conv_w, row(conv_b), row(cg), row(cb), row(gn), row(lg), row(lb)]
    tail = [dmask, qdec, kdec, sdec, bd]
    x_spec = pl.BlockSpec((ns, None, t, d), lambda i, j: (0, i, j, 0))
    in_specs = ([x_spec]
                + [_const_spec(a.shape) for a in consts]
                + [pl.BlockSpec((t, RET_QK_W), lambda i, j: (j, 0))] * 2
                + [_const_spec(a.shape) for a in tail])
    out = pl.pallas_call(
        functools.partial(_mixer_kernel, alpha),
        grid=(b // ns, s // t),
        in_specs=in_specs,
        out_specs=x_spec,
        out_shape=jax.ShapeDtypeStruct((ns, b // ns, s, d), _F32),
        scratch_shapes=[pltpu.VMEM((ns, SUBLANES, t + HALO, CONV_CH), _F32),
                        pltpu.VMEM((ns, RET_QK_W, RET_V_W), _F32),
                        pltpu.VMEM((ns, t, RET_V_W), _F32)],
        compiler_params=pltpu.CompilerParams(
            dimension_semantics=("arbitrary", "arbitrary"), vmem_limit_bytes=VMEM_LIMIT),
        name="mixer",
    )(x.reshape(ns, b // ns, s, d), *consts, cos_t, sin_t, *tail)
    return out.reshape(b, s, d)


def _swiglu_tile(xb, w1_ref, w3_ref, w2_ref):
    d_ff = w1_ref.shape[1]
    acc = None
    for c in range(0, d_ff, FFN_CHUNK):
        a = _dot(xb, w1_ref[:, c:c + FFN_CHUNK])
        g = _dot(xb, w3_ref[:, c:c + FFN_CHUNK])
        h = (a * _sigmoid(a) * g).astype(_BF)
        part = _dot(h, w2_ref[c:c + FFN_CHUNK, :])
        acc = part if acc is None else acc + part
    return acc


def _dense_ffn_kernel(alpha, x_ref, w1_ref, w3_ref, w2_ref, lg_ref, lb_ref, o_ref):
    x = x_ref[...]
    f = _swiglu_tile(x.astype(_BF), w1_ref, w3_ref, w2_ref)
    o_ref[...] = _layer_norm(alpha * x + f, lg_ref[...], lb_ref[...])


def _dense_ffn(x, w1, w3, w2, lg, lb, alpha):
    n, d = x.shape
    tm = min(FFN_TILE, n)
    assert n % tm == 0 and w1.shape[1] % FFN_CHUNK == 0
    consts = [w1, w3, w2, lg.reshape(1, -1), lb.reshape(1, -1)]
    return pl.pallas_call(
        functools.partial(_dense_ffn_kernel, alpha),
        grid=(n // tm,),
        in_specs=[pl.BlockSpec((tm, d), lambda i: (i, 0))] + [_const_spec(a.shape) for a in consts],
        out_specs=pl.BlockSpec((tm, d), lambda i: (i, 0)),
        out_shape=jax.ShapeDtypeStruct((n, d), _F32),
        compiler_params=pltpu.CompilerParams(
            dimension_semantics=("arbitrary",), vmem_limit_bytes=VMEM_LIMIT),
        name="dense_ffn",
    )(x, *consts)


def _router_kernel(x_ref, wr_ref, tri_ref, ri_ref, rf_ref, cnt_ref, carry):
    @pl.when(pl.program_id(0) == 0)
    def _():
        carry[...] = jnp.zeros_like(carry)

    tr = x_ref.shape[0]
    logits = lax.dot_general(wr_ref[...], x_ref[...].astype(_BF), (((1,), (1,)), ((), ())),
                             preferred_element_type=_F32)
    eidx = lax.broadcasted_iota(jnp.int32, (N_EXPERTS, tr), 0)
    m0 = jnp.max(logits, axis=0, keepdims=True)
    e0 = jnp.min(jnp.where(logits == m0, eidx, N_EXPERTS), axis=0, keepdims=True)
    rest = jnp.where(eidx == e0, -jnp.inf, logits)
    m1 = jnp.max(rest, axis=0, keepdims=True)
    e1 = jnp.min(jnp.where(rest == m1, eidx, N_EXPERTS), axis=0, keepdims=True)
    tt = jnp.exp(m1 - m0)
    g0 = 1.0 / (1.0 + tt)
    g1 = tt / (1.0 + tt)

    oh0 = eidx == e0
    oh1 = eidx == e1
    member = jnp.where(oh0 | oh1, 1.0, 0.0)
    before = _dot(member.astype(_BF), tri_ref[...]) + carry[:, 0:1]
    rank0 = jnp.sum(jnp.where(oh0, before, 0.0), axis=0, keepdims=True)
    rank1 = jnp.sum(jnp.where(oh1, before, 0.0), axis=0, keepdims=True)
    carry[...] = carry[...] + jnp.sum(member, axis=1, keepdims=True)

    zi = jnp.zeros((N_EXPERTS - 4, tr), jnp.int32)
    ri_ref[...] = jnp.concatenate(
        [e0, e1, rank0.astype(jnp.int32), rank1.astype(jnp.int32), zi], axis=0)
    rf_ref[...] = jnp.concatenate([g0, g1, jnp.zeros((N_EXPERTS - 2, tr), _F32)], axis=0)
    cnt_ref[...] = carry[...]


def _router(x, wr_t):
    n, d = x.shape
    tr = min(ROUTE_TILE, n)
    assert n % tr == 0
    tri = (jnp.arange(tr)[:, None] < jnp.arange(tr)[None, :]).astype(_BF)
    return pl.pallas_call(
        _router_kernel,
        grid=(n // tr,),
        in_specs=[pl.BlockSpec((tr, d), lambda i: (i, 0)),
                  _const_spec(wr_t.shape), _const_spec(tri.shape)],
        out_specs=[pl.BlockSpec((N_EXPERTS, tr), lambda i: (0, i)),
                   pl.BlockSpec((N_EXPERTS, tr), lambda i: (0, i)),
                   pl.BlockSpec((N_EXPERTS, LANES), lambda i: (0, 0))],
        out_shape=[jax.ShapeDtypeStruct((N_EXPERTS, n), jnp.int32),
                   jax.ShapeDtypeStruct((N_EXPERTS, n), _F32),
                   jax.ShapeDtypeStruct((N_EXPERTS, LANES), _F32)],
        scratch_shapes=[pltpu.VMEM((N_EXPERTS, LANES), _F32)],
        compiler_params=pltpu.CompilerParams(
            dimension_semantics=("arbitrary",), vmem_limit_bytes=VMEM_LIMIT),
        name="router",
    )(x, wr_t, tri)


def _for_each_row(n_rows, fn):
    def body(c, carry):
        for u in range(SUBLANES):
            fn(c, u)
        return carry

    lax.fori_loop(0, n_rows // SUBLANES, body, 0)


def _wait_rows(hbm_ref, n_rows, sem):
    rows = hbm_ref.at[pl.ds(0, n_rows)]
    pltpu.make_async_copy(rows, rows, sem).wait()


def _dispatch_kernel(pad_ref, pos_ref, x_ref, xs_ref, zbuf, sem):
    tm = x_ref.shape[0] * SUBLANES

    @pl.when(pl.program_id(0) == 0)
    def _():
        zbuf[...] = jnp.zeros_like(zbuf)
        for e in range(pad_ref.shape[0]):
            fill = pltpu.make_async_copy(
                zbuf, xs_ref.at[pl.ds(pl.multiple_of(pad_ref[e], SUBLANES), zbuf.shape[0])],
                sem.at[0])
            fill.start()
            fill.wait()

    def start(c, u):
        i = c * SUBLANES + u
        src = x_ref.at[c, pl.ds(u, 1)]
        pltpu.make_async_copy(src, xs_ref.at[pl.ds(pos_ref[0, i], 1)], sem.at[0]).start(priority=0)
        pltpu.make_async_copy(src, xs_ref.at[pl.ds(pos_ref[0, tm + i], 1)], sem.at[1]).start(priority=1)

    _for_each_row(tm, start)
    _wait_rows(xs_ref, tm, sem.at[0])
    _wait_rows(xs_ref, tm, sem.at[1])


def _move_tile(n):
    tm = min(MOVE_TILE, n)
    assert n % tm == 0 and tm % SUBLANES == 0
    return tm


def _tile_positions(pos, tm):
    n = pos.shape[1]
    return pos.reshape(2, n // tm, tm).transpose(1, 0, 2).reshape(n // tm, 1, 2 * tm)


def _dispatch(x, pos, pad_start, n_slots, pad_rows):
    n, d = x.shape
    tm = _move_tile(n)
    return pl.pallas_call(
        _dispatch_kernel,
        grid_spec=pltpu.PrefetchScalarGridSpec(
            num_scalar_prefetch=1,
            grid=(n // tm,),
            in_specs=[pl.BlockSpec((None, 1, 2 * tm), lambda i, pad: (i, 0, 0),
                                   memory_space=pltpu.SMEM),
                      pl.BlockSpec((tm // SUBLANES, SUBLANES, d), lambda i, pad: (i, 0, 0))],
            out_specs=pl.BlockSpec(memory_space=pl.ANY),
            scratch_shapes=[pltpu.VMEM((pad_rows, d), _F32), pltpu.SemaphoreType.DMA((2,))],
        ),
        out_shape=jax.ShapeDtypeStruct((n_slots, d), _F32),
        compiler_params=pltpu.CompilerParams(
            dimension_semantics=("arbitrary",), vmem_limit_bytes=VMEM_LIMIT,
            has_side_effects=True),
        name="dispatch",
    )(pad_start, _tile_positions(pos, tm), x.reshape(n // SUBLANES, SUBLANES, d))


def _expert_kernel(te_ref, na_ref, xs_ref, w1_ref, w3_ref, w2_ref, ys_ref):
    del te_ref
    active = pl.program_id(0) < na_ref[0]

    @pl.when(active)
    def _():
        ys_ref[...] = _swiglu_tile(xs_ref[...].astype(_BF), w1_ref, w3_ref, w2_ref)

    @pl.when(jnp.logical_not(active))
    def _():
        ys_ref[...] = jnp.zeros_like(ys_ref)


def _expert_ffn(xs, w1, w3, w2, tile_expert, n_active):
    d = xs.shape[1]
    tm = FFN_TILE
    d_ff = w1.shape[2]
    n_tiles = tile_expert.shape[0]
    n_slots = n_tiles * tm
    w_in_spec = pl.BlockSpec((None, d, d_ff), lambda i, te, na: (te[i], 0, 0),
                             pipeline_mode=pl.Buffered(1))
    w_out_spec = pl.BlockSpec((None, d_ff, d), lambda i, te, na: (te[i], 0, 0),
                              pipeline_mode=pl.Buffered(1))
    return pl.pallas_call(
        _expert_kernel,
        grid_spec=pltpu.PrefetchScalarGridSpec(
            num_scalar_prefetch=2,
            grid=(n_tiles,),
            in_specs=[pl.BlockSpec((tm, d), lambda i, te, na: (jnp.minimum(i, na[0] - 1), 0)),
                      w_in_spec, w_in_spec, w_out_spec],
            out_specs=pl.BlockSpec((tm, d), lambda i, te, na: (i, 0)),
        ),
        out_shape=jax.ShapeDtypeStruct((n_slots, d), _F32),
        compiler_params=pltpu.CompilerParams(
            dimension_semantics=("arbitrary",), vmem_limit_bytes=VMEM_LIMIT),
        name="expert_ffn",
    )(tile_expert, n_active, xs, w1, w3, w2)


def _sc_gather(table, idx):
    m, d = idx.shape[0], table.shape[1]
    workers = SC_CORES * SC_SUBCORES
    per_worker = m // workers
    assert m % (workers * SC_ROWS) == 0

    def body(table_hbm, idx_hbm, out_hbm, idx_v, rows_v, sem):
        base = (lax.axis_index("s") * SC_CORES + lax.axis_index("c")) * per_worker

        @pl.loop(0, per_worker // SC_ROWS)
        def _(c):
            off = pl.multiple_of(base + c * SC_ROWS, SC_ROWS)
            pltpu.sync_copy(idx_hbm.at[pl.ds(off, SC_ROWS)], idx_v)
            pltpu.async_copy(table_hbm.at[idx_v], rows_v, sem).wait()
            pltpu.sync_copy(rows_v, out_hbm.at[pl.ds(off, SC_ROWS)])

    return pl.kernel(
        body, out_type=jax.ShapeDtypeStruct((m, d), table.dtype), mesh=_sc_mesh(),
        scratch_types=[pltpu.VMEM((SC_ROWS,), jnp.int32), pltpu.VMEM((SC_ROWS, d), table.dtype),
                       pltpu.SemaphoreType.DMA],
        name="sc_gather")(table, idx)


def _sc_mesh():
    return plsc.VectorSubcoreMesh(core_axis_name="c", subcore_axis_name="s",
                                  num_cores=SC_CORES, num_subcores=SC_SUBCORES)


def _sc_scatter(x, pos, pad_slots, n_slots):
    n, d = x.shape
    workers = SC_CORES * SC_SUBCORES
    per_worker = n // workers
    pad_per_worker = pad_slots.shape[0] // workers
    assert n % (workers * SC_ROWS) == 0 and pad_slots.shape[0] % (workers * SC_ROWS) == 0
    assert pos.shape[0] == 2 * n and n_slots == 2 * n + pad_slots.shape[0]

    def body(x_hbm, pos_hbm, pad_hbm, out_hbm, idx_v, rows_v):
        wid = lax.axis_index("s") * SC_CORES + lax.axis_index("c")

        @pl.loop(0, per_worker // SC_ROWS)
        def _(c):
            off = pl.multiple_of(wid * per_worker + c * SC_ROWS, SC_ROWS)
            pltpu.sync_copy(x_hbm.at[pl.ds(off, SC_ROWS)], rows_v)
            for choice in range(2):
                pltpu.sync_copy(pos_hbm.at[pl.ds(choice * n + off, SC_ROWS)], idx_v)
                pltpu.sync_copy(rows_v, out_hbm.at[idx_v])

        @pl.loop(0, pad_per_worker // SC_ROWS)
        def _(c):
            off = pl.multiple_of(wid * pad_per_worker + c * SC_ROWS, SC_ROWS)
            pltpu.sync_copy(pad_hbm.at[pl.ds(off, SC_ROWS)], idx_v)
            pltpu.sync_copy(rows_v, out_hbm.at[idx_v])

    return pl.kernel(
        body, out_type=jax.ShapeDtypeStruct((n_slots, d), x.dtype), mesh=_sc_mesh(),
        scratch_types=[pltpu.VMEM((SC_ROWS,), jnp.int32), pltpu.VMEM((SC_ROWS, d), x.dtype)],
        name="sc_scatter")(x, pos, pad_slots)


def _combine_dense_kernel(alpha, x_ref, y0_ref, y1_ref, gate_ref, lg_ref, lb_ref, o_ref):
    gate = gate_ref[...]
    f = y0_ref[...] * gate[:, 0:1] + y1_ref[...] * gate[:, 1:2]
    o_ref[...] = _layer_norm(alpha * x_ref[...] + f, lg_ref[...], lb_ref[...])


def _combine_dense(x, y01, gates, lg, lb, alpha):
    n, d = x.shape
    tm = _move_tile(n)
    return pl.pallas_call(
        functools.partial(_combine_dense_kernel, alpha),
        grid=(n // tm,),
        in_specs=[pl.BlockSpec((tm, d), lambda i: (i, 0)),
                  pl.BlockSpec((tm, d), lambda i: (i, 0)),
                  pl.BlockSpec((tm, d), lambda i: (i + n // tm, 0)),
                  pl.BlockSpec((tm, 2), lambda i: (i, 0)),
                  _const_spec((1, d)), _const_spec((1, d))],
        out_specs=pl.BlockSpec((tm, d), lambda i: (i, 0)),
        out_shape=jax.ShapeDtypeStruct((n, d), _F32),
        compiler_params=pltpu.CompilerParams(
            dimension_semantics=("arbitrary",), vmem_limit_bytes=VMEM_LIMIT),
        name="combine_dense",
    )(x, y01, y01, gates, lg.reshape(1, -1), lb.reshape(1, -1))


def _combine_kernel(alpha, pos_ref, x_ref, gate_ref, lg_ref, lb_ref, ys_ref, o_ref, buf0, buf1, sem):
    tm, d = x_ref.shape

    def start(c, u):
        i = c * SUBLANES + u
        pltpu.make_async_copy(ys_ref.at[pl.ds(pos_ref[0, i], 1)], buf0.at[c, pl.ds(u, 1)],
                              sem.at[0]).start(priority=0)
        pltpu.make_async_copy(ys_ref.at[pl.ds(pos_ref[0, tm + i], 1)], buf1.at[c, pl.ds(u, 1)],
                              sem.at[1]).start(priority=1)

    _for_each_row(tm, start)
    _wait_rows(ys_ref, tm, sem.at[0])
    _wait_rows(ys_ref, tm, sem.at[1])
    gate = gate_ref[...]
    f = buf0[...].reshape(tm, d) * gate[:, 0:1] + buf1[...].reshape(tm, d) * gate[:, 1:2]
    o_ref[...] = _layer_norm(alpha * x_ref[...] + f, lg_ref[...], lb_ref[...])


def _combine(x, ys, pos, gates, lg, lb, alpha):
    n, d = x.shape
    tm = _move_tile(n)
    buf = pltpu.VMEM((tm // SUBLANES, SUBLANES, d), _F32)
    return pl.pallas_call(
        functools.partial(_combine_kernel, alpha),
        grid=(n // tm,),
        in_specs=[pl.BlockSpec((None, 1, 2 * tm), lambda i: (i, 0, 0), memory_space=pltpu.SMEM),
                  pl.BlockSpec((tm, d), lambda i: (i, 0)),
                  pl.BlockSpec((tm, 2), lambda i: (i, 0)),
                  _const_spec((1, d)), _const_spec((1, d)),
                  pl.BlockSpec(memory_space=pl.ANY)],
        out_specs=pl.BlockSpec((tm, d), lambda i: (i, 0)),
        out_shape=jax.ShapeDtypeStruct((n, d), _F32),
        scratch_shapes=[buf, buf, pltpu.SemaphoreType.DMA((2,))],
        compiler_params=pltpu.CompilerParams(
            dimension_semantics=("arbitrary",), vmem_limit_bytes=VMEM_LIMIT),
        name="combine",
    )(_tile_positions(pos, tm), x, gates, lg.reshape(1, -1), lb.reshape(1, -1), ys)


def _moe_ffn(x, router, w1, w3, w2, lg, lb, alpha):
    n, d = x.shape
    tm = FFN_TILE
    n_tiles = -(-(2 * n) // tm) + N_EXPERTS
    n_slots = n_tiles * tm

    ri, rf, cnt = _router(x, router.T.astype(_BF))
    counts = cnt[:, 0].astype(jnp.int32)
    padded = (counts + tm - 1) // tm * tm
    group_end = jnp.cumsum(padded)
    group_start = group_end - padded
    experts = jnp.arange(N_EXPERTS, dtype=jnp.int32)[:, None]

    def slot(e, rank):
        return jnp.sum(jnp.where(e[None, :] == experts, group_start[:, None], 0), axis=0) + rank

    pos = jnp.stack([slot(ri[0], ri[2]), slot(ri[1], ri[3])])
    tile_start = jnp.arange(n_tiles, dtype=jnp.int32) * tm
    tile_expert = jnp.minimum(
        jnp.sum(tile_start[:, None] >= group_end[None, :], axis=1), N_EXPERTS - 1).astype(jnp.int32)
    n_active = (group_end[-1:] // tm).astype(jnp.int32)

    pad_len = jnp.concatenate([padded - counts, n_slots - group_end[-1:]])
    pad_begin = jnp.concatenate([group_start + counts, group_end[-1:]])
    pad_end = jnp.cumsum(pad_len)
    k = jnp.arange(n_slots - 2 * n, dtype=jnp.int32)
    run = jnp.sum(k[:, None] >= pad_end[None, :], axis=1)
    pad_slots = k + jnp.sum(
        jnp.where(run[:, None] == jnp.arange(N_EXPERTS + 1)[None, :],
                  (pad_begin - (pad_end - pad_len))[None, :], 0), axis=1)
    xs = _sc_scatter(x, pos.reshape(-1), pad_slots.astype(jnp.int32), n_slots)
    ys = _expert_ffn(xs, w1, w3, w2, tile_expert, n_active)
    y01 = _sc_gather(ys, pos.reshape(-1))
    return _combine_dense(x, y01, rf[:2].T, lg, lb, alpha)


def kernel(x, w_in, w_out, conv_w, conv_b, conv_ln_g, conv_ln_b, ret_gn_g, ln1_g, ln1_b, ln2_g,
           ln2_b, dense_w1, dense_w3, dense_w2, moe_router, moe_w1, moe_w3, moe_w2):
    b, s, d = x.shape
    depth = w_in.shape[0]
    alpha = (2.0 * depth) ** 0.25
    tables = _retention_tables(s)
    ng = BATCH_GROUPS if b % BATCH_GROUPS == 0 else 1
    bg = b // ng
    groups = [x[g * bg:(g + 1) * bg] for g in range(ng)]
    for l in range(depth):
        i = l // 2
        mix_w = (w_in[l].astype(_BF), w_out[l].astype(_BF))
        if l % 2 == 0:
            ffn_w = (dense_w1[i].astype(_BF), dense_w3[i].astype(_BF), dense_w2[i].astype(_BF))
        else:
            ffn_w = (moe_w1[i].astype(_BF), moe_w3[i].astype(_BF), moe_w2[i].astype(_BF))
        for g in range(ng):
            xg = _mixer(groups[g], *mix_w, conv_w[l], conv_b[l], conv_ln_g[l], conv_ln_b[l],
                        ret_gn_g[l], ln1_g[l], ln1_b[l], tables, alpha)
            h = xg.reshape(bg * s, d)
            if l % 2 == 0:
                h = _dense_ffn(h, *ffn_w, ln2_g[l], ln2_b[l], alpha)
            else:
                h = _moe_ffn(h, moe_router[i], *ffn_w, ln2_g[l], ln2_b[l], alpha)
            groups[g] = h.reshape(bg, s, d)
    return jnp.concatenate(groups, axis=0)
```

```python
import functools

import jax
import jax.numpy as jnp
from jax import lax
from jax.experimental import pallas as pl
from jax.experimental.pallas import tpu as pltpu
from jax.experimental.pallas import tpu_sc as plsc

CHUNK = 64
CONV_CH = 512
CONV_K = 31
RET_HEADS = 4
RET_DK = 64
RET_DV = 128
RET_QK_W = RET_HEADS * RET_DK
RET_V_W = RET_HEADS * RET_DV
ROPE_BASE = 10000.0
N_EXPERTS = 8
LN_EPS = 1e-5

LANES = 128
SUBLANES = 8
SEQ_TILE = 512
RET_BLOCK = 128
CONV_ROWS = 32
HALO = 32
FFN_TILE = 512
FFN_CHUNK = 512
ROUTE_TILE = 1024
MOVE_TILE = 512
SC_CORES = 2
SC_SUBCORES = 16
SC_ROWS = 32
BATCH_GROUPS = 2
VMEM_LIMIT = 56 * 1024 * 1024

_BF = jnp.bfloat16
_F32 = jnp.float32


def _dot(a, b):
    return jnp.dot(a, b, preferred_element_type=_F32)


def _layer_norm(v, g, b):
    mu = jnp.mean(v, axis=-1, keepdims=True)
    d = v - mu
    var = jnp.mean(d * d, axis=-1, keepdims=True)
    return d * lax.rsqrt(var + LN_EPS) * g + b


def _sigmoid(v):
    return 1.0 / (1.0 + jnp.exp(-v))


def _const_spec(shape):
    nd = len(shape)
    return pl.BlockSpec(shape, lambda *_: (0,) * nd, pipeline_mode=pl.Buffered(1))


def _mixer_kernel(alpha, x_ref, w_in_ref, w_out_ref, conv_w_ref, conv_b_ref, cg_ref, cb_ref,
                  gn_ref, lg_ref, lb_ref, cos_ref, sin_ref, dmask_ref, qdec_ref, kdec_ref,
                  sdec_ref, bd_ref, o_ref, ush, state, rbuf):
    t = x_ref.shape[0]
    ubuf = ush.at[0]

    @pl.when(pl.program_id(1) == 0)
    def _():
        ubuf[0:HALO, :] = jnp.zeros((HALO, CONV_CH), _F32)
        state[...] = jnp.zeros_like(state)

    x = x_ref[...]
    xb = x.astype(_BF)

    c0 = 2 * CONV_CH
    ab = _dot(xb, w_in_ref[:, 0:c0])
    qk = _dot(xb, w_in_ref[:, c0:c0 + 2 * RET_QK_W])
    v = _dot(xb, w_in_ref[:, c0 + 2 * RET_QK_W:c0 + 2 * RET_QK_W + RET_V_W])
    g = _dot(xb, w_in_ref[:, c0 + 2 * RET_QK_W + RET_V_W:])

    ubuf[HALO:HALO + t, :] = ab[:, :CONV_CH] * _sigmoid(ab[:, CONV_CH:])
    span = t + HALO - SUBLANES
    for r in range(1, SUBLANES):
        ush[r, 0:span, :] = ubuf[r:r + span, :]
    off = HALO - (CONV_K - 1)
    conv_b = conv_b_ref[...]
    blocks = []
    for r0 in range(0, t, CONV_ROWS):
        acc = jnp.broadcast_to(conv_b, (CONV_ROWS, CONV_CH))
        for r in range(SUBLANES):
            taps = [j for j in range(CONV_K) if (j + off) % SUBLANES == r]
            reach = max((j + off) // SUBLANES for j in taps) * SUBLANES
            seg = ush[r, r0:r0 + reach + CONV_ROWS, :]
            for j in taps:
                a = (j + off) // SUBLANES * SUBLANES
                acc = acc + conv_w_ref[j:j + 1, :] * seg[a:a + CONV_ROWS]
        blocks.append(acc)
    conv = jnp.concatenate(blocks, axis=0)
    ubuf[0:HALO, :] = ubuf[t:t + HALO, :]
    un = _layer_norm(conv, cg_ref[...], cb_ref[...])
    u_out = un * _sigmoid(un)

    cos_t = cos_ref[...]
    sin_t = sin_ref[...]
    lane = lax.broadcasted_iota(jnp.int32, (1, LANES), 1)
    first_half = (lane % RET_DK) < (RET_DK // 2)

    def rope(z):
        parts = []
        for c in range(0, RET_QK_W, LANES):
            zc = z[:, c:c + LANES]
            up = pltpu.roll(zc, LANES - RET_DK // 2, axis=1)
            dn = pltpu.roll(zc, RET_DK // 2, axis=1)
            parts.append(jnp.where(first_half, up, dn))
        return z * cos_t + jnp.concatenate(parts, axis=1) * sin_t

    q = rope(qk[:, :RET_QK_W])
    k = rope(qk[:, RET_QK_W:])

    lane_qk = lax.broadcasted_iota(jnp.int32, (1, RET_QK_W), 1)
    dmask = dmask_ref[...]
    qdec = qdec_ref[...]
    kdec = kdec_ref[...]
    nr = RET_BLOCK
    for s in range(t // nr):
        rows = slice(s * nr, (s + 1) * nr)
        q_s, k_s, v_s = q[rows], k[rows], v[rows]
        v_b = v_s.astype(_BF)
        kbd = jnp.concatenate(
            [jnp.where(lane_qk // RET_DK == h, k_s, 0.0).astype(_BF) for h in range(RET_HEADS)],
            axis=0)
        sc = lax.dot_general(q_s.astype(_BF), kbd, (((1,), (1,)), ((), ())),
                             preferred_element_type=_F32)
        p = (sc * dmask).astype(_BF)
        intra = jnp.concatenate(
            [_dot(p[:, h * nr:(h + 1) * nr], v_b[:, h * RET_DV:(h + 1) * RET_DV])
             for h in range(RET_HEADS)], axis=1)
        st = state[...]
        cross = _dot((q_s * qdec).astype(_BF), st.astype(_BF))
        kv = lax.dot_general((k_s * kdec).astype(_BF), v_b, (((0,), (0,)), ((), ())),
                             preferred_element_type=_F32)
        state[...] = st * sdec_ref[...] + kv * bd_ref[...]
        rbuf[rows, :] = intra + cross

    r = rbuf[...]
    gate = g * _sigmoid(g)
    gn = gn_ref[...]
    r_parts = []
    for h in range(RET_HEADS):
        hs = slice(h * RET_DV, (h + 1) * RET_DV)
        rh = r[:, hs]
        mu = jnp.mean(rh, axis=-1, keepdims=True)
        d = rh - mu
        var = jnp.mean(d * d, axis=-1, keepdims=True)
        r_parts.append(gate[:, hs] * (d * lax.rsqrt(var + LN_EPS) * gn[:, hs]))

    mixed = jnp.concatenate([u_out] + r_parts, axis=1).astype(_BF)
    mix = _dot(mixed, w_out_ref[...])
    o_ref[...] = _layer_norm(alpha * x + mix, lg_ref[...], lb_ref[...])


def _retention_tables(seq):
    half = RET_DK // 2
    inv_freq = ROPE_BASE ** (-jnp.arange(half, dtype=_F32) / half)
    ang = jnp.arange(seq, dtype=jnp.int32).astype(_F32)[:, None] * inv_freq[None, :]
    cos, sin = jnp.cos(ang), jnp.sin(ang)
    cos_t = jnp.tile(jnp.concatenate([cos, cos], axis=1), (1, RET_HEADS))
    sin_t = jnp.tile(jnp.concatenate([-sin, sin], axis=1), (1, RET_HEADS))

    nr = RET_BLOCK
    log_gamma = jnp.log1p(-(2.0 ** (-5.0 - jnp.arange(RET_HEADS, dtype=_F32))))
    idx = jnp.arange(nr, dtype=_F32)
    dist = jnp.abs(idx[:, None] - idx[None, :])
    visible = (jnp.arange(nr)[None, :] // CHUNK) <= (jnp.arange(nr)[:, None] // CHUNK)
    scale = RET_DK ** -0.5
    dm = jnp.exp(log_gamma[:, None, None] * dist[None]) * visible[None] * scale
    dmask = jnp.transpose(dm, (1, 0, 2)).reshape(nr, RET_HEADS * nr)
    qdec = jnp.repeat(jnp.exp(log_gamma[None, :] * (idx + 1.0)[:, None]), RET_DK, axis=1)
    kdec = jnp.repeat(jnp.exp(log_gamma[None, :] * (nr - 1.0 - idx)[:, None]), RET_DK, axis=1) * scale
    sdec = jnp.broadcast_to(jnp.repeat(jnp.exp(log_gamma * nr), RET_DK)[:, None],
                            (RET_QK_W, RET_V_W))
    bd = (jnp.arange(RET_QK_W)[:, None] // RET_DK == jnp.arange(RET_V_W)[None, :] // RET_DV)
    return cos_t, sin_t, dmask, qdec, kdec, sdec, bd.astype(_F32)


def _mixer(x, batch0, nb, w_in, w_out, conv_w, conv_b, cg, cb, gn, lg, lb, tables, alpha):
    _, s, d = x.shape
    t = min(SEQ_TILE, s)
    assert s % t == 0 and t % RET_BLOCK == 0 and RET_BLOCK % CHUNK == 0 and t % CONV_ROWS == 0
    cos_t, sin_t, dmask, qdec, kdec, sdec, bd = tables
    row = lambda a: a.reshape(1, -1)
    conv_w = jnp.pad(conv_w, ((0, HALO - CONV_K), (0, 0)))
    consts = [w_in, w_out, conv_w, row(conv_b), row(cg), row(cb), row(gn), row(lg), row(lb)]
    tail = [dmask, qdec, kdec, sdec, bd]
    in_specs = ([pl.BlockSpec((None, t, d), lambda i, j: (i + batch0, j, 0))]
                + [_const_spec(a.shape) for a in consts]
                + [pl.BlockSpec((t, RET_QK_W), lambda i, j: (j, 0))] * 2
                + [_const_spec(a.shape) for a in tail])
    return pl.pallas_call(
        functools.partial(_mixer_kernel, alpha),
        grid=(nb, s // t),
        in_specs=in_specs,
        out_specs=pl.BlockSpec((None, t, d), lambda i, j: (i, j, 0)),
        out_shape=jax.ShapeDtypeStruct((nb, s, d), _F32),
        scratch_shapes=[pltpu.VMEM((SUBLANES, t + HALO, CONV_CH), _F32),
                        pltpu.VMEM((RET_QK_W, RET_V_W), _F32),
                        pltpu.VMEM((t, RET_V_W), _F32)],
        compiler_params=pltpu.CompilerParams(
            dimension_semantics=("arbitrary", "arbitrary"), vmem_limit_bytes=VMEM_LIMIT),
        name="mixer",
    )(x, *consts, cos_t, sin_t, *tail)


def _swiglu_tile(xb, w1_ref, w3_ref, w2_ref):
    d_ff = w1_ref.shape[1]
    acc = None
    for c in range(0, d_ff, FFN_CHUNK):
        a = _dot(xb, w1_ref[:, c:c + FFN_CHUNK])
        g = _dot(xb, w3_ref[:, c:c + FFN_CHUNK])
        h = (a * _sigmoid(a) * g).astype(_BF)
        part = _dot(h, w2_ref[c:c + FFN_CHUNK, :])
        acc = part if acc is None else acc + part
    return acc


def _dense_ffn_kernel(alpha, x_ref, w1_ref, w3_ref, w2_ref, lg_ref, lb_ref, o_ref):
    x = x_ref[...]
    f = _swiglu_tile(x.astype(_BF), w1_ref, w3_ref, w2_ref)
    o_ref[...] = _layer_norm(alpha * x + f, lg_ref[...], lb_ref[...])


def _dense_ffn(x, w1, w3, w2, lg, lb, alpha):
    n, d = x.shape
    tm = min(FFN_TILE, n)
    assert n % tm == 0 and w1.shape[1] % FFN_CHUNK == 0
    consts = [w1, w3, w2, lg.reshape(1, -1), lb.reshape(1, -1)]
    return pl.pallas_call(
        functools.partial(_dense_ffn_kernel, alpha),
        grid=(n // tm,),
        in_specs=[pl.BlockSpec((tm, d), lambda i: (i, 0))] + [_const_spec(a.shape) for a in consts],
        out_specs=pl.BlockSpec((tm, d), lambda i: (i, 0)),
        out_shape=jax.ShapeDtypeStruct((n, d), _F32),
        compiler_params=pltpu.CompilerParams(
            dimension_semantics=("arbitrary",), vmem_limit_bytes=VMEM_LIMIT),
        name="dense_ffn",
    )(x, *consts)


def _router_kernel(x_ref, wr_ref, tri_ref, ri_ref, rf_ref, cnt_ref, carry):
    @pl.when(pl.program_id(0) == 0)
    def _():
        carry[...] = jnp.zeros_like(carry)

    tr = x_ref.shape[0]
    logits = lax.dot_general(wr_ref[...], x_ref[...].astype(_BF), (((1,), (1,)), ((), ())),
                             preferred_element_type=_F32)
    eidx = lax.broadcasted_iota(jnp.int32, (N_EXPERTS, tr), 0)
    m0 = jnp.max(logits, axis=0, keepdims=True)
    e0 = jnp.min(jnp.where(logits == m0, eidx, N_EXPERTS), axis=0, keepdims=True)
    rest = jnp.where(eidx == e0, -jnp.inf, logits)
    m1 = jnp.max(rest, axis=0, keepdims=True)
    e1 = jnp.min(jnp.where(rest == m1, eidx, N_EXPERTS), axis=0, keepdims=True)
    tt = jnp.exp(m1 - m0)
    g0 = 1.0 / (1.0 + tt)
    g1 = tt / (1.0 + tt)

    oh0 = eidx == e0
    oh1 = eidx == e1
    member = jnp.where(oh0 | oh1, 1.0, 0.0)
    before = _dot(member.astype(_BF), tri_ref[...]) + carry[:, 0:1]
    rank0 = jnp.sum(jnp.where(oh0, before, 0.0), axis=0, keepdims=True)
    rank1 = jnp.sum(jnp.where(oh1, before, 0.0), axis=0, keepdims=True)
    carry[...] = carry[...] + jnp.sum(member, axis=1, keepdims=True)

    zi = jnp.zeros((N_EXPERTS - 4, tr), jnp.int32)
    ri_ref[...] = jnp.concatenate(
        [e0, e1, rank0.astype(jnp.int32), rank1.astype(jnp.int32), zi], axis=0)
    rf_ref[...] = jnp.concatenate([g0, g1, jnp.zeros((N_EXPERTS - 2, tr), _F32)], axis=0)
    cnt_ref[...] = carry[...]


def _router(x, wr_t):
    n, d = x.shape
    tr = min(ROUTE_TILE, n)
    assert n % tr == 0
    tri = (jnp.arange(tr)[:, None] < jnp.arange(tr)[None, :]).astype(_BF)
    return pl.pallas_call(
        _router_kernel,
        grid=(n // tr,),
        in_specs=[pl.BlockSpec((tr, d), lambda i: (i, 0)),
                  _const_spec(wr_t.shape), _const_spec(tri.shape)],
        out_specs=[pl.BlockSpec((N_EXPERTS, tr), lambda i: (0, i)),
                   pl.BlockSpec((N_EXPERTS, tr), lambda i: (0, i)),
                   pl.BlockSpec((N_EXPERTS, LANES), lambda i: (0, 0))],
        out_shape=[jax.ShapeDtypeStruct((N_EXPERTS, n), jnp.int32),
                   jax.ShapeDtypeStruct((N_EXPERTS, n), _F32),
                   jax.ShapeDtypeStruct((N_EXPERTS, LANES), _F32)],
        scratch_shapes=[pltpu.VMEM((N_EXPERTS, LANES), _F32)],
        compiler_params=pltpu.CompilerParams(
            dimension_semantics=("arbitrary",), vmem_limit_bytes=VMEM_LIMIT),
        name="router",
    )(x, wr_t, tri)


def _move_tile(n):
    tm = min(MOVE_TILE, n)
    assert n % tm == 0 and tm % SUBLANES == 0
    return tm


def _expert_kernel(te_ref, na_ref, xs_ref, w1_ref, w3_ref, w2_ref, ys_ref):
    del te_ref
    active = pl.program_id(0) < na_ref[0]

    @pl.when(active)
    def _():
        ys_ref[...] = _swiglu_tile(xs_ref[...].astype(_BF), w1_ref, w3_ref, w2_ref)

    @pl.when(jnp.logical_not(active))
    def _():
        ys_ref[...] = jnp.zeros_like(ys_ref)


def _expert_ffn(xs, w1, w3, w2, tile_expert, n_active):
    d = xs.shape[1]
    tm = FFN_TILE
    d_ff = w1.shape[2]
    n_tiles = tile_expert.shape[0]
    n_slots = n_tiles * tm
    w_in_spec = pl.BlockSpec((None, d, d_ff), lambda i, te, na: (te[i], 0, 0),
                             pipeline_mode=pl.Buffered(1))
    w_out_spec = pl.BlockSpec((None, d_ff, d), lambda i, te, na: (te[i], 0, 0),
                              pipeline_mode=pl.Buffered(1))
    return pl.pallas_call(
        _expert_kernel,
        grid_spec=pltpu.PrefetchScalarGridSpec(
            num_scalar_prefetch=2,
            grid=(n_tiles,),
            in_specs=[pl.BlockSpec((tm, d), lambda i, te, na: (jnp.minimum(i, na[0] - 1), 0)),
                      w_in_spec, w_in_spec, w_out_spec],
            out_specs=pl.BlockSpec((tm, d), lambda i, te, na: (i, 0)),
        ),
        out_shape=jax.ShapeDtypeStruct((n_slots, d), _F32),
        compiler_params=pltpu.CompilerParams(
            dimension_semantics=("arbitrary",), vmem_limit_bytes=VMEM_LIMIT),
        name="expert_ffn",
    )(tile_expert, n_active, xs, w1, w3, w2)


def _sc_gather(table, idx):
    m, d = idx.shape[0], table.shape[1]
    workers = SC_CORES * SC_SUBCORES
    per_worker = m // workers
    assert m % (workers * SC_ROWS) == 0

    def body(table_hbm, idx_hbm, out_hbm, idx_v, rows_v, sem):
        base = (lax.axis_index("s") * SC_CORES + lax.axis_index("c")) * per_worker

        @pl.loop(0, per_worker // SC_ROWS)
        def _(c):
            off = pl.multiple_of(base + c * SC_ROWS, SC_ROWS)
            pltpu.sync_copy(idx_hbm.at[pl.ds(off, SC_ROWS)], idx_v)
            pltpu.async_copy(table_hbm.at[idx_v], rows_v, sem).wait()
            pltpu.sync_copy(rows_v, out_hbm.at[pl.ds(off, SC_ROWS)])

    return pl.kernel(
        body, out_type=jax.ShapeDtypeStruct((m, d), table.dtype), mesh=_sc_mesh(),
        scratch_types=[pltpu.VMEM((SC_ROWS,), jnp.int32), pltpu.VMEM((SC_ROWS, d), table.dtype),
                       pltpu.SemaphoreType.DMA],
        name="sc_gather")(table, idx)


def _sc_mesh():
    return plsc.VectorSubcoreMesh(core_axis_name="c", subcore_axis_name="s",
                                  num_cores=SC_CORES, num_subcores=SC_SUBCORES)


def _sc_scatter(x, pos, pad_slots, n_slots):
    n, d = x.shape
    workers = SC_CORES * SC_SUBCORES
    per_worker = n // workers
    pad_per_worker = pad_slots.shape[0] // workers
    assert n % (workers * SC_ROWS) == 0 and pad_slots.shape[0] % (workers * SC_ROWS) == 0
    assert pos.shape[0] == 2 * n and n_slots == 2 * n + pad_slots.shape[0]

    def body(x_hbm, pos_hbm, pad_hbm, out_hbm, idx_v, rows_v):
        wid = lax.axis_index("s") * SC_CORES + lax.axis_index("c")

        @pl.loop(0, per_worker // SC_ROWS)
        def _(c):
            off = pl.multiple_of(wid * per_worker + c * SC_ROWS, SC_ROWS)
            pltpu.sync_copy(x_hbm.at[pl.ds(off, SC_ROWS)], rows_v)
            for choice in range(2):
                pltpu.sync_copy(pos_hbm.at[pl.ds(choice * n + off, SC_ROWS)], idx_v)
                pltpu.sync_copy(rows_v, out_hbm.at[idx_v])

        @pl.loop(0, pad_per_worker // SC_ROWS)
        def _(c):
            off = pl.multiple_of(wid * pad_per_worker + c * SC_ROWS, SC_ROWS)
            pltpu.sync_copy(pad_hbm.at[pl.ds(off, SC_ROWS)], idx_v)
            pltpu.sync_copy(rows_v, out_hbm.at[idx_v])

    return pl.kernel(
        body, out_type=jax.ShapeDtypeStruct((n_slots, d), x.dtype), mesh=_sc_mesh(),
        scratch_types=[pltpu.VMEM((SC_ROWS,), jnp.int32), pltpu.VMEM((SC_ROWS, d), x.dtype)],
        name="sc_scatter")(x, pos, pad_slots)


def _combine_kernel(alpha, n_groups, *refs):
    lg_ref, lb_ref, o_ref = refs[-3:]
    steps = pl.num_programs(0) // n_groups
    i = pl.program_id(0)
    for g in range(n_groups):
        x_ref, y0_ref, y1_ref, gate_ref = refs[4 * g:4 * g + 4]

        @pl.when((i >= g * steps) & (i < (g + 1) * steps))
        def _(x_ref=x_ref, y0_ref=y0_ref, y1_ref=y1_ref, gate_ref=gate_ref):
            gate = gate_ref[...]
            f = y0_ref[...] * gate[:, 0:1] + y1_ref[...] * gate[:, 1:2]
            o_ref[...] = _layer_norm(alpha * x_ref[...] + f, lg_ref[...], lb_ref[...])


def _combine(parts, lg, lb, alpha):
    n, d = parts[0][0].shape
    tm = _move_tile(n)
    steps = n // tm
    in_specs, args = [], []
    for g, (x, y01, gates) in enumerate(parts):
        row = lambda i, g=g: jnp.clip(i - g * steps, 0, steps - 1)
        in_specs += [pl.BlockSpec((tm, d), lambda i, row=row: (row(i), 0)),
                     pl.BlockSpec((tm, d), lambda i, row=row: (row(i), 0)),
                     pl.BlockSpec((tm, d), lambda i, row=row: (row(i) + steps, 0)),
                     pl.BlockSpec((tm, 2), lambda i, row=row: (row(i), 0))]
        args += [x, y01, y01, gates]
    return pl.pallas_call(
        functools.partial(_combine_kernel, alpha, len(parts)),
        grid=(len(parts) * steps,),
        in_specs=in_specs + [_const_spec((1, d)), _const_spec((1, d))],
        out_specs=pl.BlockSpec((tm, d), lambda i: (i, 0)),
        out_shape=jax.ShapeDtypeStruct((len(parts) * n, d), _F32),
        compiler_params=pltpu.CompilerParams(
            dimension_semantics=("arbitrary",), vmem_limit_bytes=VMEM_LIMIT),
        name="combine",
    )(*args, lg.reshape(1, -1), lb.reshape(1, -1))


def _moe_experts(x, router, w1, w3, w2):
    n, d = x.shape
    tm = FFN_TILE
    n_tiles = -(-(2 * n) // tm) + N_EXPERTS
    n_slots = n_tiles * tm

    ri, rf, cnt = _router(x, router.T.astype(_BF))
    counts = cnt[:, 0].astype(jnp.int32)
    padded = (counts + tm - 1) // tm * tm
    group_end = jnp.cumsum(padded)
    group_start = group_end - padded
    experts = jnp.arange(N_EXPERTS, dtype=jnp.int32)[:, None]

    def slot(e, rank):
        return jnp.sum(jnp.where(e[None, :] == experts, group_start[:, None], 0), axis=0) + rank

    pos = jnp.stack([slot(ri[0], ri[2]), slot(ri[1], ri[3])])
    tile_start = jnp.arange(n_tiles, dtype=jnp.int32) * tm
    tile_expert = jnp.minimum(
        jnp.sum(tile_start[:, None] >= group_end[None, :], axis=1), N_EXPERTS - 1).astype(jnp.int32)
    n_active = (group_end[-1:] // tm).astype(jnp.int32)

    pad_len = jnp.concatenate([padded - counts, n_slots - group_end[-1:]])
    pad_begin = jnp.concatenate([group_start + counts, group_end[-1:]])
    pad_end = jnp.cumsum(pad_len)
    k = jnp.arange(n_slots - 2 * n, dtype=jnp.int32)
    run = jnp.sum(k[:, None] >= pad_end[None, :], axis=1)
    pad_slots = k + jnp.sum(
        jnp.where(run[:, None] == jnp.arange(N_EXPERTS + 1)[None, :],
                  (pad_begin - (pad_end - pad_len))[None, :], 0), axis=1)
    xs = _sc_scatter(x, pos.reshape(-1), pad_slots.astype(jnp.int32), n_slots)
    ys = _expert_ffn(xs, w1, w3, w2, tile_expert, n_active)
    return x, _sc_gather(ys, pos.reshape(-1)), rf[:2].T


def kernel(x, w_in, w_out, conv_w, conv_b, conv_ln_g, conv_ln_b, ret_gn_g, ln1_g, ln1_b, ln2_g,
           ln2_b, dense_w1, dense_w3, dense_w2, moe_router, moe_w1, moe_w3, moe_w2):
    b, s, d = x.shape
    depth = w_in.shape[0]
    alpha = (2.0 * depth) ** 0.25
    tables = _retention_tables(s)
    ng = BATCH_GROUPS if b % BATCH_GROUPS == 0 else 1
    bg = b // ng
    groups = [(x, g * bg) for g in range(ng)]
    for l in range(depth):
        i = l // 2
        mix_w = (w_in[l].astype(_BF), w_out[l].astype(_BF))
        if l % 2 == 0:
            ffn_w = (dense_w1[i].astype(_BF), dense_w3[i].astype(_BF), dense_w2[i].astype(_BF))
        else:
            ffn_w = (moe_w1[i].astype(_BF), moe_w3[i].astype(_BF), moe_w2[i].astype(_BF))
        parts = []
        for g, (xg, batch0) in enumerate(groups):
            h = _mixer(xg, batch0, bg, *mix_w, conv_w[l], conv_b[l], conv_ln_g[l], conv_ln_b[l],
                       ret_gn_g[l], ln1_g[l], ln1_b[l], tables, alpha).reshape(bg * s, d)
            if l % 2 == 0:
                h = _dense_ffn(h, *ffn_w, ln2_g[l], ln2_b[l], alpha)
            else:
                parts.append(_moe_experts(h, moe_router[i], *ffn_w))
                if l == depth - 1:
                    continue
                h = _combine(parts[-1:], ln2_g[l], ln2_b[l], alpha)
            groups[g] = (h.reshape(bg, s, d), 0)
    if depth % 2 == 0:
        return _combine(parts, ln2_g[-1], ln2_b[-1], alpha).reshape(b, s, d)
    return jnp.concatenate([xg for xg, _ in groups], axis=0)
```

```python
import functools

import jax
import jax.numpy as jnp
from jax import lax
from jax.experimental import pallas as pl
from jax.experimental.pallas import tpu as pltpu
from jax.experimental.pallas import tpu_sc as plsc

CHUNK = 64
CONV_CH = 512
CONV_K = 31
RET_HEADS = 4
RET_DK = 64
RET_DV = 128
RET_QK_W = RET_HEADS * RET_DK
RET_V_W = RET_HEADS * RET_DV
ROPE_BASE = 10000.0
N_EXPERTS = 8
LN_EPS = 1e-5

LANES = 128
SUBLANES = 8
SEQ_TILE = 512
RET_BLOCK = 128
CONV_ROWS = 32
HALO = 32
FFN_TILE = 512
FFN_CHUNK = 512
ROUTE_TILE = 1024
MOVE_TILE = 512
SC_CORES = 2
SC_SUBCORES = 16
SC_ROWS = 32
BATCH_GROUPS = 2
VMEM_LIMIT = 56 * 1024 * 1024

_BF = jnp.bfloat16
_F32 = jnp.float32


def _dot(a, b):
    return jnp.dot(a, b, preferred_element_type=_F32)


def _layer_norm(v, g, b):
    mu = jnp.mean(v, axis=-1, keepdims=True)
    d = v - mu
    var = jnp.mean(d * d, axis=-1, keepdims=True)
    return d * lax.rsqrt(var + LN_EPS) * g + b


def _sigmoid(v):
    return 1.0 / (1.0 + jnp.exp(-v))


def _const_spec(shape):
    nd = len(shape)
    return pl.BlockSpec(shape, lambda *_: (0,) * nd, pipeline_mode=pl.Buffered(1))


def _layer_spec(stacked, layer):
    rest = stacked.shape[1:]
    return pl.BlockSpec((None,) + rest, lambda *_: (layer,) + (0,) * len(rest),
                        pipeline_mode=pl.Buffered(1))


def _mixer_kernel(alpha, x_ref, w_in_ref, w_out_ref, conv_w_ref, conv_b_ref, cg_ref, cb_ref,
                  gn_ref, lg_ref, lb_ref, cos_ref, sin_ref, dmask_ref, qdec_ref, kdec_ref,
                  sdec_ref, bd_ref, o_ref, ush, state, rbuf):
    t = x_ref.shape[0]
    ubuf = ush.at[0]

    @pl.when(pl.program_id(1) == 0)
    def _():
        ubuf[0:HALO, :] = jnp.zeros((HALO, CONV_CH), _F32)
        state[...] = jnp.zeros_like(state)

    x = x_ref[...]
    xb = x.astype(_BF)

    c0 = 2 * CONV_CH
    ab = _dot(xb, w_in_ref[:, 0:c0])
    qk = _dot(xb, w_in_ref[:, c0:c0 + 2 * RET_QK_W])
    v = _dot(xb, w_in_ref[:, c0 + 2 * RET_QK_W:c0 + 2 * RET_QK_W + RET_V_W])
    g = _dot(xb, w_in_ref[:, c0 + 2 * RET_QK_W + RET_V_W:])

    ubuf[HALO:HALO + t, :] = ab[:, :CONV_CH] * _sigmoid(ab[:, CONV_CH:])
    span = t + HALO - SUBLANES
    for r in range(1, SUBLANES):
        ush[r, 0:span, :] = ubuf[r:r + span, :]
    off = HALO - (CONV_K - 1)
    conv_b = conv_b_ref[...]
    blocks = []
    for r0 in range(0, t, CONV_ROWS):
        acc = jnp.broadcast_to(conv_b, (CONV_ROWS, CONV_CH))
        for r in range(SUBLANES):
            taps = [j for j in range(CONV_K) if (j + off) % SUBLANES == r]
            reach = max((j + off) // SUBLANES for j in taps) * SUBLANES
            seg = ush[r, r0:r0 + reach + CONV_ROWS, :]
            for j in taps:
                a = (j + off) // SUBLANES * SUBLANES
                acc = acc + conv_w_ref[j:j + 1, :] * seg[a:a + CONV_ROWS]
        blocks.append(acc)
    conv = jnp.concatenate(blocks, axis=0)
    ubuf[0:HALO, :] = ubuf[t:t + HALO, :]
    un = _layer_norm(conv, cg_ref[...], cb_ref[...])
    u_out = un * _sigmoid(un)

    cos_t = cos_ref[...]
    sin_t = sin_ref[...]
    lane = lax.broadcasted_iota(jnp.int32, (1, LANES), 1)
    first_half = (lane % RET_DK) < (RET_DK // 2)

    def rope(z):
        parts = []
        for c in range(0, RET_QK_W, LANES):
            zc = z[:, c:c + LANES]
            up = pltpu.roll(zc, LANES - RET_DK // 2, axis=1)
            dn = pltpu.roll(zc, RET_DK // 2, axis=1)
            parts.append(jnp.where(first_half, up, dn))
        return z * cos_t + jnp.concatenate(parts, axis=1) * sin_t

    q = rope(qk[:, :RET_QK_W])
    k = rope(qk[:, RET_QK_W:])

    lane_qk = lax.broadcasted_iota(jnp.int32, (1, RET_QK_W), 1)
    dmask = dmask_ref[...]
    qdec = qdec_ref[...]
    kdec = kdec_ref[...]
    nr = RET_BLOCK
    for s in range(t // nr):
        rows = slice(s * nr, (s + 1) * nr)
        q_s, k_s, v_s = q[rows], k[rows], v[rows]
        v_b = v_s.astype(_BF)
        kbd = jnp.concatenate(
            [jnp.where(lane_qk // RET_DK == h, k_s, 0.0).astype(_BF) for h in range(RET_HEADS)],
            axis=0)
        sc = lax.dot_general(q_s.astype(_BF), kbd, (((1,), (1,)), ((), ())),
                             preferred_element_type=_F32)
        p = (sc * dmask).astype(_BF)
        intra = jnp.concatenate(
            [_dot(p[:, h * nr:(h + 1) * nr], v_b[:, h * RET_DV:(h + 1) * RET_DV])
             for h in range(RET_HEADS)], axis=1)
        st = state[...]
        cross = _dot((q_s * qdec).astype(_BF), st.astype(_BF))
        kv = lax.dot_general((k_s * kdec).astype(_BF), v_b, (((0,), (0,)), ((), ())),
                             preferred_element_type=_F32)
        state[...] = st * sdec_ref[...] + kv * bd_ref[...]
        rbuf[rows, :] = intra + cross

    r = rbuf[...]
    gate = g * _sigmoid(g)
    gn = gn_ref[...]
    r_parts = []
    for h in range(RET_HEADS):
        hs = slice(h * RET_DV, (h + 1) * RET_DV)
        rh = r[:, hs]
        mu = jnp.mean(rh, axis=-1, keepdims=True)
        d = rh - mu
        var = jnp.mean(d * d, axis=-1, keepdims=True)
        r_parts.append(gate[:, hs] * (d * lax.rsqrt(var + LN_EPS) * gn[:, hs]))

    mixed = jnp.concatenate([u_out] + r_parts, axis=1).astype(_BF)
    mix = _dot(mixed, w_out_ref[...])
    o_ref[...] = _layer_norm(alpha * x + mix, lg_ref[...], lb_ref[...])


def _retention_tables(seq):
    half = RET_DK // 2
    inv_freq = ROPE_BASE ** (-jnp.arange(half, dtype=_F32) / half)
    ang = jnp.arange(seq, dtype=jnp.int32).astype(_F32)[:, None] * inv_freq[None, :]
    cos, sin = jnp.cos(ang), jnp.sin(ang)
    cos_t = jnp.tile(jnp.concatenate([cos, cos], axis=1), (1, RET_HEADS))
    sin_t = jnp.tile(jnp.concatenate([-sin, sin], axis=1), (1, RET_HEADS))

    nr = RET_BLOCK
    log_gamma = jnp.log1p(-(2.0 ** (-5.0 - jnp.arange(RET_HEADS, dtype=_F32))))
    idx = jnp.arange(nr, dtype=_F32)
    dist = jnp.abs(idx[:, None] - idx[None, :])
    visible = (jnp.arange(nr)[None, :] // CHUNK) <= (jnp.arange(nr)[:, None] // CHUNK)
    scale = RET_DK ** -0.5
    dm = jnp.exp(log_gamma[:, None, None] * dist[None]) * visible[None] * scale
    dmask = jnp.transpose(dm, (1, 0, 2)).reshape(nr, RET_HEADS * nr)
    qdec = jnp.repeat(jnp.exp(log_gamma[None, :] * (idx + 1.0)[:, None]), RET_DK, axis=1)
    kdec = jnp.repeat(jnp.exp(log_gamma[None, :] * (nr - 1.0 - idx)[:, None]), RET_DK, axis=1) * scale
    sdec = jnp.broadcast_to(jnp.repeat(jnp.exp(log_gamma * nr), RET_DK)[:, None],
                            (RET_QK_W, RET_V_W))
    bd = (jnp.arange(RET_QK_W)[:, None] // RET_DK == jnp.arange(RET_V_W)[None, :] // RET_DV)
    return cos_t, sin_t, dmask, qdec, kdec, sdec, bd.astype(_F32)


def _mixer(x, batch0, nb, layer, w_in, w_out, conv_w, conv_b, cg, cb, gn, lg, lb, tables, alpha):
    _, s, d = x.shape
    t = min(SEQ_TILE, s)
    assert s % t == 0 and t % RET_BLOCK == 0 and RET_BLOCK % CHUNK == 0 and t % CONV_ROWS == 0
    cos_t, sin_t, dmask, qdec, kdec, sdec, bd = tables
    row = lambda a: a.reshape(1, -1)
    conv_w = jnp.pad(conv_w, ((0, HALO - CONV_K), (0, 0)))
    consts = [conv_w, row(conv_b), row(cg), row(cb), row(gn), row(lg), row(lb)]
    tail = [dmask, qdec, kdec, sdec, bd]
    in_specs = ([pl.BlockSpec((None, t, d), lambda i, j: (i + batch0, j, 0)),
                 _layer_spec(w_in, layer), _layer_spec(w_out, layer)]
                + [_const_spec(a.shape) for a in consts]
                + [pl.BlockSpec((t, RET_QK_W), lambda i, j: (j, 0))] * 2
                + [_const_spec(a.shape) for a in tail])
    return pl.pallas_call(
        functools.partial(_mixer_kernel, alpha),
        grid=(nb, s // t),
        in_specs=in_specs,
        out_specs=pl.BlockSpec((None, t, d), lambda i, j: (i, j, 0)),
        out_shape=jax.ShapeDtypeStruct((nb, s, d), _F32),
        scratch_shapes=[pltpu.VMEM((SUBLANES, t + HALO, CONV_CH), _F32),
                        pltpu.VMEM((RET_QK_W, RET_V_W), _F32),
                        pltpu.VMEM((t, RET_V_W), _F32)],
        compiler_params=pltpu.CompilerParams(
            dimension_semantics=("arbitrary", "arbitrary"), vmem_limit_bytes=VMEM_LIMIT),
        name="mixer",
    )(x, w_in, w_out, *consts, cos_t, sin_t, *tail)


def _swiglu_tile(xb, w1_ref, w3_ref, w2_ref):
    d_ff = w1_ref.shape[1]
    acc = None
    for c in range(0, d_ff, FFN_CHUNK):
        a = _dot(xb, w1_ref[:, c:c + FFN_CHUNK])
        g = _dot(xb, w3_ref[:, c:c + FFN_CHUNK])
        h = (a * _sigmoid(a) * g).astype(_BF)
        part = _dot(h, w2_ref[c:c + FFN_CHUNK, :])
        acc = part if acc is None else acc + part
    return acc


def _dense_ffn_kernel(alpha, x_ref, w1_ref, w3_ref, w2_ref, lg_ref, lb_ref, o_ref):
    x = x_ref[...]
    f = _swiglu_tile(x.astype(_BF), w1_ref, w3_ref, w2_ref)
    o_ref[...] = _layer_norm(alpha * x + f, lg_ref[...], lb_ref[...])


def _dense_ffn(x, layer, w1, w3, w2, lg, lb, alpha):
    n, d = x.shape
    tm = min(FFN_TILE, n)
    assert n % tm == 0 and w1.shape[2] % FFN_CHUNK == 0
    weights = [w1, w3, w2]
    norm = [lg.reshape(1, -1), lb.reshape(1, -1)]
    return pl.pallas_call(
        functools.partial(_dense_ffn_kernel, alpha),
        grid=(n // tm,),
        in_specs=([pl.BlockSpec((tm, d), lambda i: (i, 0))]
                  + [_layer_spec(w, layer) for w in weights]
                  + [_const_spec(a.shape) for a in norm]),
        out_specs=pl.BlockSpec((tm, d), lambda i: (i, 0)),
        out_shape=jax.ShapeDtypeStruct((n, d), _F32),
        compiler_params=pltpu.CompilerParams(
            dimension_semantics=("arbitrary",), vmem_limit_bytes=VMEM_LIMIT),
        name="dense_ffn",
    )(x, *weights, *norm)


def _router_kernel(x_ref, wr_ref, tri_ref, ri_ref, rf_ref, cnt_ref, carry):
    @pl.when(pl.program_id(0) == 0)
    def _():
        carry[...] = jnp.zeros_like(carry)

    tr = x_ref.shape[0]
    logits = lax.dot_general(wr_ref[...], x_ref[...].astype(_BF), (((1,), (1,)), ((), ())),
                             preferred_element_type=_F32)
    eidx = lax.broadcasted_iota(jnp.int32, (N_EXPERTS, tr), 0)
    m0 = jnp.max(logits, axis=0, keepdims=True)
    e0 = jnp.min(jnp.where(logits == m0, eidx, N_EXPERTS), axis=0, keepdims=True)
    rest = jnp.where(eidx == e0, -jnp.inf, logits)
    m1 = jnp.max(rest, axis=0, keepdims=True)
    e1 = jnp.min(jnp.where(rest == m1, eidx, N_EXPERTS), axis=0, keepdims=True)
    tt = jnp.exp(m1 - m0)
    g0 = 1.0 / (1.0 + tt)
    g1 = tt / (1.0 + tt)

    oh0 = eidx == e0
    oh1 = eidx == e1
    member = jnp.where(oh0 | oh1, 1.0, 0.0)
    before = _dot(member.astype(_BF), tri_ref[...]) + carry[:, 0:1]
    rank0 = jnp.sum(jnp.where(oh0, before, 0.0), axis=0, keepdims=True)
    rank1 = jnp.sum(jnp.where(oh1, before, 0.0), axis=0, keepdims=True)
    carry[...] = carry[...] + jnp.sum(member, axis=1, keepdims=True)

    zi = jnp.zeros((N_EXPERTS - 4, tr), jnp.int32)
    ri_ref[...] = jnp.concatenate(
        [e0, e1, rank0.astype(jnp.int32), rank1.astype(jnp.int32), zi], axis=0)
    rf_ref[...] = jnp.concatenate([g0, g1, jnp.zeros((N_EXPERTS - 2, tr), _F32)], axis=0)
    cnt_ref[...] = carry[...]


def _router(x, wr_t):
    n, d = x.shape
    tr = min(ROUTE_TILE, n)
    assert n % tr == 0
    tri = (jnp.arange(tr)[:, None] < jnp.arange(tr)[None, :]).astype(_BF)
    return pl.pallas_call(
        _router_kernel,
        grid=(n // tr,),
        in_specs=[pl.BlockSpec((tr, d), lambda i: (i, 0)),
                  _const_spec(wr_t.shape), _const_spec(tri.shape)],
        out_specs=[pl.BlockSpec((N_EXPERTS, tr), lambda i: (0, i)),
                   pl.BlockSpec((N_EXPERTS, tr), lambda i: (0, i)),
                   pl.BlockSpec((N_EXPERTS, LANES), lambda i: (0, 0))],
        out_shape=[jax.ShapeDtypeStruct((N_EXPERTS, n), jnp.int32),
                   jax.ShapeDtypeStruct((N_EXPERTS, n), _F32),
                   jax.ShapeDtypeStruct((N_EXPERTS, LANES), _F32)],
        scratch_shapes=[pltpu.VMEM((N_EXPERTS, LANES), _F32)],
        compiler_params=pltpu.CompilerParams(
            dimension_semantics=("arbitrary",), vmem_limit_bytes=VMEM_LIMIT),
        name="router",
    )(x, wr_t, tri)


def _move_tile(n):
    tm = min(MOVE_TILE, n)
    assert n % tm == 0 and tm % SUBLANES == 0
    return tm


def _expert_kernel(te_ref, na_ref, xs_ref, w1_ref, w3_ref, w2_ref, ys_ref):
    del te_ref
    active = pl.program_id(0) < na_ref[0]

    @pl.when(active)
    def _():
        ys_ref[...] = _swiglu_tile(xs_ref[...].astype(_BF), w1_ref, w3_ref, w2_ref)

    @pl.when(jnp.logical_not(active))
    def _():
        ys_ref[...] = jnp.zeros_like(ys_ref)


def _expert_ffn(xs, layer, w1, w3, w2, tile_expert, n_active):
    d = xs.shape[1]
    tm = FFN_TILE
    d_ff = w1.shape[3]
    n_tiles = tile_expert.shape[0]
    n_slots = n_tiles * tm
    w_in_spec = pl.BlockSpec((None, None, d, d_ff), lambda i, te, na: (layer, te[i], 0, 0),
                             pipeline_mode=pl.Buffered(1))
    w_out_spec = pl.BlockSpec((None, None, d_ff, d), lambda i, te, na: (layer, te[i], 0, 0),
                              pipeline_mode=pl.Buffered(1))
    return pl.pallas_call(
        _expert_kernel,
        grid_spec=pltpu.PrefetchScalarGridSpec(
            num_scalar_prefetch=2,
            grid=(n_tiles,),
            in_specs=[pl.BlockSpec((tm, d), lambda i, te, na: (jnp.minimum(i, na[0] - 1), 0)),
                      w_in_spec, w_in_spec, w_out_spec],
            out_specs=pl.BlockSpec((tm, d), lambda i, te, na: (i, 0)),
        ),
        out_shape=jax.ShapeDtypeStruct((n_slots, d), _F32),
        compiler_params=pltpu.CompilerParams(
            dimension_semantics=("arbitrary",), vmem_limit_bytes=VMEM_LIMIT),
        name="expert_ffn",
    )(tile_expert, n_active, xs, w1, w3, w2)


def _sc_gather(table, idx):
    m, d = idx.shape[0], table.shape[1]
    workers = SC_CORES * SC_SUBCORES
    per_worker = m // workers
    assert m % (workers * SC_ROWS) == 0

    def body(table_hbm, idx_hbm, out_hbm, idx_v, rows_v, sem):
        base = (lax.axis_index("s") * SC_CORES + lax.axis_index("c")) * per_worker

        @pl.loop(0, per_worker // SC_ROWS)
        def _(c):
            off = pl.multiple_of(base + c * SC_ROWS, SC_ROWS)
            pltpu.sync_copy(idx_hbm.at[pl.ds(off, SC_ROWS)], idx_v)
            pltpu.async_copy(table_hbm.at[idx_v], rows_v, sem).wait()
            pltpu.sync_copy(rows_v, out_hbm.at[pl.ds(off, SC_ROWS)])

    return pl.kernel(
        body, out_type=jax.ShapeDtypeStruct((m, d), table.dtype), mesh=_sc_mesh(),
        scratch_types=[pltpu.VMEM((SC_ROWS,), jnp.int32), pltpu.VMEM((SC_ROWS, d), table.dtype),
                       pltpu.SemaphoreType.DMA],
        name="sc_gather")(table, idx)


def _sc_mesh():
    return plsc.VectorSubcoreMesh(core_axis_name="c", subcore_axis_name="s",
                                  num_cores=SC_CORES, num_subcores=SC_SUBCORES)


def _sc_scatter(x, pos, pad_slots, n_slots):
    n, d = x.shape
    workers = SC_CORES * SC_SUBCORES
    per_worker = n // workers
    pad_per_worker = pad_slots.shape[0] // workers
    assert n % (workers * SC_ROWS) == 0 and pad_slots.shape[0] % (workers * SC_ROWS) == 0
    assert pos.shape[0] == 2 * n and n_slots == 2 * n + pad_slots.shape[0]

    def body(x_hbm, pos_hbm, pad_hbm, out_hbm, idx_v, rows_v):
        wid = lax.axis_index("s") * SC_CORES + lax.axis_index("c")

        @pl.loop(0, per_worker // SC_ROWS)
        def _(c):
            off = pl.multiple_of(wid * per_worker + c * SC_ROWS, SC_ROWS)
            pltpu.sync_copy(x_hbm.at[pl.ds(off, SC_ROWS)], rows_v)
            for choice in range(2):
                pltpu.sync_copy(pos_hbm.at[pl.ds(choice * n + off, SC_ROWS)], idx_v)
                pltpu.sync_copy(rows_v, out_hbm.at[idx_v])

        @pl.loop(0, pad_per_worker // SC_ROWS)
        def _(c):
            off = pl.multiple_of(wid * pad_per_worker + c * SC_ROWS, SC_ROWS)
            pltpu.sync_copy(pad_hbm.at[pl.ds(off, SC_ROWS)], idx_v)
            pltpu.sync_copy(rows_v, out_hbm.at[idx_v])

    return pl.kernel(
        body, out_type=jax.ShapeDtypeStruct((n_slots, d), x.dtype), mesh=_sc_mesh(),
        scratch_types=[pltpu.VMEM((SC_ROWS,), jnp.int32), pltpu.VMEM((SC_ROWS, d), x.dtype)],
        name="sc_scatter")(x, pos, pad_slots)


def _combine_kernel(alpha, n_groups, *refs):
    lg_ref, lb_ref, o_ref = refs[-3:]
    steps = pl.num_programs(0) // n_groups
    i = pl.program_id(0)
    for g in range(n_groups):
        x_ref, y0_ref, y1_ref, gate_ref = refs[4 * g:4 * g + 4]

        @pl.when((i >= g * steps) & (i < (g + 1) * steps))
        def _(x_ref=x_ref, y0_ref=y0_ref, y1_ref=y1_ref, gate_ref=gate_ref):
            gate = gate_ref[...]
            f = y0_ref[...] * gate[:, 0:1] + y1_ref[...] * gate[:, 1:2]
            o_ref[...] = _layer_norm(alpha * x_ref[...] + f, lg_ref[...], lb_ref[...])


def _combine(parts, lg, lb, alpha):
    n, d = parts[0][0].shape
    tm = _move_tile(n)
    steps = n // tm
    in_specs, args = [], []
    for g, (x, y01, gates) in enumerate(parts):
        row = lambda i, g=g: jnp.clip(i - g * steps, 0, steps - 1)
        in_specs += [pl.BlockSpec((tm, d), lambda i, row=row: (row(i), 0)),
                     pl.BlockSpec((tm, d), lambda i, row=row: (row(i), 0)),
                     pl.BlockSpec((tm, d), lambda i, row=row: (row(i) + steps, 0)),
                     pl.BlockSpec((tm, 2), lambda i, row=row: (row(i), 0))]
        args += [x, y01, y01, gates]
    return pl.pallas_call(
        functools.partial(_combine_kernel, alpha, len(parts)),
        grid=(len(parts) * steps,),
        in_specs=in_specs + [_const_spec((1, d)), _const_spec((1, d))],
        out_specs=pl.BlockSpec((tm, d), lambda i: (i, 0)),
        out_shape=jax.ShapeDtypeStruct((len(parts) * n, d), _F32),
        compiler_params=pltpu.CompilerParams(
            dimension_semantics=("arbitrary",), vmem_limit_bytes=VMEM_LIMIT),
        name="combine",
    )(*args, lg.reshape(1, -1), lb.reshape(1, -1))


def _moe_experts(x, router, layer, w1, w3, w2):
    n, d = x.shape
    tm = FFN_TILE
    n_tiles = -(-(2 * n) // tm) + N_EXPERTS
    n_slots = n_tiles * tm

    ri, rf, cnt = _router(x, router.T.astype(_BF))
    counts = cnt[:, 0].astype(jnp.int32)
    padded = (counts + tm - 1) // tm * tm
    group_end = jnp.cumsum(padded)
    group_start = group_end - padded
    experts = jnp.arange(N_EXPERTS, dtype=jnp.int32)[:, None]

    def slot(e, rank):
        return jnp.sum(jnp.where(e[None, :] == experts, group_start[:, None], 0), axis=0) + rank

    pos = jnp.stack([slot(ri[0], ri[2]), slot(ri[1], ri[3])])
    tile_start = jnp.arange(n_tiles, dtype=jnp.int32) * tm
    tile_expert = jnp.minimum(
        jnp.sum(tile_start[:, None] >= group_end[None, :], axis=1), N_EXPERTS - 1).astype(jnp.int32)
    n_active = (group_end[-1:] // tm).astype(jnp.int32)

    pad_len = jnp.concatenate([padded - counts, n_slots - group_end[-1:]])
    pad_begin = jnp.concatenate([group_start + counts, group_end[-1:]])
    pad_end = jnp.cumsum(pad_len)
    k = jnp.arange(n_slots - 2 * n, dtype=jnp.int32)
    run = jnp.sum(k[:, None] >= pad_end[None, :], axis=1)
    pad_slots = k + jnp.sum(
        jnp.where(run[:, None] == jnp.arange(N_EXPERTS + 1)[None, :],
                  (pad_begin - (pad_end - pad_len))[None, :], 0), axis=1)
    xs = _sc_scatter(x, pos.reshape(-1), pad_slots.astype(jnp.int32), n_slots)
    ys = _expert_ffn(xs, layer, w1, w3, w2, tile_expert, n_active)
    return x, _sc_gather(ys, pos.reshape(-1)), rf[:2].T


def kernel(x, w_in, w_out, conv_w, conv_b, conv_ln_g, conv_ln_b, ret_gn_g, ln1_g, ln1_b, ln2_g,
           ln2_b, dense_w1, dense_w3, dense_w2, moe_router, moe_w1, moe_w3, moe_w2):
    b, s, d = x.shape
    depth = w_in.shape[0]
    alpha = (2.0 * depth) ** 0.25
    tables = _retention_tables(s)
    ng = BATCH_GROUPS if b % BATCH_GROUPS == 0 else 1
    bg = b // ng
    groups = [(x, g * bg) for g in range(ng)]
    mix_w = (w_in.astype(_BF), w_out.astype(_BF))
    dense_w = (dense_w1.astype(_BF), dense_w3.astype(_BF), dense_w2.astype(_BF))
    moe_w = (moe_w1.astype(_BF), moe_w3.astype(_BF), moe_w2.astype(_BF))
    for l in range(depth):
        i = l // 2
        parts = []
        for g, (xg, batch0) in enumerate(groups):
            h = _mixer(xg, batch0, bg, l, *mix_w, conv_w[l], conv_b[l], conv_ln_g[l], conv_ln_b[l],
                       ret_gn_g[l], ln1_g[l], ln1_b[l], tables, alpha).reshape(bg * s, d)
            if l % 2 == 0:
                h = _dense_ffn(h, i, *dense_w, ln2_g[l], ln2_b[l], alpha)
            else:
                parts.append(_moe_experts(h, moe_router[i], i, *moe_w))
                if l == depth - 1:
                    continue
                h = _combine(parts[-1:], ln2_g[l], ln2_b[l], alpha)
            groups[g] = (h.reshape(bg, s, d), 0)
    if depth % 2 == 0:
        return _combine(parts, ln2_g[-1], ln2_b[-1], alpha).reshape(b, s, d)
    return jnp.concatenate([xg for xg, _ in groups], axis=0)
```

```python
import functools

import jax
import jax.numpy as jnp
from jax import lax
from jax.experimental import pallas as pl
from jax.experimental.pallas import tpu as pltpu
from jax.experimental.pallas import tpu_sc as plsc

CHUNK = 64
CONV_CH = 512
CONV_K = 31
RET_HEADS = 4
RET_DK = 64
RET_DV = 128
RET_QK_W = RET_HEADS * RET_DK
RET_V_W = RET_HEADS * RET_DV
ROPE_BASE = 10000.0
N_EXPERTS = 8
LN_EPS = 1e-5

LANES = 128
SUBLANES = 8
SEQ_TILE = 512
RET_BLOCK = 128
CONV_ROWS = 32
HALO = 32
FFN_TILE = 512
FFN_CHUNK = 512
ROUTE_TILE = 1024
MOVE_TILE = 512
SC_CORES = 2
SC_SUBCORES = 16
SC_ROWS = 32
BATCH_GROUPS = 2
VMEM_LIMIT = 56 * 1024 * 1024

_BF = jnp.bfloat16
_F32 = jnp.float32


def _dot(a, b):
    return jnp.dot(a, b, preferred_element_type=_F32)


def _layer_norm(v, g, b):
    mu = jnp.mean(v, axis=-1, keepdims=True)
    d = v - mu
    var = jnp.mean(d * d, axis=-1, keepdims=True)
    return d * lax.rsqrt(var + LN_EPS) * g + b


def _sigmoid(v):
    return 1.0 / (1.0 + jnp.exp(-v))


def _const_spec(shape):
    nd = len(shape)
    return pl.BlockSpec(shape, lambda *_: (0,) * nd, pipeline_mode=pl.Buffered(1))


def _layer_spec(stacked, layer):
    rest = stacked.shape[1:]
    return pl.BlockSpec((None,) + rest, lambda *_: (layer,) + (0,) * len(rest),
                        pipeline_mode=pl.Buffered(1))


def _mixer_kernel(alpha, x_ref, w_in_ref, w_out_ref, conv_w_ref, conv_b_ref, cg_ref, cb_ref,
                  gn_ref, lg_ref, lb_ref, cos_ref, sin_ref, dmask_ref, qdec_ref, kdec_ref,
                  sdec_ref, bd_ref, o_ref, ush, state, rbuf):
    t = x_ref.shape[0]
    ubuf = ush.at[0]

    @pl.when(pl.program_id(1) == 0)
    def _():
        ubuf[0:HALO, :] = jnp.zeros((HALO, CONV_CH), _F32)
        state[...] = jnp.zeros_like(state)

    x = x_ref[...]
    xb = x.astype(_BF)

    c0 = 2 * CONV_CH
    ab = _dot(xb, w_in_ref[:, 0:c0])
    qk = _dot(xb, w_in_ref[:, c0:c0 + 2 * RET_QK_W])
    v = _dot(xb, w_in_ref[:, c0 + 2 * RET_QK_W:c0 + 2 * RET_QK_W + RET_V_W])
    g = _dot(xb, w_in_ref[:, c0 + 2 * RET_QK_W + RET_V_W:])

    ubuf[HALO:HALO + t, :] = ab[:, :CONV_CH] * _sigmoid(ab[:, CONV_CH:])
    span = t + HALO - SUBLANES
    for r in range(1, SUBLANES):
        ush[r, 0:span, :] = ubuf[r:r + span, :]
    off = HALO - (CONV_K - 1)
    conv_b = conv_b_ref[...]
    blocks = []
    for r0 in range(0, t, CONV_ROWS):
        acc = jnp.broadcast_to(conv_b, (CONV_ROWS, CONV_CH))
        for r in range(SUBLANES):
            taps = [j for j in range(CONV_K) if (j + off) % SUBLANES == r]
            reach = max((j + off) // SUBLANES for j in taps) * SUBLANES
            seg = ush[r, r0:r0 + reach + CONV_ROWS, :]
            for j in taps:
                a = (j + off) // SUBLANES * SUBLANES
                acc = acc + conv_w_ref[j:j + 1, :] * seg[a:a + CONV_ROWS]
        blocks.append(acc)
    conv = jnp.concatenate(blocks, axis=0)
    ubuf[0:HALO, :] = ubuf[t:t + HALO, :]
    un = _layer_norm(conv, cg_ref[...], cb_ref[...])
    u_out = un * _sigmoid(un)

    cos_t = cos_ref[...]
    sin_t = sin_ref[...]
    lane = lax.broadcasted_iota(jnp.int32, (1, LANES), 1)
    first_half = (lane % RET_DK) < (RET_DK // 2)

    def rope(z):
        parts = []
        for c in range(0, RET_QK_W, LANES):
            zc = z[:, c:c + LANES]
            up = pltpu.roll(zc, LANES - RET_DK // 2, axis=1)
            dn = pltpu.roll(zc, RET_DK // 2, axis=1)
            parts.append(jnp.where(first_half, up, dn))
        return z * cos_t + jnp.concatenate(parts, axis=1) * sin_t

    q = rope(qk[:, :RET_QK_W])
    k = rope(qk[:, RET_QK_W:])

    lane_qk = lax.broadcasted_iota(jnp.int32, (1, RET_QK_W), 1)
    dmask = dmask_ref[...]
    qdec = qdec_ref[...]
    kdec = kdec_ref[...]
    nr = RET_BLOCK
    for s in range(t // nr):
        rows = slice(s * nr, (s + 1) * nr)
        q_s, k_s, v_s = q[rows], k[rows], v[rows]
        v_b = v_s.astype(_BF)
        kbd = jnp.concatenate(
            [jnp.where(lane_qk // RET_DK == h, k_s, 0.0).astype(_BF) for h in range(RET_HEADS)],
            axis=0)
        sc = lax.dot_general(q_s.astype(_BF), kbd, (((1,), (1,)), ((), ())),
                             preferred_element_type=_F32)
        p = (sc * dmask).astype(_BF)
        intra = jnp.concatenate(
            [_dot(p[:, h * nr:(h + 1) * nr], v_b[:, h * RET_DV:(h + 1) * RET_DV])
             for h in range(RET_HEADS)], axis=1)
        st = state[...]
        cross = _dot((q_s * qdec).astype(_BF), st.astype(_BF))
        kv = lax.dot_general((k_s * kdec).astype(_BF), v_b, (((0,), (0,)), ((), ())),
                             preferred_element_type=_F32)
        state[...] = st * sdec_ref[...] + kv * bd_ref[...]
        rbuf[rows, :] = intra + cross

    r = rbuf[...]
    gate = g * _sigmoid(g)
    gn = gn_ref[...]
    r_parts = []
    for h in range(RET_HEADS):
        hs = slice(h * RET_DV, (h + 1) * RET_DV)
        rh = r[:, hs]
        mu = jnp.mean(rh, axis=-1, keepdims=True)
        d = rh - mu
        var = jnp.mean(d * d, axis=-1, keepdims=True)
        r_parts.append(gate[:, hs] * (d * lax.rsqrt(var + LN_EPS) * gn[:, hs]))

    mixed = jnp.concatenate([u_out] + r_parts, axis=1).astype(_BF)
    mix = _dot(mixed, w_out_ref[...])
    o_ref[...] = _layer_norm(alpha * x + mix, lg_ref[...], lb_ref[...])


def _retention_tables(seq):
    half = RET_DK // 2
    inv_freq = ROPE_BASE ** (-jnp.arange(half, dtype=_F32) / half)
    ang = jnp.arange(seq, dtype=jnp.int32).astype(_F32)[:, None] * inv_freq[None, :]
    cos, sin = jnp.cos(ang), jnp.sin(ang)
    cos_t = jnp.tile(jnp.concatenate([cos, cos], axis=1), (1, RET_HEADS))
    sin_t = jnp.tile(jnp.concatenate([-sin, sin], axis=1), (1, RET_HEADS))

    nr = RET_BLOCK
    log_gamma = jnp.log1p(-(2.0 ** (-5.0 - jnp.arange(RET_HEADS, dtype=_F32))))
    idx = jnp.arange(nr, dtype=_F32)
    dist = jnp.abs(idx[:, None] - idx[None, :])
    visible = (jnp.arange(nr)[None, :] // CHUNK) <= (jnp.arange(nr)[:, None] // CHUNK)
    scale = RET_DK ** -0.5
    dm = jnp.exp(log_gamma[:, None, None] * dist[None]) * visible[None] * scale
    dmask = jnp.transpose(dm, (1, 0, 2)).reshape(nr, RET_HEADS * nr)
    qdec = jnp.repeat(jnp.exp(log_gamma[None, :] * (idx + 1.0)[:, None]), RET_DK, axis=1)
    kdec = jnp.repeat(jnp.exp(log_gamma[None, :] * (nr - 1.0 - idx)[:, None]), RET_DK, axis=1) * scale
    sdec = jnp.broadcast_to(jnp.repeat(jnp.exp(log_gamma * nr), RET_DK)[:, None],
                            (RET_QK_W, RET_V_W))
    bd = (jnp.arange(RET_QK_W)[:, None] // RET_DK == jnp.arange(RET_V_W)[None, :] // RET_DV)
    return cos_t, sin_t, dmask, qdec, kdec, sdec, bd.astype(_F32)


def _mixer(x, batch0, nb, layer, w_in, w_out, conv_w, conv_b, cg, cb, gn, lg, lb, tables, alpha):
    _, s, d = x.shape
    t = min(SEQ_TILE, s)
    assert s % t == 0 and t % RET_BLOCK == 0 and RET_BLOCK % CHUNK == 0 and t % CONV_ROWS == 0
    cos_t, sin_t, dmask, qdec, kdec, sdec, bd = tables
    row = lambda a: a.reshape(1, -1)
    conv_w = jnp.pad(conv_w, ((0, HALO - CONV_K), (0, 0)))
    consts = [conv_w, row(conv_b), row(cg), row(cb), row(gn), row(lg), row(lb)]
    tail = [dmask, qdec, kdec, sdec, bd]
    in_specs = ([pl.BlockSpec((None, t, d), lambda i, j: (i + batch0, j, 0)),
                 _layer_spec(w_in, layer), _layer_spec(w_out, layer)]
                + [_const_spec(a.shape) for a in consts]
                + [pl.BlockSpec((t, RET_QK_W), lambda i, j: (j, 0))] * 2
                + [_const_spec(a.shape) for a in tail])
    return pl.pallas_call(
        functools.partial(_mixer_kernel, alpha),
        grid=(nb, s // t),
        in_specs=in_specs,
        out_specs=pl.BlockSpec((None, t, d), lambda i, j: (i, j, 0)),
        out_shape=jax.ShapeDtypeStruct((nb, s, d), _F32),
        scratch_shapes=[pltpu.VMEM((SUBLANES, t + HALO, CONV_CH), _F32),
                        pltpu.VMEM((RET_QK_W, RET_V_W), _F32),
                        pltpu.VMEM((t, RET_V_W), _F32)],
        compiler_params=pltpu.CompilerParams(
            dimension_semantics=("arbitrary", "arbitrary"), vmem_limit_bytes=VMEM_LIMIT),
        name="mixer",
    )(x, w_in, w_out, *consts, cos_t, sin_t, *tail)


def _swiglu_tile(xb, w1_ref, w3_ref, w2_ref):
    d_ff = w1_ref.shape[1]
    acc = None
    for c in range(0, d_ff, FFN_CHUNK):
        a = _dot(xb, w1_ref[:, c:c + FFN_CHUNK])
        g = _dot(xb, w3_ref[:, c:c + FFN_CHUNK])
        h = (a * _sigmoid(a) * g).astype(_BF)
        part = _dot(h, w2_ref[c:c + FFN_CHUNK, :])
        acc = part if acc is None else acc + part
    return acc


def _dense_ffn_kernel(alpha, x_ref, w1_ref, w3_ref, w2_ref, lg_ref, lb_ref, o_ref):
    x = x_ref[...]
    f = _swiglu_tile(x.astype(_BF), w1_ref, w3_ref, w2_ref)
    o_ref[...] = _layer_norm(alpha * x + f, lg_ref[...], lb_ref[...])


def _dense_ffn(x, layer, w1, w3, w2, lg, lb, alpha):
    n, d = x.shape
    tm = min(FFN_TILE, n)
    assert n % tm == 0 and w1.shape[2] % FFN_CHUNK == 0
    weights = [w1, w3, w2]
    norm = [lg.reshape(1, -1), lb.reshape(1, -1)]
    return pl.pallas_call(
        functools.partial(_dense_ffn_kernel, alpha),
        grid=(n // tm,),
        in_specs=([pl.BlockSpec((tm, d), lambda i: (i, 0))]
                  + [_layer_spec(w, layer) for w in weights]
                  + [_const_spec(a.shape) for a in norm]),
        out_specs=pl.BlockSpec((tm, d), lambda i: (i, 0)),
        out_shape=jax.ShapeDtypeStruct((n, d), _F32),
        compiler_params=pltpu.CompilerParams(
            dimension_semantics=("arbitrary",), vmem_limit_bytes=VMEM_LIMIT),
        name="dense_ffn",
    )(x, *weights, *norm)


def _router_kernel(x_ref, wr_ref, tri_ref, ri_ref, rf_ref, cnt_ref, carry):
    @pl.when(pl.program_id(0) == 0)
    def _():
        carry[...] = jnp.zeros_like(carry)

    tr = x_ref.shape[0]
    logits = lax.dot_general(wr_ref[...], x_ref[...].astype(_BF), (((1,), (1,)), ((), ())),
                             preferred_element_type=_F32)
    eidx = lax.broadcasted_iota(jnp.int32, (N_EXPERTS, tr), 0)
    m0 = jnp.max(logits, axis=0, keepdims=True)
    e0 = jnp.min(jnp.where(logits == m0, eidx, N_EXPERTS), axis=0, keepdims=True)
    rest = jnp.where(eidx == e0, -jnp.inf, logits)
    m1 = jnp.max(rest, axis=0, keepdims=True)
    e1 = jnp.min(jnp.where(rest == m1, eidx, N_EXPERTS), axis=0, keepdims=True)
    tt = jnp.exp(m1 - m0)
    g0 = 1.0 / (1.0 + tt)
    g1 = tt / (1.0 + tt)

    oh0 = eidx == e0
    oh1 = eidx == e1
    member = jnp.where(oh0 | oh1, 1.0, 0.0)
    before = _dot(member.astype(_BF), tri_ref[...]) + carry[:, 0:1]
    rank0 = jnp.sum(jnp.where(oh0, before, 0.0), axis=0, keepdims=True)
    rank1 = jnp.sum(jnp.where(oh1, before, 0.0), axis=0, keepdims=True)
    carry[...] = carry[...] + jnp.sum(member, axis=1, keepdims=True)

    zi = jnp.zeros((N_EXPERTS - 4, tr), jnp.int32)
    ri_ref[...] = jnp.concatenate(
        [e0, e1, rank0.astype(jnp.int32), rank1.astype(jnp.int32), zi], axis=0)
    rf_ref[...] = jnp.concatenate([g0, g1, jnp.zeros((N_EXPERTS - 2, tr), _F32)], axis=0)
    cnt_ref[...] = carry[...]


def _router(x, wr_t):
    n, d = x.shape
    tr = min(ROUTE_TILE, n)
    assert n % tr == 0
    tri = (jnp.arange(tr)[:, None] < jnp.arange(tr)[None, :]).astype(_BF)
    return pl.pallas_call(
        _router_kernel,
        grid=(n // tr,),
        in_specs=[pl.BlockSpec((tr, d), lambda i: (i, 0)),
                  _const_spec(wr_t.shape), _const_spec(tri.shape)],
        out_specs=[pl.BlockSpec((N_EXPERTS, tr), lambda i: (0, i)),
                   pl.BlockSpec((N_EXPERTS, tr), lambda i: (0, i)),
                   pl.BlockSpec((N_EXPERTS, LANES), lambda i: (0, 0))],
        out_shape=[jax.ShapeDtypeStruct((N_EXPERTS, n), jnp.int32),
                   jax.ShapeDtypeStruct((N_EXPERTS, n), _F32),
                   jax.ShapeDtypeStruct((N_EXPERTS, LANES), _F32)],
        scratch_shapes=[pltpu.VMEM((N_EXPERTS, LANES), _F32)],
        compiler_params=pltpu.CompilerParams(
            dimension_semantics=("arbitrary",), vmem_limit_bytes=VMEM_LIMIT),
        name="router",
    )(x, wr_t, tri)


def _move_tile(n):
    tm = min(MOVE_TILE, n)
    assert n % tm == 0 and tm % SUBLANES == 0
    return tm


def _expert_kernel(te_ref, na_ref, xs_ref, w1_ref, w3_ref, w2_ref, ys_ref):
    del te_ref
    active = pl.program_id(0) < na_ref[0]

    @pl.when(active)
    def _():
        ys_ref[...] = _swiglu_tile(xs_ref[...].astype(_BF), w1_ref, w3_ref, w2_ref)

    @pl.when(jnp.logical_not(active))
    def _():
        ys_ref[...] = jnp.zeros_like(ys_ref)


def _expert_ffn(xs, layer, w1, w3, w2, tile_expert, n_active):
    d = xs.shape[1]
    tm = FFN_TILE
    d_ff = w1.shape[3]
    n_tiles = tile_expert.shape[0]
    n_slots = n_tiles * tm
    w_in_spec = pl.BlockSpec((None, None, d, d_ff), lambda i, te, na: (layer, te[i], 0, 0),
                             pipeline_mode=pl.Buffered(1))
    w_out_spec = pl.BlockSpec((None, None, d_ff, d), lambda i, te, na: (layer, te[i], 0, 0),
                              pipeline_mode=pl.Buffered(1))
    return pl.pallas_call(
        _expert_kernel,
        grid_spec=pltpu.PrefetchScalarGridSpec(
            num_scalar_prefetch=2,
            grid=(n_tiles,),
            in_specs=[pl.BlockSpec((tm, d), lambda i, te, na: (jnp.minimum(i, na[0] - 1), 0)),
                      w_in_spec, w_in_spec, w_out_spec],
            out_specs=pl.BlockSpec((tm, d), lambda i, te, na: (i, 0)),
        ),
        out_shape=jax.ShapeDtypeStruct((n_slots, d), _F32),
        compiler_params=pltpu.CompilerParams(
            dimension_semantics=("arbitrary",), vmem_limit_bytes=VMEM_LIMIT),
        name="expert_ffn",
    )(tile_expert, n_active, xs, w1, w3, w2)


def _sc_gather(table, idx):
    m, d = idx.shape[0], table.shape[1]
    workers = SC_CORES * SC_SUBCORES
    per_worker = m // workers
    pairs = per_worker // (2 * SC_ROWS)
    assert m % (workers * 2 * SC_ROWS) == 0

    def body(table_hbm, idx_hbm, out_hbm, idx_a, idx_b, rows_a, rows_b, gsem_a, gsem_b, wsem_a, wsem_b):
        base = (lax.axis_index("s") * SC_CORES + lax.axis_index("c")) * per_worker
        buf_a = (idx_a, rows_a, gsem_a, wsem_a)
        buf_b = (idx_b, rows_b, gsem_b, wsem_b)

        def out_rows(c):
            return out_hbm.at[pl.ds(pl.multiple_of(base + c * SC_ROWS, SC_ROWS), SC_ROWS)]

        def start_gather(c, buf):
            idx_v, rows_v, gsem, _ = buf
            off = pl.multiple_of(base + c * SC_ROWS, SC_ROWS)
            pltpu.sync_copy(idx_hbm.at[pl.ds(off, SC_ROWS)], idx_v)
            pltpu.async_copy(table_hbm.at[idx_v], rows_v, gsem)

        def start_write(c, buf):
            idx_v, rows_v, gsem, wsem = buf
            pltpu.make_async_copy(table_hbm.at[idx_v], rows_v, gsem).wait()
            pltpu.async_copy(rows_v, out_rows(c), wsem)

        def wait_write(c, buf):
            _, rows_v, _, wsem = buf
            pltpu.make_async_copy(rows_v, out_rows(c), wsem).wait()

        start_gather(0, buf_a)

        @pl.loop(0, pairs)
        def _(p):
            c = 2 * p

            @pl.when(p > 0)
            def _():
                wait_write(c - 1, buf_b)

            start_gather(c + 1, buf_b)
            start_write(c, buf_a)
            wait_write(c, buf_a)

            @pl.when(p < pairs - 1)
            def _():
                start_gather(c + 2, buf_a)

            start_write(c + 1, buf_b)

        wait_write(2 * pairs - 1, buf_b)

    rows = pltpu.VMEM((SC_ROWS, d), table.dtype)
    index = pltpu.VMEM((SC_ROWS,), jnp.int32)
    return pl.kernel(
        body, out_type=jax.ShapeDtypeStruct((m, d), table.dtype), mesh=_sc_mesh(),
        scratch_types=[index, index, rows, rows] + [pltpu.SemaphoreType.DMA] * 4,
        name="sc_gather")(table, idx)


def _sc_mesh():
    return plsc.VectorSubcoreMesh(core_axis_name="c", subcore_axis_name="s",
                                  num_cores=SC_CORES, num_subcores=SC_SUBCORES)


def _sc_scatter(x, pos, pad_slots, n_slots):
    n, d = x.shape
    workers = SC_CORES * SC_SUBCORES
    per_worker = n // workers
    pad_per_worker = pad_slots.shape[0] // workers
    assert n % (workers * SC_ROWS) == 0 and pad_slots.shape[0] % (workers * SC_ROWS) == 0
    assert pos.shape[0] == 2 * n and n_slots == 2 * n + pad_slots.shape[0]

    pairs = per_worker // (2 * SC_ROWS)
    assert per_worker % (2 * SC_ROWS) == 0

    def body(x_hbm, pos_hbm, pad_hbm, out_hbm, i0_a, i1_a, i0_b, i1_b, rows_a, rows_b,
             lsem_a, lsem_b, ssem_a, ssem_b):
        wid = lax.axis_index("s") * SC_CORES + lax.axis_index("c")
        buf_a = (i0_a, i1_a, rows_a, lsem_a, ssem_a)
        buf_b = (i0_b, i1_b, rows_b, lsem_b, ssem_b)

        def x_rows(c):
            return x_hbm.at[pl.ds(pl.multiple_of(wid * per_worker + c * SC_ROWS, SC_ROWS), SC_ROWS)]

        def start_load(c, buf):
            i0, i1, rows_v, lsem, _ = buf
            off = pl.multiple_of(wid * per_worker + c * SC_ROWS, SC_ROWS)
            pltpu.async_copy(x_rows(c), rows_v, lsem)
            pltpu.sync_copy(pos_hbm.at[pl.ds(off, SC_ROWS)], i0)
            pltpu.sync_copy(pos_hbm.at[pl.ds(n + off, SC_ROWS)], i1)

        def start_scatter(c, buf):
            i0, i1, rows_v, lsem, ssem = buf
            pltpu.make_async_copy(x_rows(c), rows_v, lsem).wait()
            pltpu.async_copy(rows_v, out_hbm.at[i0], ssem)
            pltpu.async_copy(rows_v, out_hbm.at[i1], ssem)

        def wait_scatter(buf):
            i0, i1, rows_v, _, ssem = buf
            pltpu.make_async_copy(rows_v, out_hbm.at[i0], ssem).wait()
            pltpu.make_async_copy(rows_v, out_hbm.at[i1], ssem).wait()

        start_load(0, buf_a)

        @pl.loop(0, pairs)
        def _(p):
            c = 2 * p

            @pl.when(p > 0)
            def _():
                wait_scatter(buf_b)

            start_load(c + 1, buf_b)
            start_scatter(c, buf_a)
            wait_scatter(buf_a)

            @pl.when(p < pairs - 1)
            def _():
                start_load(c + 2, buf_a)

            start_scatter(c + 1, buf_b)

        wait_scatter(buf_b)

        @pl.loop(0, pad_per_worker // SC_ROWS)
        def _(c):
            off = pl.multiple_of(wid * pad_per_worker + c * SC_ROWS, SC_ROWS)
            pltpu.sync_copy(pad_hbm.at[pl.ds(off, SC_ROWS)], i0_a)
            pltpu.sync_copy(rows_b, out_hbm.at[i0_a])

    rows = pltpu.VMEM((SC_ROWS, d), x.dtype)
    index = pltpu.VMEM((SC_ROWS,), jnp.int32)
    return pl.kernel(
        body, out_type=jax.ShapeDtypeStruct((n_slots, d), x.dtype), mesh=_sc_mesh(),
        scratch_types=[index] * 4 + [rows, rows] + [pltpu.SemaphoreType.DMA] * 4,
        name="sc_scatter")(x, pos, pad_slots)


def _combine_kernel(alpha, n_groups, *refs):
    lg_ref, lb_ref, o_ref = refs[-3:]
    steps = pl.num_programs(0) // n_groups
    i = pl.program_id(0)
    for g in range(n_groups):
        x_ref, y0_ref, y1_ref, gate_ref = refs[4 * g:4 * g + 4]

        @pl.when((i >= g * steps) & (i < (g + 1) * steps))
        def _(x_ref=x_ref, y0_ref=y0_ref, y1_ref=y1_ref, gate_ref=gate_ref):
            gate = gate_ref[...]
            f = y0_ref[...] * gate[:, 0:1] + y1_ref[...] * gate[:, 1:2]
            o_ref[...] = _layer_norm(alpha * x_ref[...] + f, lg_ref[...], lb_ref[...])


def _combine(parts, lg, lb, alpha):
    n, d = parts[0][0].shape
    tm = _move_tile(n)
    steps = n // tm
    in_specs, args = [], []
    for g, (x, y01, gates) in enumerate(parts):
        row = lambda i, g=g: jnp.clip(i - g * steps, 0, steps - 1)
        in_specs += [pl.BlockSpec((tm, d), lambda i, row=row: (row(i), 0)),
                     pl.BlockSpec((tm, d), lambda i, row=row: (row(i), 0)),
                     pl.BlockSpec((tm, d), lambda i, row=row: (row(i) + steps, 0)),
                     pl.BlockSpec((tm, 2), lambda i, row=row: (row(i), 0))]
        args += [x, y01, y01, gates]
    return pl.pallas_call(
        functools.partial(_combine_kernel, alpha, len(parts)),
        grid=(len(parts) * steps,),
        in_specs=in_specs + [_const_spec((1, d)), _const_spec((1, d))],
        out_specs=pl.BlockSpec((tm, d), lambda i: (i, 0)),
        out_shape=jax.ShapeDtypeStruct((len(parts) * n, d), _F32),
        compiler_params=pltpu.CompilerParams(
            dimension_semantics=("arbitrary",), vmem_limit_bytes=VMEM_LIMIT),
        name="combine",
    )(*args, lg.reshape(1, -1), lb.reshape(1, -1))


def _moe_experts(x, router, layer, w1, w3, w2):
    n, d = x.shape
    tm = FFN_TILE
    n_tiles = -(-(2 * n) // tm) + N_EXPERTS
    n_slots = n_tiles * tm

    ri, rf, cnt = _router(x, router.T.astype(_BF))
    counts = cnt[:, 0].astype(jnp.int32)
    padded = (counts + tm - 1) // tm * tm
    group_end = jnp.cumsum(padded)
    group_start = group_end - padded
    experts = jnp.arange(N_EXPERTS, dtype=jnp.int32)[:, None]

    def slot(e, rank):
        return jnp.sum(jnp.where(e[None, :] == experts, group_start[:, None], 0), axis=0) + rank

    pos = jnp.stack([slot(ri[0], ri[2]), slot(ri[1], ri[3])])
    tile_start = jnp.arange(n_tiles, dtype=jnp.int32) * tm
    tile_expert = jnp.minimum(
        jnp.sum(tile_start[:, None] >= group_end[None, :], axis=1), N_EXPERTS - 1).astype(jnp.int32)
    n_active = (group_end[-1:] // tm).astype(jnp.int32)

    pad_len = jnp.concatenate([padded - counts, n_slots - group_end[-1:]])
    pad_begin = jnp.concatenate([group_start + counts, group_end[-1:]])
    pad_end = jnp.cumsum(pad_len)
    k = jnp.arange(n_slots - 2 * n, dtype=jnp.int32)
    run = jnp.sum(k[:, None] >= pad_end[None, :], axis=1)
    pad_slots = k + jnp.sum(
        jnp.where(run[:, None] == jnp.arange(N_EXPERTS + 1)[None, :],
                  (pad_begin - (pad_end - pad_len))[None, :], 0), axis=1)
    xs = _sc_scatter(x, pos.reshape(-1), pad_slots.astype(jnp.int32), n_slots)
    ys = _expert_ffn(xs, layer, w1, w3, w2, tile_expert, n_active)
    return x, _sc_gather(ys, pos.reshape(-1)), rf[:2].T


def kernel(x, w_in, w_out, conv_w, conv_b, conv_ln_g, conv_ln_b, ret_gn_g, ln1_g, ln1_b, ln2_g,
           ln2_b, dense_w1, dense_w3, dense_w2, moe_router, moe_w1, moe_w3, moe_w2):
    b, s, d = x.shape
    depth = w_in.shape[0]
    alpha = (2.0 * depth) ** 0.25
    tables = _retention_tables(s)
    ng = BATCH_GROUPS if b % BATCH_GROUPS == 0 else 1
    bg = b // ng
    groups = [(x, g * bg) for g in range(ng)]
    mix_w = (w_in.astype(_BF), w_out.astype(_BF))
    dense_w = (dense_w1.astype(_BF), dense_w3.astype(_BF), dense_w2.astype(_BF))
    moe_w = (moe_w1.astype(_BF), moe_w3.astype(_BF), moe_w2.astype(_BF))
    for l in range(depth):
        i = l // 2
        parts = []
        for g, (xg, batch0) in enumerate(groups):
            h = _mixer(xg, batch0, bg, l, *mix_w, conv_w[l], conv_b[l], conv_ln_g[l], conv_ln_b[l],
                       ret_gn_g[l], ln1_g[l], ln1_b[l], tables, alpha).reshape(bg * s, d)
            if l % 2 == 0:
                h = _dense_ffn(h, i, *dense_w, ln2_g[l], ln2_b[l], alpha)
            else:
                parts.append(_moe_experts(h, moe_router[i], i, *moe_w))
                if l == depth - 1:
                    continue
                h = _combine(parts[-1:], ln2_g[l], ln2_b[l], alpha)
            groups[g] = (h.reshape(bg, s, d), 0)
    if depth % 2 == 0:
        return _combine(parts, ln2_g[-1], ln2_b[-1], alpha).reshape(b, s, d)
    return jnp.concatenate([xg for xg, _ in groups], axis=0)
```

```python
import functools

import jax
import jax.numpy as jnp
from jax import lax
from jax.experimental import pallas as pl
from jax.experimental.pallas import tpu as pltpu
from jax.experimental.pallas import tpu_sc as plsc

CHUNK = 64
CONV_CH = 512
CONV_K = 31
RET_HEADS = 4
RET_DK = 64
RET_DV = 128
RET_QK_W = RET_HEADS * RET_DK
RET_V_W = RET_HEADS * RET_DV
ROPE_BASE = 10000.0
N_EXPERTS = 8
LN_EPS = 1e-5

LANES = 128
SUBLANES = 8
SEQ_TILE = 512
RET_BLOCK = 128
CONV_ROWS = 32
HALO = 32
FFN_TILE = 512
FFN_CHUNK = 512
ROUTE_TILE = 1024
MOVE_TILE = 512
SC_CORES = 2
SC_SUBCORES = 16
SC_ROWS = 32
BATCH_GROUPS = 2
VMEM_LIMIT = 56 * 1024 * 1024

_BF = jnp.bfloat16
_F32 = jnp.float32


def _dot(a, b):
    return jnp.dot(a, b, preferred_element_type=_F32)


def _layer_norm(v, g, b):
    mu = jnp.mean(v, axis=-1, keepdims=True)
    d = v - mu
    var = jnp.mean(d * d, axis=-1, keepdims=True)
    return d * lax.rsqrt(var + LN_EPS) * g + b


def _sigmoid(v):
    return 1.0 / (1.0 + jnp.exp(-v))


def _const_spec(shape):
    nd = len(shape)
    return pl.BlockSpec(shape, lambda *_: (0,) * nd, pipeline_mode=pl.Buffered(1))


def _layer_spec(stacked, layer):
    rest = stacked.shape[1:]
    return pl.BlockSpec((None,) + rest, lambda *_: (layer,) + (0,) * len(rest),
                        pipeline_mode=pl.Buffered(1))


def _moe_residual(alpha, x, y0, y1, gate, g, b):
    return _layer_norm(alpha * x + (y0 * gate[:, 0:1] + y1 * gate[:, 1:2]), g, b)


def _mixer_kernel(alpha, after_moe, x_ref, *refs):
    if after_moe:
        y0_ref, y1_ref, gate_ref, pg_ref, pb_ref, *refs = refs
    (w_in_ref, w_out_ref, conv_w_ref, conv_b_ref, cg_ref, cb_ref, gn_ref, lg_ref, lb_ref, cos_ref,
     sin_ref, dmask_ref, qdec_ref, kdec_ref, sdec_ref, bd_ref, o_ref, ush, state, rbuf) = refs
    t = x_ref.shape[0]
    ubuf = ush.at[0]

    @pl.when(pl.program_id(1) == 0)
    def _():
        ubuf[0:HALO, :] = jnp.zeros((HALO, CONV_CH), _F32)
        state[...] = jnp.zeros_like(state)

    x = x_ref[...]
    if after_moe:
        x = _moe_residual(alpha, x, y0_ref[...], y1_ref[...], gate_ref[...], pg_ref[...], pb_ref[...])
    xb = x.astype(_BF)

    c0 = 2 * CONV_CH
    ab = _dot(xb, w_in_ref[:, 0:c0])
    qk = _dot(xb, w_in_ref[:, c0:c0 + 2 * RET_QK_W])
    v = _dot(xb, w_in_ref[:, c0 + 2 * RET_QK_W:c0 + 2 * RET_QK_W + RET_V_W])
    g = _dot(xb, w_in_ref[:, c0 + 2 * RET_QK_W + RET_V_W:])

    ubuf[HALO:HALO + t, :] = ab[:, :CONV_CH] * _sigmoid(ab[:, CONV_CH:])
    span = t + HALO - SUBLANES
    for r in range(1, SUBLANES):
        ush[r, 0:span, :] = ubuf[r:r + span, :]
    off = HALO - (CONV_K - 1)
    conv_b = conv_b_ref[...]
    blocks = []
    for r0 in range(0, t, CONV_ROWS):
        acc = jnp.broadcast_to(conv_b, (CONV_ROWS, CONV_CH))
        for r in range(SUBLANES):
            taps = [j for j in range(CONV_K) if (j + off) % SUBLANES == r]
            reach = max((j + off) // SUBLANES for j in taps) * SUBLANES
            seg = ush[r, r0:r0 + reach + CONV_ROWS, :]
            for j in taps:
                a = (j + off) // SUBLANES * SUBLANES
                acc = acc + conv_w_ref[j:j + 1, :] * seg[a:a + CONV_ROWS]
        blocks.append(acc)
    conv = jnp.concatenate(blocks, axis=0)
    ubuf[0:HALO, :] = ubuf[t:t + HALO, :]
    un = _layer_norm(conv, cg_ref[...], cb_ref[...])
    u_out = un * _sigmoid(un)

    cos_t = cos_ref[...]
    sin_t = sin_ref[...]
    lane = lax.broadcasted_iota(jnp.int32, (1, LANES), 1)
    first_half = (lane % RET_DK) < (RET_DK // 2)

    def rope(z):
        parts = []
        for c in range(0, RET_QK_W, LANES):
            zc = z[:, c:c + LANES]
            up = pltpu.roll(zc, LANES - RET_DK // 2, axis=1)
            dn = pltpu.roll(zc, RET_DK // 2, axis=1)
            parts.append(jnp.where(first_half, up, dn))
        return z * cos_t + jnp.concatenate(parts, axis=1) * sin_t

    q = rope(qk[:, :RET_QK_W])
    k = rope(qk[:, RET_QK_W:])

    lane_qk = lax.broadcasted_iota(jnp.int32, (1, RET_QK_W), 1)
    dmask = dmask_ref[...]
    qdec = qdec_ref[...]
    kdec = kdec_ref[...]
    nr = RET_BLOCK
    for s in range(t // nr):
        rows = slice(s * nr, (s + 1) * nr)
        q_s, k_s, v_s = q[rows], k[rows], v[rows]
        v_b = v_s.astype(_BF)
        kbd = jnp.concatenate(
            [jnp.where(lane_qk // RET_DK == h, k_s, 0.0).astype(_BF) for h in range(RET_HEADS)],
            axis=0)
        sc = lax.dot_general(q_s.astype(_BF), kbd, (((1,), (1,)), ((), ())),
                             preferred_element_type=_F32)
        p = (sc * dmask).astype(_BF)
        intra = jnp.concatenate(
            [_dot(p[:, h * nr:(h + 1) * nr], v_b[:, h * RET_DV:(h + 1) * RET_DV])
             for h in range(RET_HEADS)], axis=1)
        st = state[...]
        cross = _dot((q_s * qdec).astype(_BF), st.astype(_BF))
        kv = lax.dot_general((k_s * kdec).astype(_BF), v_b, (((0,), (0,)), ((), ())),
                             preferred_element_type=_F32)
        state[...] = st * sdec_ref[...] + kv * bd_ref[...]
        rbuf[rows, :] = intra + cross

    r = rbuf[...]
    gate = g * _sigmoid(g)
    gn = gn_ref[...]
    r_parts = []
    for h in range(RET_HEADS):
        hs = slice(h * RET_DV, (h + 1) * RET_DV)
        rh = r[:, hs]
        mu = jnp.mean(rh, axis=-1, keepdims=True)
        d = rh - mu
        var = jnp.mean(d * d, axis=-1, keepdims=True)
        r_parts.append(gate[:, hs] * (d * lax.rsqrt(var + LN_EPS) * gn[:, hs]))

    mixed = jnp.concatenate([u_out] + r_parts, axis=1).astype(_BF)
    mix = _dot(mixed, w_out_ref[...])
    o_ref[...] = _layer_norm(alpha * x + mix, lg_ref[...], lb_ref[...])


def _retention_tables(seq):
    half = RET_DK // 2
    inv_freq = ROPE_BASE ** (-jnp.arange(half, dtype=_F32) / half)
    ang = jnp.arange(seq, dtype=jnp.int32).astype(_F32)[:, None] * inv_freq[None, :]
    cos, sin = jnp.cos(ang), jnp.sin(ang)
    cos_t = jnp.tile(jnp.concatenate([cos, cos], axis=1), (1, RET_HEADS))
    sin_t = jnp.tile(jnp.concatenate([-sin, sin], axis=1), (1, RET_HEADS))

    nr = RET_BLOCK
    log_gamma = jnp.log1p(-(2.0 ** (-5.0 - jnp.arange(RET_HEADS, dtype=_F32))))
    idx = jnp.arange(nr, dtype=_F32)
    dist = jnp.abs(idx[:, None] - idx[None, :])
    visible = (jnp.arange(nr)[None, :] // CHUNK) <= (jnp.arange(nr)[:, None] // CHUNK)
    scale = RET_DK ** -0.5
    dm = jnp.exp(log_gamma[:, None, None] * dist[None]) * visible[None] * scale
    dmask = jnp.transpose(dm, (1, 0, 2)).reshape(nr, RET_HEADS * nr)
    qdec = jnp.repeat(jnp.exp(log_gamma[None, :] * (idx + 1.0)[:, None]), RET_DK, axis=1)
    kdec = jnp.repeat(jnp.exp(log_gamma[None, :] * (nr - 1.0 - idx)[:, None]), RET_DK, axis=1) * scale
    sdec = jnp.broadcast_to(jnp.repeat(jnp.exp(log_gamma * nr), RET_DK)[:, None],
                            (RET_QK_W, RET_V_W))
    bd = (jnp.arange(RET_QK_W)[:, None] // RET_DK == jnp.arange(RET_V_W)[None, :] // RET_DV)
    return cos_t, sin_t, dmask, qdec, kdec, sdec, bd.astype(_F32)


def _mixer(x, batch0, nb, layer, w_in, w_out, conv_w, conv_b, cg, cb, gn, lg, lb, tables, alpha,
           moe_part=None):
    _, s, d = x.shape
    t = min(SEQ_TILE, s)
    assert s % t == 0 and t % RET_BLOCK == 0 and RET_BLOCK % CHUNK == 0 and t % CONV_ROWS == 0
    cos_t, sin_t, dmask, qdec, kdec, sdec, bd = tables
    row = lambda a: a.reshape(1, -1)
    conv_w = jnp.pad(conv_w, ((0, HALO - CONV_K), (0, 0)))
    consts = [conv_w, row(conv_b), row(cg), row(cb), row(gn), row(lg), row(lb)]
    tail = [dmask, qdec, kdec, sdec, bd]
    pre, pre_specs = [], []
    if moe_part is not None:
        assert batch0 == 0 and x.shape[0] == nb
        y01, gates, pg, pb = moe_part
        y01 = y01.reshape(2, nb, s, d)
        pre = [y01, y01, gates.reshape(nb, s, 2), row(pg), row(pb)]
        pre_specs = [pl.BlockSpec((None, None, t, d), lambda i, j: (0, i, j, 0)),
                     pl.BlockSpec((None, None, t, d), lambda i, j: (1, i, j, 0)),
                     pl.BlockSpec((None, t, 2), lambda i, j: (i, j, 0)),
                     _const_spec((1, d)), _const_spec((1, d))]
    in_specs = ([pl.BlockSpec((None, t, d), lambda i, j: (i + batch0, j, 0))] + pre_specs
                + [_layer_spec(w_in, layer), _layer_spec(w_out, layer)]
                + [_const_spec(a.shape) for a in consts]
                + [pl.BlockSpec((t, RET_QK_W), lambda i, j: (j, 0))] * 2
                + [_const_spec(a.shape) for a in tail])
    return pl.pallas_call(
        functools.partial(_mixer_kernel, alpha, moe_part is not None),
        grid=(nb, s // t),
        in_specs=in_specs,
        out_specs=pl.BlockSpec((None, t, d), lambda i, j: (i, j, 0)),
        out_shape=jax.ShapeDtypeStruct((nb, s, d), _F32),
        scratch_shapes=[pltpu.VMEM((SUBLANES, t + HALO, CONV_CH), _F32),
                        pltpu.VMEM((RET_QK_W, RET_V_W), _F32),
                        pltpu.VMEM((t, RET_V_W), _F32)],
        compiler_params=pltpu.CompilerParams(
            dimension_semantics=("arbitrary", "arbitrary"), vmem_limit_bytes=VMEM_LIMIT),
        name="mixer",
    )(x, *pre, w_in, w_out, *consts, cos_t, sin_t, *tail)


def _swiglu_tile(xb, w1_ref, w3_ref, w2_ref):
    d_ff = w1_ref.shape[1]
    acc = None
    for c in range(0, d_ff, FFN_CHUNK):
        a = _dot(xb, w1_ref[:, c:c + FFN_CHUNK])
        g = _dot(xb, w3_ref[:, c:c + FFN_CHUNK])
        h = (a * _sigmoid(a) * g).astype(_BF)
        part = _dot(h, w2_ref[c:c + FFN_CHUNK, :])
        acc = part if acc is None else acc + part
    return acc


def _dense_ffn_kernel(alpha, x_ref, w1_ref, w3_ref, w2_ref, lg_ref, lb_ref, o_ref):
    x = x_ref[...]
    f = _swiglu_tile(x.astype(_BF), w1_ref, w3_ref, w2_ref)
    o_ref[...] = _layer_norm(alpha * x + f, lg_ref[...], lb_ref[...])


def _dense_ffn(x, layer, w1, w3, w2, lg, lb, alpha):
    n, d = x.shape
    tm = min(FFN_TILE, n)
    assert n % tm == 0 and w1.shape[2] % FFN_CHUNK == 0
    weights = [w1, w3, w2]
    norm = [lg.reshape(1, -1), lb.reshape(1, -1)]
    return pl.pallas_call(
        functools.partial(_dense_ffn_kernel, alpha),
        grid=(n // tm,),
        in_specs=([pl.BlockSpec((tm, d), lambda i: (i, 0))]
                  + [_layer_spec(w, layer) for w in weights]
                  + [_const_spec(a.shape) for a in norm]),
        out_specs=pl.BlockSpec((tm, d), lambda i: (i, 0)),
        out_shape=jax.ShapeDtypeStruct((n, d), _F32),
        compiler_params=pltpu.CompilerParams(
            dimension_semantics=("arbitrary",), vmem_limit_bytes=VMEM_LIMIT),
        name="dense_ffn",
    )(x, *weights, *norm)


def _router_kernel(x_ref, wr_ref, tri_ref, ri_ref, rf_ref, cnt_ref, carry):
    @pl.when(pl.program_id(0) == 0)
    def _():
        carry[...] = jnp.zeros_like(carry)

    tr = x_ref.shape[0]
    logits = lax.dot_general(wr_ref[...], x_ref[...].astype(_BF), (((1,), (1,)), ((), ())),
                             preferred_element_type=_F32)
    eidx = lax.broadcasted_iota(jnp.int32, (N_EXPERTS, tr), 0)
    m0 = jnp.max(logits, axis=0, keepdims=True)
    e0 = jnp.min(jnp.where(logits == m0, eidx, N_EXPERTS), axis=0, keepdims=True)
    rest = jnp.where(eidx == e0, -jnp.inf, logits)
    m1 = jnp.max(rest, axis=0, keepdims=True)
    e1 = jnp.min(jnp.where(rest == m1, eidx, N_EXPERTS), axis=0, keepdims=True)
    tt = jnp.exp(m1 - m0)
    g0 = 1.0 / (1.0 + tt)
    g1 = tt / (1.0 + tt)

    oh0 = eidx == e0
    oh1 = eidx == e1
    member = jnp.where(oh0 | oh1, 1.0, 0.0)
    before = _dot(member.astype(_BF), tri_ref[...]) + carry[:, 0:1]
    rank0 = jnp.sum(jnp.where(oh0, before, 0.0), axis=0, keepdims=True)
    rank1 = jnp.sum(jnp.where(oh1, before, 0.0), axis=0, keepdims=True)
    carry[...] = carry[...] + jnp.sum(member, axis=1, keepdims=True)

    zi = jnp.zeros((N_EXPERTS - 4, tr), jnp.int32)
    ri_ref[...] = jnp.concatenate(
        [e0, e1, rank0.astype(jnp.int32), rank1.astype(jnp.int32), zi], axis=0)
    rf_ref[...] = jnp.concatenate([g0, g1, jnp.zeros((N_EXPERTS - 2, tr), _F32)], axis=0)
    cnt_ref[...] = carry[...]


def _router(x, wr_t):
    n, d = x.shape
    tr = min(ROUTE_TILE, n)
    assert n % tr == 0
    tri = (jnp.arange(tr)[:, None] < jnp.arange(tr)[None, :]).astype(_BF)
    return pl.pallas_call(
        _router_kernel,
        grid=(n // tr,),
        in_specs=[pl.BlockSpec((tr, d), lambda i: (i, 0)),
                  _const_spec(wr_t.shape), _const_spec(tri.shape)],
        out_specs=[pl.BlockSpec((N_EXPERTS, tr), lambda i: (0, i)),
                   pl.BlockSpec((N_EXPERTS, tr), lambda i: (0, i)),
                   pl.BlockSpec((N_EXPERTS, LANES), lambda i: (0, 0))],
        out_shape=[jax.ShapeDtypeStruct((N_EXPERTS, n), jnp.int32),
                   jax.ShapeDtypeStruct((N_EXPERTS, n), _F32),
                   jax.ShapeDtypeStruct((N_EXPERTS, LANES), _F32)],
        scratch_shapes=[pltpu.VMEM((N_EXPERTS, LANES), _F32)],
        compiler_params=pltpu.CompilerParams(
            dimension_semantics=("arbitrary",), vmem_limit_bytes=VMEM_LIMIT),
        name="router",
    )(x, wr_t, tri)


def _move_tile(n):
    tm = min(MOVE_TILE, n)
    assert n % tm == 0 and tm % SUBLANES == 0
    return tm


def _expert_kernel(te_ref, na_ref, xs_ref, w1_ref, w3_ref, w2_ref, ys_ref):
    del te_ref
    active = pl.program_id(0) < na_ref[0]

    @pl.when(active)
    def _():
        ys_ref[...] = _swiglu_tile(xs_ref[...].astype(_BF), w1_ref, w3_ref, w2_ref)

    @pl.when(jnp.logical_not(active))
    def _():
        ys_ref[...] = jnp.zeros_like(ys_ref)


def _expert_ffn(xs, layer, w1, w3, w2, tile_expert, n_active):
    d = xs.shape[1]
    tm = FFN_TILE
    d_ff = w1.shape[3]
    n_tiles = tile_expert.shape[0]
    n_slots = n_tiles * tm
    w_in_spec = pl.BlockSpec((None, None, d, d_ff), lambda i, te, na: (layer, te[i], 0, 0),
                             pipeline_mode=pl.Buffered(1))
    w_out_spec = pl.BlockSpec((None, None, d_ff, d), lambda i, te, na: (layer, te[i], 0, 0),
                              pipeline_mode=pl.Buffered(1))
    return pl.pallas_call(
        _expert_kernel,
        grid_spec=pltpu.PrefetchScalarGridSpec(
            num_scalar_prefetch=2,
            grid=(n_tiles,),
            in_specs=[pl.BlockSpec((tm, d), lambda i, te, na: (jnp.minimum(i, na[0] - 1), 0)),
                      w_in_spec, w_in_spec, w_out_spec],
            out_specs=pl.BlockSpec((tm, d), lambda i, te, na: (i, 0)),
        ),
        out_shape=jax.ShapeDtypeStruct((n_slots, d), _F32),
        compiler_params=pltpu.CompilerParams(
            dimension_semantics=("arbitrary",), vmem_limit_bytes=VMEM_LIMIT),
        name="expert_ffn",
    )(tile_expert, n_active, xs, w1, w3, w2)


def _sc_gather(table, idx):
    m, d = idx.shape[0], table.shape[1]
    workers = SC_CORES * SC_SUBCORES
    per_worker = m // workers
    pairs = per_worker // (2 * SC_ROWS)
    assert m % (workers * 2 * SC_ROWS) == 0

    def body(table_hbm, idx_hbm, out_hbm, idx_a, idx_b, rows_a, rows_b, gsem_a, gsem_b, wsem_a, wsem_b):
        base = (lax.axis_index("s") * SC_CORES + lax.axis_index("c")) * per_worker
        buf_a = (idx_a, rows_a, gsem_a, wsem_a)
        buf_b = (idx_b, rows_b, gsem_b, wsem_b)

        def out_rows(c):
            return out_hbm.at[pl.ds(pl.multiple_of(base + c * SC_ROWS, SC_ROWS), SC_ROWS)]

        def start_gather(c, buf):
            idx_v, rows_v, gsem, _ = buf
            off = pl.multiple_of(base + c * SC_ROWS, SC_ROWS)
            pltpu.sync_copy(idx_hbm.at[pl.ds(off, SC_ROWS)], idx_v)
            pltpu.async_copy(table_hbm.at[idx_v], rows_v, gsem)

        def start_write(c, buf):
            idx_v, rows_v, gsem, wsem = buf
            pltpu.make_async_copy(table_hbm.at[idx_v], rows_v, gsem).wait()
            pltpu.async_copy(rows_v, out_rows(c), wsem)

        def wait_write(c, buf):
            _, rows_v, _, wsem = buf
            pltpu.make_async_copy(rows_v, out_rows(c), wsem).wait()

        start_gather(0, buf_a)

        @pl.loop(0, pairs)
        def _(p):
            c = 2 * p

            @pl.when(p > 0)
            def _():
                wait_write(c - 1, buf_b)

            start_gather(c + 1, buf_b)
            start_write(c, buf_a)
            wait_write(c, buf_a)

            @pl.when(p < pairs - 1)
            def _():
                start_gather(c + 2, buf_a)

            start_write(c + 1, buf_b)

        wait_write(2 * pairs - 1, buf_b)

    rows = pltpu.VMEM((SC_ROWS, d), table.dtype)
    index = pltpu.VMEM((SC_ROWS,), jnp.int32)
    return pl.kernel(
        body, out_type=jax.ShapeDtypeStruct((m, d), table.dtype), mesh=_sc_mesh(),
        scratch_types=[index, index, rows, rows] + [pltpu.SemaphoreType.DMA] * 4,
        name="sc_gather")(table, idx)


def _sc_mesh():
    return plsc.VectorSubcoreMesh(core_axis_name="c", subcore_axis_name="s",
                                  num_cores=SC_CORES, num_subcores=SC_SUBCORES)


def _sc_scatter(x, pos, pad_slots, n_slots):
    n, d = x.shape
    workers = SC_CORES * SC_SUBCORES
    per_worker = n // workers
    pad_per_worker = pad_slots.shape[0] // workers
    assert n % (workers * SC_ROWS) == 0 and pad_slots.shape[0] % (workers * SC_ROWS) == 0
    assert pos.shape[0] == 2 * n and n_slots == 2 * n + pad_slots.shape[0]

    pairs = per_worker // (2 * SC_ROWS)
    assert per_worker % (2 * SC_ROWS) == 0

    def body(x_hbm, pos_hbm, pad_hbm, out_hbm, i0_a, i1_a, i0_b, i1_b, rows_a, rows_b,
             lsem_a, lsem_b, ssem_a, ssem_b):
        wid = lax.axis_index("s") * SC_CORES + lax.axis_index("c")
        buf_a = (i0_a, i1_a, rows_a, lsem_a, ssem_a)
        buf_b = (i0_b, i1_b, rows_b, lsem_b, ssem_b)

        def x_rows(c):
            return x_hbm.at[pl.ds(pl.multiple_of(wid * per_worker + c * SC_ROWS, SC_ROWS), SC_ROWS)]

        def start_load(c, buf):
            i0, i1, rows_v, lsem, _ = buf
            off = pl.multiple_of(wid * per_worker + c * SC_ROWS, SC_ROWS)
            pltpu.async_copy(x_rows(c), rows_v, lsem)
            pltpu.sync_copy(pos_hbm.at[pl.ds(off, SC_ROWS)], i0)
            pltpu.sync_copy(pos_hbm.at[pl.ds(n + off, SC_ROWS)], i1)

        def start_scatter(c, buf):
            i0, i1, rows_v, lsem, ssem = buf
            pltpu.make_async_copy(x_rows(c), rows_v, lsem).wait()
            pltpu.async_copy(rows_v, out_hbm.at[i0], ssem)
            pltpu.async_copy(rows_v, out_hbm.at[i1], ssem)

        def wait_scatter(buf):
            i0, i1, rows_v, _, ssem = buf
            pltpu.make_async_copy(rows_v, out_hbm.at[i0], ssem).wait()
            pltpu.make_async_copy(rows_v, out_hbm.at[i1], ssem).wait()

        start_load(0, buf_a)

        @pl.loop(0, pairs)
        def _(p):
            c = 2 * p

            @pl.when(p > 0)
            def _():
                wait_scatter(buf_b)

            start_load(c + 1, buf_b)
            start_scatter(c, buf_a)
            wait_scatter(buf_a)

            @pl.when(p < pairs - 1)
            def _():
                start_load(c + 2, buf_a)

            start_scatter(c + 1, buf_b)

        wait_scatter(buf_b)

        @pl.loop(0, pad_per_worker // SC_ROWS)
        def _(c):
            off = pl.multiple_of(wid * pad_per_worker + c * SC_ROWS, SC_ROWS)
            pltpu.sync_copy(pad_hbm.at[pl.ds(off, SC_ROWS)], i0_a)
            pltpu.sync_copy(rows_b, out_hbm.at[i0_a])

    rows = pltpu.VMEM((SC_ROWS, d), x.dtype)
    index = pltpu.VMEM((SC_ROWS,), jnp.int32)
    return pl.kernel(
        body, out_type=jax.ShapeDtypeStruct((n_slots, d), x.dtype), mesh=_sc_mesh(),
        scratch_types=[index] * 4 + [rows, rows] + [pltpu.SemaphoreType.DMA] * 4,
        name="sc_scatter")(x, pos, pad_slots)


def _combine_kernel(alpha, n_groups, *refs):
    lg_ref, lb_ref, o_ref = refs[-3:]
    steps = pl.num_programs(0) // n_groups
    i = pl.program_id(0)
    for g in range(n_groups):
        x_ref, y0_ref, y1_ref, gate_ref = refs[4 * g:4 * g + 4]

        @pl.when((i >= g * steps) & (i < (g + 1) * steps))
        def _(x_ref=x_ref, y0_ref=y0_ref, y1_ref=y1_ref, gate_ref=gate_ref):
            o_ref[...] = _moe_residual(alpha, x_ref[...], y0_ref[...], y1_ref[...], gate_ref[...],
                                       lg_ref[...], lb_ref[...])


def _combine(parts, lg, lb, alpha):
    n, d = parts[0][0].shape
    tm = _move_tile(n)
    steps = n // tm
    in_specs, args = [], []
    for g, (x, y01, gates) in enumerate(parts):
        row = lambda i, g=g: jnp.clip(i - g * steps, 0, steps - 1)
        in_specs += [pl.BlockSpec((tm, d), lambda i, row=row: (row(i), 0)),
                     pl.BlockSpec((tm, d), lambda i, row=row: (row(i), 0)),
                     pl.BlockSpec((tm, d), lambda i, row=row: (row(i) + steps, 0)),
                     pl.BlockSpec((tm, 2), lambda i, row=row: (row(i), 0))]
        args += [x, y01, y01, gates]
    return pl.pallas_call(
        functools.partial(_combine_kernel, alpha, len(parts)),
        grid=(len(parts) * steps,),
        in_specs=in_specs + [_const_spec((1, d)), _const_spec((1, d))],
        out_specs=pl.BlockSpec((tm, d), lambda i: (i, 0)),
        out_shape=jax.ShapeDtypeStruct((len(parts) * n, d), _F32),
        compiler_params=pltpu.CompilerParams(
            dimension_semantics=("arbitrary",), vmem_limit_bytes=VMEM_LIMIT),
        name="combine",
    )(*args, lg.reshape(1, -1), lb.reshape(1, -1))


def _moe_experts(x, router, layer, w1, w3, w2):
    n, d = x.shape
    tm = FFN_TILE
    n_tiles = -(-(2 * n) // tm) + N_EXPERTS
    n_slots = n_tiles * tm

    ri, rf, cnt = _router(x, router.T.astype(_BF))
    counts = cnt[:, 0].astype(jnp.int32)
    padded = (counts + tm - 1) // tm * tm
    group_end = jnp.cumsum(padded)
    group_start = group_end - padded
    experts = jnp.arange(N_EXPERTS, dtype=jnp.int32)[:, None]

    def slot(e, rank):
        return jnp.sum(jnp.where(e[None, :] == experts, group_start[:, None], 0), axis=0) + rank

    pos = jnp.stack([slot(ri[0], ri[2]), slot(ri[1], ri[3])])
    tile_start = jnp.arange(n_tiles, dtype=jnp.int32) * tm
    tile_expert = jnp.minimum(
        jnp.sum(tile_start[:, None] >= group_end[None, :], axis=1), N_EXPERTS - 1).astype(jnp.int32)
    n_active = (group_end[-1:] // tm).astype(jnp.int32)

    pad_len = jnp.concatenate([padded - counts, n_slots - group_end[-1:]])
    pad_begin = jnp.concatenate([group_start + counts, group_end[-1:]])
    pad_end = jnp.cumsum(pad_len)
    k = jnp.arange(n_slots - 2 * n, dtype=jnp.int32)
    run = jnp.sum(k[:, None] >= pad_end[None, :], axis=1)
    pad_slots = k + jnp.sum(
        jnp.where(run[:, None] == jnp.arange(N_EXPERTS + 1)[None, :],
                  (pad_begin - (pad_end - pad_len))[None, :], 0), axis=1)
    xs = _sc_scatter(x, pos.reshape(-1), pad_slots.astype(jnp.int32), n_slots)
    ys = _expert_ffn(xs, layer, w1, w3, w2, tile_expert, n_active)
    return x, _sc_gather(ys, pos.reshape(-1)), rf[:2].T


def kernel(x, w_in, w_out, conv_w, conv_b, conv_ln_g, conv_ln_b, ret_gn_g, ln1_g, ln1_b, ln2_g,
           ln2_b, dense_w1, dense_w3, dense_w2, moe_router, moe_w1, moe_w3, moe_w2):
    b, s, d = x.shape
    depth = w_in.shape[0]
    alpha = (2.0 * depth) ** 0.25
    tables = _retention_tables(s)
    ng = BATCH_GROUPS if b % BATCH_GROUPS == 0 else 1
    bg = b // ng
    groups = [(x, g * bg, None) for g in range(ng)]
    mix_w = (w_in.astype(_BF), w_out.astype(_BF))
    dense_w = (dense_w1.astype(_BF), dense_w3.astype(_BF), dense_w2.astype(_BF))
    moe_w = (moe_w1.astype(_BF), moe_w3.astype(_BF), moe_w2.astype(_BF))
    for l in range(depth):
        i = l // 2
        for g, (xg, batch0, moe_part) in enumerate(groups):
            h = _mixer(xg, batch0, bg, l, *mix_w, conv_w[l], conv_b[l], conv_ln_g[l], conv_ln_b[l],
                       ret_gn_g[l], ln1_g[l], ln1_b[l], tables, alpha, moe_part).reshape(bg * s, d)
            if l % 2 == 0:
                h = _dense_ffn(h, i, *dense_w, ln2_g[l], ln2_b[l], alpha)
                groups[g] = (h.reshape(bg, s, d), 0, None)
            else:
                x_in, y01, gates = _moe_experts(h, moe_router[i], i, *moe_w)
                groups[g] = (x_in.reshape(bg, s, d), 0, (y01, gates, ln2_g[l], ln2_b[l]))
    if groups[0][2] is None:
        return jnp.concatenate([xg for xg, _, _ in groups], axis=0)
    parts = [(xg.reshape(bg * s, d), part[0], part[1]) for xg, _, part in groups]
    return _combine(parts, ln2_g[-1], ln2_b[-1], alpha).reshape(b, s, d)
```

```python
import functools

import jax
import jax.numpy as jnp
from jax import lax
from jax.experimental import pallas as pl
from jax.experimental.pallas import tpu as pltpu
from jax.experimental.pallas import tpu_sc as plsc

CHUNK = 64
CONV_CH = 512
CONV_K = 31
RET_HEADS = 4
RET_DK = 64
RET_DV = 128
RET_QK_W = RET_HEADS * RET_DK
RET_V_W = RET_HEADS * RET_DV
ROPE_BASE = 10000.0
N_EXPERTS = 8
LN_EPS = 1e-5

LANES = 128
SUBLANES = 8
SEQ_TILE = 512
RET_BLOCK = 128
CONV_ROWS = 32
HALO = 32
FFN_TILE = 512
FFN_CHUNK = 512
ROUTE_TILE = 1024
MOVE_TILE = 512
SC_CORES = 2
SC_SUBCORES = 16
SC_ROWS = 32
BATCH_GROUPS = 2
VMEM_LIMIT = 56 * 1024 * 1024

_BF = jnp.bfloat16
_F32 = jnp.float32


def _dot(a, b):
    return jnp.dot(a, b, preferred_element_type=_F32)


def _layer_norm(v, g, b):
    mu = jnp.mean(v, axis=-1, keepdims=True)
    d = v - mu
    var = jnp.mean(d * d, axis=-1, keepdims=True)
    return d * lax.rsqrt(var + LN_EPS) * g + b


def _sigmoid(v):
    return 1.0 / (1.0 + jnp.exp(-v))


def _const_spec(shape):
    nd = len(shape)
    return pl.BlockSpec(shape, lambda *_: (0,) * nd, pipeline_mode=pl.Buffered(1))


def _layer_spec(stacked, layer):
    rest = stacked.shape[1:]
    return pl.BlockSpec((None,) + rest, lambda *_: (layer,) + (0,) * len(rest),
                        pipeline_mode=pl.Buffered(1))


def _moe_residual(alpha, x, y0, y1, gate, g, b):
    return _layer_norm(alpha * x + (y0 * gate[:, 0:1] + y1 * gate[:, 1:2]), g, b)


def _mixer_kernel(alpha, after_moe, x_ref, *refs):
    if after_moe:
        y0_ref, y1_ref, gate_ref, pg_ref, pb_ref, *refs = refs
    (w_in_ref, w_out_ref, conv_w_ref, conv_b_ref, cg_ref, cb_ref, gn_ref, lg_ref, lb_ref, cos_ref,
     sin_ref, dmask_ref, qdec_ref, kdec_ref, sdec_ref, bd_ref, o_ref, ush, state, rbuf) = refs
    t = x_ref.shape[0]
    ubuf = ush.at[0]

    @pl.when(pl.program_id(1) == 0)
    def _():
        ubuf[0:HALO, :] = jnp.zeros((HALO, CONV_CH), _F32)
        state[...] = jnp.zeros_like(state)

    x = x_ref[...]
    if after_moe:
        x = _moe_residual(alpha, x, y0_ref[...], y1_ref[...], gate_ref[...], pg_ref[...], pb_ref[...])
    xb = x.astype(_BF)

    c0 = 2 * CONV_CH
    ab = _dot(xb, w_in_ref[:, 0:c0])
    qk = _dot(xb, w_in_ref[:, c0:c0 + 2 * RET_QK_W])
    v = _dot(xb, w_in_ref[:, c0 + 2 * RET_QK_W:c0 + 2 * RET_QK_W + RET_V_W])
    g = _dot(xb, w_in_ref[:, c0 + 2 * RET_QK_W + RET_V_W:])

    ubuf[HALO:HALO + t, :] = ab[:, :CONV_CH] * _sigmoid(ab[:, CONV_CH:])
    span = t + HALO - SUBLANES
    for r in range(1, SUBLANES):
        ush[r, 0:span, :] = ubuf[r:r + span, :]
    off = HALO - (CONV_K - 1)
    conv_b = conv_b_ref[...]
    blocks = []
    for r0 in range(0, t, CONV_ROWS):
        acc = jnp.broadcast_to(conv_b, (CONV_ROWS, CONV_CH))
        for r in range(SUBLANES):
            taps = [j for j in range(CONV_K) if (j + off) % SUBLANES == r]
            reach = max((j + off) // SUBLANES for j in taps) * SUBLANES
            seg = ush[r, r0:r0 + reach + CONV_ROWS, :]
            for j in taps:
                a = (j + off) // SUBLANES * SUBLANES
                acc = acc + conv_w_ref[j:j + 1, :] * seg[a:a + CONV_ROWS]
        blocks.append(acc)
    conv = jnp.concatenate(blocks, axis=0)
    ubuf[0:HALO, :] = ubuf[t:t + HALO, :]
    un = _layer_norm(conv, cg_ref[...], cb_ref[...])
    u_out = un * _sigmoid(un)

    cos_t = cos_ref[...]
    sin_t = sin_ref[...]
    lane = lax.broadcasted_iota(jnp.int32, (1, LANES), 1)
    first_half = (lane % RET_DK) < (RET_DK // 2)

    def rope(z):
        parts = []
        for c in range(0, RET_QK_W, LANES):
            zc = z[:, c:c + LANES]
            up = pltpu.roll(zc, LANES - RET_DK // 2, axis=1)
            dn = pltpu.roll(zc, RET_DK // 2, axis=1)
            parts.append(jnp.where(first_half, up, dn))
        return z * cos_t + jnp.concatenate(parts, axis=1) * sin_t

    q = rope(qk[:, :RET_QK_W])
    k = rope(qk[:, RET_QK_W:])

    lane_qk = lax.broadcasted_iota(jnp.int32, (1, RET_QK_W), 1)
    dmask = dmask_ref[...]
    qdec = qdec_ref[...]
    kdec = kdec_ref[...]
    nr = RET_BLOCK
    for s in range(t // nr):
        rows = slice(s * nr, (s + 1) * nr)
        q_s, k_s, v_s = q[rows], k[rows], v[rows]
        v_b = v_s.astype(_BF)
        kbd = jnp.concatenate(
            [jnp.where(lane_qk // RET_DK == h, k_s, 0.0).astype(_BF) for h in range(RET_HEADS)],
            axis=0)
        sc = lax.dot_general(q_s.astype(_BF), kbd, (((1,), (1,)), ((), ())),
                             preferred_element_type=_F32)
        p = (sc * dmask).astype(_BF)
        intra = jnp.concatenate(
            [_dot(p[:, h * nr:(h + 1) * nr], v_b[:, h * RET_DV:(h + 1) * RET_DV])
             for h in range(RET_HEADS)], axis=1)
        st = state[...]
        cross = _dot((q_s * qdec).astype(_BF), st.astype(_BF))
        kv = lax.dot_general((k_s * kdec).astype(_BF), v_b, (((0,), (0,)), ((), ())),
                             preferred_element_type=_F32)
        state[...] = st * sdec_ref[...] + kv * bd_ref[...]
        rbuf[rows, :] = intra + cross

    r = rbuf[...]
    gate = g * _sigmoid(g)
    gn = gn_ref[...]
    r_parts = []
    for h in range(RET_HEADS):
        hs = slice(h * RET_DV, (h + 1) * RET_DV)
        rh = r[:, hs]
        mu = jnp.mean(rh, axis=-1, keepdims=True)
        d = rh - mu
        var = jnp.mean(d * d, axis=-1, keepdims=True)
        r_parts.append(gate[:, hs] * (d * lax.rsqrt(var + LN_EPS) * gn[:, hs]))

    mixed = jnp.concatenate([u_out] + r_parts, axis=1).astype(_BF)
    mix = _dot(mixed, w_out_ref[...])
    o_ref[...] = _layer_norm(alpha * x + mix, lg_ref[...], lb_ref[...])


def _retention_tables(seq):
    half = RET_DK // 2
    inv_freq = ROPE_BASE ** (-jnp.arange(half, dtype=_F32) / half)
    ang = jnp.arange(seq, dtype=jnp.int32).astype(_F32)[:, None] * inv_freq[None, :]
    cos, sin = jnp.cos(ang), jnp.sin(ang)
    cos_t = jnp.tile(jnp.concatenate([cos, cos], axis=1), (1, RET_HEADS))
    sin_t = jnp.tile(jnp.concatenate([-sin, sin], axis=1), (1, RET_HEADS))

    nr = RET_BLOCK
    log_gamma = jnp.log1p(-(2.0 ** (-5.0 - jnp.arange(RET_HEADS, dtype=_F32))))
    idx = jnp.arange(nr, dtype=_F32)
    dist = jnp.abs(idx[:, None] - idx[None, :])
    visible = (jnp.arange(nr)[None, :] // CHUNK) <= (jnp.arange(nr)[:, None] // CHUNK)
    scale = RET_DK ** -0.5
    dm = jnp.exp(log_gamma[:, None, None] * dist[None]) * visible[None] * scale
    dmask = jnp.transpose(dm, (1, 0, 2)).reshape(nr, RET_HEADS * nr)
    qdec = jnp.repeat(jnp.exp(log_gamma[None, :] * (idx + 1.0)[:, None]), RET_DK, axis=1)
    kdec = jnp.repeat(jnp.exp(log_gamma[None, :] * (nr - 1.0 - idx)[:, None]), RET_DK, axis=1) * scale
    sdec = jnp.broadcast_to(jnp.repeat(jnp.exp(log_gamma * nr), RET_DK)[:, None],
                            (RET_QK_W, RET_V_W))
    bd = (jnp.arange(RET_QK_W)[:, None] // RET_DK == jnp.arange(RET_V_W)[None, :] // RET_DV)
    return cos_t, sin_t, dmask, qdec, kdec, sdec, bd.astype(_F32)


def _mixer(x, batch0, nb, layer, w_in, w_out, conv_w, conv_b, cg, cb, gn, lg, lb, tables, alpha,
           moe_part=None):
    _, s, d = x.shape
    t = min(SEQ_TILE, s)
    assert s % t == 0 and t % RET_BLOCK == 0 and RET_BLOCK % CHUNK == 0 and t % CONV_ROWS == 0
    cos_t, sin_t, dmask, qdec, kdec, sdec, bd = tables
    row = lambda a: a.reshape(1, -1)
    conv_w = jnp.pad(conv_w, ((0, HALO - CONV_K), (0, 0)))
    consts = [conv_w, row(conv_b), row(cg), row(cb), row(gn), row(lg), row(lb)]
    tail = [dmask, qdec, kdec, sdec, bd]
    pre, pre_specs = [], []
    if moe_part is not None:
        assert batch0 == 0 and x.shape[0] == nb
        y01, gates, pg, pb = moe_part
        y01 = y01.reshape(2, nb, s, d)
        pre = [y01, y01, gates.reshape(nb, s, 2), row(pg), row(pb)]
        pre_specs = [pl.BlockSpec((None, None, t, d), lambda i, j: (0, i, j, 0)),
                     pl.BlockSpec((None, None, t, d), lambda i, j: (1, i, j, 0)),
                     pl.BlockSpec((None, t, 2), lambda i, j: (i, j, 0)),
                     _const_spec((1, d)), _const_spec((1, d))]
    in_specs = ([pl.BlockSpec((None, t, d), lambda i, j: (i + batch0, j, 0))] + pre_specs
                + [_layer_spec(w_in, layer), _layer_spec(w_out, layer)]
                + [_const_spec(a.shape) for a in consts]
                + [pl.BlockSpec((t, RET_QK_W), lambda i, j: (j, 0))] * 2
                + [_const_spec(a.shape) for a in tail])
    return pl.pallas_call(
        functools.partial(_mixer_kernel, alpha, moe_part is not None),
        grid=(nb, s // t),
        in_specs=in_specs,
        out_specs=pl.BlockSpec((None, t, d), lambda i, j: (i, j, 0)),
        out_shape=jax.ShapeDtypeStruct((nb, s, d), _F32),
        scratch_shapes=[pltpu.VMEM((SUBLANES, t + HALO, CONV_CH), _F32),
                        pltpu.VMEM((RET_QK_W, RET_V_W), _F32),
                        pltpu.VMEM((t, RET_V_W), _F32)],
        compiler_params=pltpu.CompilerParams(
            dimension_semantics=("arbitrary", "arbitrary"), vmem_limit_bytes=VMEM_LIMIT),
        name="mixer",
    )(x, *pre, w_in, w_out, *consts, cos_t, sin_t, *tail)


def _swiglu_tile(xb, w1_ref, w3_ref, w2_ref):
    d_ff = w1_ref.shape[1]
    acc = None
    for c in range(0, d_ff, FFN_CHUNK):
        a = _dot(xb, w1_ref[:, c:c + FFN_CHUNK])
        g = _dot(xb, w3_ref[:, c:c + FFN_CHUNK])
        h = (a * _sigmoid(a) * g).astype(_BF)
        part = _dot(h, w2_ref[c:c + FFN_CHUNK, :])
        acc = part if acc is None else acc + part
    return acc


def _dense_ffn_kernel(alpha, n_cast, x_ref, w1_ref, w3_ref, w2_ref, lg_ref, lb_ref, *refs):
    cast_in, (o_ref, *cast_out) = refs[:n_cast], refs[n_cast:]
    x = x_ref[...]
    f = _swiglu_tile(x.astype(_BF), w1_ref, w3_ref, w2_ref)
    o_ref[...] = _layer_norm(alpha * x + f, lg_ref[...], lb_ref[...])
    for src, dst in zip(cast_in, cast_out):
        dst[...] = src[...].astype(_BF)


def _dense_ffn(x, layer, w1, w3, w2, lg, lb, alpha, cast=(), cast_layer=0):
    n, d = x.shape
    tm = min(FFN_TILE, n)
    steps = n // tm
    assert n % tm == 0 and w1.shape[2] % FFN_CHUNK == 0
    weights = [w1, w3, w2]
    norm = [lg.reshape(1, -1), lb.reshape(1, -1)]
    cast_in, cast_specs, cast_shapes = [], [], []
    for a in cast:
        cols = a.shape[-1]
        rows = a[0].size // cols
        assert rows % (steps * 2 * SUBLANES) == 0
        cast_in.append(a.reshape(-1, cols))
        cast_specs.append(pl.BlockSpec((rows // steps, cols),
                                       lambda i, base=cast_layer * steps: (base + i, 0)))
        cast_shapes.append(jax.ShapeDtypeStruct((rows, cols), _BF))
    row_spec = pl.BlockSpec((tm, d), lambda i: (i, 0))
    out = pl.pallas_call(
        functools.partial(_dense_ffn_kernel, alpha, len(cast)),
        grid=(steps,),
        in_specs=([row_spec] + [_layer_spec(w, layer) for w in weights]
                  + [_const_spec(a.shape) for a in norm] + cast_specs),
        out_specs=[row_spec] + [pl.BlockSpec(s.block_shape, lambda i: (i, 0)) for s in cast_specs],
        out_shape=[jax.ShapeDtypeStruct((n, d), _F32)] + cast_shapes,
        compiler_params=pltpu.CompilerParams(
            dimension_semantics=("arbitrary",), vmem_limit_bytes=VMEM_LIMIT),
        name="dense_ffn",
    )(x, *weights, *norm, *cast_in)
    return out[0], [o.reshape((1,) + a.shape[1:]) for o, a in zip(out[1:], cast)]


def _router_kernel(x_ref, wr_ref, tri_ref, ri_ref, rf_ref, cnt_ref, carry):
    @pl.when(pl.program_id(0) == 0)
    def _():
        carry[...] = jnp.zeros_like(carry)

    tr = x_ref.shape[0]
    logits = lax.dot_general(wr_ref[...], x_ref[...].astype(_BF), (((1,), (1,)), ((), ())),
                             preferred_element_type=_F32)
    eidx = lax.broadcasted_iota(jnp.int32, (N_EXPERTS, tr), 0)
    m0 = jnp.max(logits, axis=0, keepdims=True)
    e0 = jnp.min(jnp.where(logits == m0, eidx, N_EXPERTS), axis=0, keepdims=True)
    rest = jnp.where(eidx == e0, -jnp.inf, logits)
    m1 = jnp.max(rest, axis=0, keepdims=True)
    e1 = jnp.min(jnp.where(rest == m1, eidx, N_EXPERTS), axis=0, keepdims=True)
    tt = jnp.exp(m1 - m0)
    g0 = 1.0 / (1.0 + tt)
    g1 = tt / (1.0 + tt)

    oh0 = eidx == e0
    oh1 = eidx == e1
    member = jnp.where(oh0 | oh1, 1.0, 0.0)
    before = _dot(member.astype(_BF), tri_ref[...]) + carry[:, 0:1]
    rank0 = jnp.sum(jnp.where(oh0, before, 0.0), axis=0, keepdims=True)
    rank1 = jnp.sum(jnp.where(oh1, before, 0.0), axis=0, keepdims=True)
    carry[...] = carry[...] + jnp.sum(member, axis=1, keepdims=True)

    zi = jnp.zeros((N_EXPERTS - 4, tr), jnp.int32)
    ri_ref[...] = jnp.concatenate(
        [e0, e1, rank0.astype(jnp.int32), rank1.astype(jnp.int32), zi], axis=0)
    rf_ref[...] = jnp.concatenate([g0, g1, jnp.zeros((N_EXPERTS - 2, tr), _F32)], axis=0)
    cnt_ref[...] = carry[...]


def _router(x, wr_t):
    n, d = x.shape
    tr = min(ROUTE_TILE, n)
    assert n % tr == 0
    tri = (jnp.arange(tr)[:, None] < jnp.arange(tr)[None, :]).astype(_BF)
    return pl.pallas_call(
        _router_kernel,
        grid=(n // tr,),
        in_specs=[pl.BlockSpec((tr, d), lambda i: (i, 0)),
                  _const_spec(wr_t.shape), _const_spec(tri.shape)],
        out_specs=[pl.BlockSpec((N_EXPERTS, tr), lambda i: (0, i)),
                   pl.BlockSpec((N_EXPERTS, tr), lambda i: (0, i)),
                   pl.BlockSpec((N_EXPERTS, LANES), lambda i: (0, 0))],
        out_shape=[jax.ShapeDtypeStruct((N_EXPERTS, n), jnp.int32),
                   jax.ShapeDtypeStruct((N_EXPERTS, n), _F32),
                   jax.ShapeDtypeStruct((N_EXPERTS, LANES), _F32)],
        scratch_shapes=[pltpu.VMEM((N_EXPERTS, LANES), _F32)],
        compiler_params=pltpu.CompilerParams(
            dimension_semantics=("arbitrary",), vmem_limit_bytes=VMEM_LIMIT),
        name="router",
    )(x, wr_t, tri)


def _move_tile(n):
    tm = min(MOVE_TILE, n)
    assert n % tm == 0 and tm % SUBLANES == 0
    return tm


def _expert_kernel(te_ref, na_ref, xs_ref, w1_ref, w3_ref, w2_ref, ys_ref):
    del te_ref
    active = pl.program_id(0) < na_ref[0]

    @pl.when(active)
    def _():
        ys_ref[...] = _swiglu_tile(xs_ref[...].astype(_BF), w1_ref, w3_ref, w2_ref)

    @pl.when(jnp.logical_not(active))
    def _():
        ys_ref[...] = jnp.zeros_like(ys_ref)


def _expert_ffn(xs, layer, w1, w3, w2, tile_expert, n_active):
    d = xs.shape[1]
    tm = FFN_TILE
    d_ff = w1.shape[3]
    n_tiles = tile_expert.shape[0]
    n_slots = n_tiles * tm
    w_in_spec = pl.BlockSpec((None, None, d, d_ff), lambda i, te, na: (layer, te[i], 0, 0),
                             pipeline_mode=pl.Buffered(1))
    w_out_spec = pl.BlockSpec((None, None, d_ff, d), lambda i, te, na: (layer, te[i], 0, 0),
                              pipeline_mode=pl.Buffered(1))
    return pl.pallas_call(
        _expert_kernel,
        grid_spec=pltpu.PrefetchScalarGridSpec(
            num_scalar_prefetch=2,
            grid=(n_tiles,),
            in_specs=[pl.BlockSpec((tm, d), lambda i, te, na: (jnp.minimum(i, na[0] - 1), 0)),
                      w_in_spec, w_in_spec, w_out_spec],
            out_specs=pl.BlockSpec((tm, d), lambda i, te, na: (i, 0)),
        ),
        out_shape=jax.ShapeDtypeStruct((n_slots, d), _F32),
        compiler_params=pltpu.CompilerParams(
            dimension_semantics=("arbitrary",), vmem_limit_bytes=VMEM_LIMIT),
        name="expert_ffn",
    )(tile_expert, n_active, xs, w1, w3, w2)


def _sc_gather(table, idx):
    m, d = idx.shape[0], table.shape[1]
    workers = SC_CORES * SC_SUBCORES
    per_worker = m // workers
    pairs = per_worker // (2 * SC_ROWS)
    assert m % (workers * 2 * SC_ROWS) == 0

    def body(table_hbm, idx_hbm, out_hbm, idx_a, idx_b, rows_a, rows_b, gsem_a, gsem_b, wsem_a, wsem_b):
        base = (lax.axis_index("s") * SC_CORES + lax.axis_index("c")) * per_worker
        buf_a = (idx_a, rows_a, gsem_a, wsem_a)
        buf_b = (idx_b, rows_b, gsem_b, wsem_b)

        def out_rows(c):
            return out_hbm.at[pl.ds(pl.multiple_of(base + c * SC_ROWS, SC_ROWS), SC_ROWS)]

        def start_gather(c, buf):
            idx_v, rows_v, gsem, _ = buf
            off = pl.multiple_of(base + c * SC_ROWS, SC_ROWS)
            pltpu.sync_copy(idx_hbm.at[pl.ds(off, SC_ROWS)], idx_v)
            pltpu.async_copy(table_hbm.at[idx_v], rows_v, gsem)

        def start_write(c, buf):
            idx_v, rows_v, gsem, wsem = buf
            pltpu.make_async_copy(table_hbm.at[idx_v], rows_v, gsem).wait()
            pltpu.async_copy(rows_v, out_rows(c), wsem)

        def wait_write(c, buf):
            _, rows_v, _, wsem = buf
            pltpu.make_async_copy(rows_v, out_rows(c), wsem).wait()

        start_gather(0, buf_a)

        @pl.loop(0, pairs)
        def _(p):
            c = 2 * p

            @pl.when(p > 0)
            def _():
                wait_write(c - 1, buf_b)

            start_gather(c + 1, buf_b)
            start_write(c, buf_a)
            wait_write(c, buf_a)

            @pl.when(p < pairs - 1)
            def _():
                start_gather(c + 2, buf_a)

            start_write(c + 1, buf_b)

        wait_write(2 * pairs - 1, buf_b)

    rows = pltpu.VMEM((SC_ROWS, d), table.dtype)
    index = pltpu.VMEM((SC_ROWS,), jnp.int32)
    return pl.kernel(
        body, out_type=jax.ShapeDtypeStruct((m, d), table.dtype), mesh=_sc_mesh(),
        scratch_types=[index, index, rows, rows] + [pltpu.SemaphoreType.DMA] * 4,
        name="sc_gather")(table, idx)


def _sc_mesh():
    return plsc.VectorSubcoreMesh(core_axis_name="c", subcore_axis_name="s",
                                  num_cores=SC_CORES, num_subcores=SC_SUBCORES)


def _sc_scatter(x, pos, pad_slots, n_slots):
    n, d = x.shape
    workers = SC_CORES * SC_SUBCORES
    per_worker = n // workers
    pad_per_worker = pad_slots.shape[0] // workers
    assert n % (workers * SC_ROWS) == 0 and pad_slots.shape[0] % (workers * SC_ROWS) == 0
    assert pos.shape[0] == 2 * n and n_slots == 2 * n + pad_slots.shape[0]

    pairs = per_worker // (2 * SC_ROWS)
    assert per_worker % (2 * SC_ROWS) == 0

    def body(x_hbm, pos_hbm, pad_hbm, out_hbm, i0_a, i1_a, i0_b, i1_b, rows_a, rows_b,
             lsem_a, lsem_b, ssem_a, ssem_b):
        wid = lax.axis_index("s") * SC_CORES + lax.axis_index("c")
        buf_a = (i0_a, i1_a, rows_a, lsem_a, ssem_a)
        buf_b = (i0_b, i1_b, rows_b, lsem_b, ssem_b)

        def x_rows(c):
            return x_hbm.at[pl.ds(pl.multiple_of(wid * per_worker + c * SC_ROWS, SC_ROWS), SC_ROWS)]

        def start_load(c, buf):
            i0, i1, rows_v, lsem, _ = buf
            off = pl.multiple_of(wid * per_worker + c * SC_ROWS, SC_ROWS)
            pltpu.async_copy(x_rows(c), rows_v, lsem)
            pltpu.sync_copy(pos_hbm.at[pl.ds(off, SC_ROWS)], i0)
            pltpu.sync_copy(pos_hbm.at[pl.ds(n + off, SC_ROWS)], i1)

        def start_scatter(c, buf):
            i0, i1, rows_v, lsem, ssem = buf
            pltpu.make_async_copy(x_rows(c), rows_v, lsem).wait()
            pltpu.async_copy(rows_v, out_hbm.at[i0], ssem)
            pltpu.async_copy(rows_v, out_hbm.at[i1], ssem)

        def wait_scatter(buf):
            i0, i1, rows_v, _, ssem = buf
            pltpu.make_async_copy(rows_v, out_hbm.at[i0], ssem).wait()
            pltpu.make_async_copy(rows_v, out_hbm.at[i1], ssem).wait()

        start_load(0, buf_a)

        @pl.loop(0, pairs)
        def _(p):
            c = 2 * p

            @pl.when(p > 0)
            def _():
                wait_scatter(buf_b)

            start_load(c + 1, buf_b)
            start_scatter(c, buf_a)
            wait_scatter(buf_a)

            @pl.when(p < pairs - 1)
            def _():
                start_load(c + 2, buf_a)

            start_scatter(c + 1, buf_b)

        wait_scatter(buf_b)

        @pl.loop(0, pad_per_worker // SC_ROWS)
        def _(c):
            off = pl.multiple_of(wid * pad_per_worker + c * SC_ROWS, SC_ROWS)
            pltpu.sync_copy(pad_hbm.at[pl.ds(off, SC_ROWS)], i0_a)
            pltpu.sync_copy(rows_b, out_hbm.at[i0_a])

    rows = pltpu.VMEM((SC_ROWS, d), x.dtype)
    index = pltpu.VMEM((SC_ROWS,), jnp.int32)
    return pl.kernel(
        body, out_type=jax.ShapeDtypeStruct((n_slots, d), x.dtype), mesh=_sc_mesh(),
        scratch_types=[index] * 4 + [rows, rows] + [pltpu.SemaphoreType.DMA] * 4,
        name="sc_scatter")(x, pos, pad_slots)


def _combine_kernel(alpha, n_groups, *refs):
    lg_ref, lb_ref, o_ref = refs[-3:]
    steps = pl.num_programs(0) // n_groups
    i = pl.program_id(0)
    for g in range(n_groups):
        x_ref, y0_ref, y1_ref, gate_ref = refs[4 * g:4 * g + 4]

        @pl.when((i >= g * steps) & (i < (g + 1) * steps))
        def _(x_ref=x_ref, y0_ref=y0_ref, y1_ref=y1_ref, gate_ref=gate_ref):
            o_ref[...] = _moe_residual(alpha, x_ref[...], y0_ref[...], y1_ref[...], gate_ref[...],
                                       lg_ref[...], lb_ref[...])


def _combine(parts, lg, lb, alpha):
    n, d = parts[0][0].shape
    tm = _move_tile(n)
    steps = n // tm
    in_specs, args = [], []
    for g, (x, y01, gates) in enumerate(parts):
        row = lambda i, g=g: jnp.clip(i - g * steps, 0, steps - 1)
        in_specs += [pl.BlockSpec((tm, d), lambda i, row=row: (row(i), 0)),
                     pl.BlockSpec((tm, d), lambda i, row=row: (row(i), 0)),
                     pl.BlockSpec((tm, d), lambda i, row=row: (row(i) + steps, 0)),
                     pl.BlockSpec((tm, 2), lambda i, row=row: (row(i), 0))]
        args += [x, y01, y01, gates]
    return pl.pallas_call(
        functools.partial(_combine_kernel, alpha, len(parts)),
        grid=(len(parts) * steps,),
        in_specs=in_specs + [_const_spec((1, d)), _const_spec((1, d))],
        out_specs=pl.BlockSpec((tm, d), lambda i: (i, 0)),
        out_shape=jax.ShapeDtypeStruct((len(parts) * n, d), _F32),
        compiler_params=pltpu.CompilerParams(
            dimension_semantics=("arbitrary",), vmem_limit_bytes=VMEM_LIMIT),
        name="combine",
    )(*args, lg.reshape(1, -1), lb.reshape(1, -1))


def _moe_experts(x, router, layer, w1, w3, w2):
    n, d = x.shape
    tm = FFN_TILE
    n_tiles = -(-(2 * n) // tm) + N_EXPERTS
    n_slots = n_tiles * tm

    ri, rf, cnt = _router(x, router.T.astype(_BF))
    counts = cnt[:, 0].astype(jnp.int32)
    padded = (counts + tm - 1) // tm * tm
    group_end = jnp.cumsum(padded)
    group_start = group_end - padded
    experts = jnp.arange(N_EXPERTS, dtype=jnp.int32)[:, None]

    def slot(e, rank):
        return jnp.sum(jnp.where(e[None, :] == experts, group_start[:, None], 0), axis=0) + rank

    pos = jnp.stack([slot(ri[0], ri[2]), slot(ri[1], ri[3])])
    tile_start = jnp.arange(n_tiles, dtype=jnp.int32) * tm
    tile_expert = jnp.minimum(
        jnp.sum(tile_start[:, None] >= group_end[None, :], axis=1), N_EXPERTS - 1).astype(jnp.int32)
    n_active = (group_end[-1:] // tm).astype(jnp.int32)

    pad_len = jnp.concatenate([padded - counts, n_slots - group_end[-1:]])
    pad_begin = jnp.concatenate([group_start + counts, group_end[-1:]])
    pad_end = jnp.cumsum(pad_len)
    k = jnp.arange(n_slots - 2 * n, dtype=jnp.int32)
    run = jnp.sum(k[:, None] >= pad_end[None, :], axis=1)
    pad_slots = k + jnp.sum(
        jnp.where(run[:, None] == jnp.arange(N_EXPERTS + 1)[None, :],
                  (pad_begin - (pad_end - pad_len))[None, :], 0), axis=1)
    xs = _sc_scatter(x, pos.reshape(-1), pad_slots.astype(jnp.int32), n_slots)
    ys = _expert_ffn(xs, layer, w1, w3, w2, tile_expert, n_active)
    return x, _sc_gather(ys, pos.reshape(-1)), rf[:2].T


def kernel(x, w_in, w_out, conv_w, conv_b, conv_ln_g, conv_ln_b, ret_gn_g, ln1_g, ln1_b, ln2_g,
           ln2_b, dense_w1, dense_w3, dense_w2, moe_router, moe_w1, moe_w3, moe_w2):
    b, s, d = x.shape
    depth = w_in.shape[0]
    alpha = (2.0 * depth) ** 0.25
    tables = _retention_tables(s)
    ng = BATCH_GROUPS if b % BATCH_GROUPS == 0 else 1
    bg = b // ng
    groups = [(x, g * bg, None) for g in range(ng)]
    mix_w = (w_in.astype(_BF), w_out.astype(_BF))
    dense_w = (dense_w1.astype(_BF), dense_w3.astype(_BF), dense_w2.astype(_BF))
    moe_f32 = (moe_w1, moe_w3, moe_w2)
    moe_w = {}
    for l in range(depth):
        i = l // 2
        for g, (xg, batch0, moe_part) in enumerate(groups):
            h = _mixer(xg, batch0, bg, l, *mix_w, conv_w[l], conv_b[l], conv_ln_g[l], conv_ln_b[l],
                       ret_gn_g[l], ln1_g[l], ln1_b[l], tables, alpha, moe_part).reshape(bg * s, d)
            if l % 2 == 0:
                cast = moe_f32 if g == 0 and l + 1 < depth else ()
                h, converted = _dense_ffn(h, i, *dense_w, ln2_g[l], ln2_b[l], alpha, cast, i)
                if cast:
                    moe_w[i] = converted
                groups[g] = (h.reshape(bg, s, d), 0, None)
            else:
                if i not in moe_w:
                    moe_w[i] = [a[i:i + 1].astype(_BF) for a in moe_f32]
                x_in, y01, gates = _moe_experts(h, moe_router[i], 0, *moe_w[i])
                groups[g] = (x_in.reshape(bg, s, d), 0, (y01, gates, ln2_g[l], ln2_b[l]))
    if groups[0][2] is None:
        return jnp.concatenate([xg for xg, _, _ in groups], axis=0)
    parts = [(xg.reshape(bg * s, d), part[0], part[1]) for xg, _, part in groups]
    return _combine(parts, ln2_g[-1], ln2_b[-1], alpha).reshape(b, s, d)
```

```python
import functools

import jax
import jax.numpy as jnp
from jax import lax
from jax.experimental import pallas as pl
from jax.experimental.pallas import tpu as pltpu
from jax.experimental.pallas import tpu_sc as plsc

CHUNK = 64
CONV_CH = 512
CONV_K = 31
RET_HEADS = 4
RET_DK = 64
RET_DV = 128
RET_QK_W = RET_HEADS * RET_DK
RET_V_W = RET_HEADS * RET_DV
ROPE_BASE = 10000.0
N_EXPERTS = 8
LN_EPS = 1e-5

LANES = 128
SUBLANES = 8
SEQ_TILE = 512
RET_BLOCK = 128
CONV_ROWS = 32
HALO = 32
FFN_TILE = 512
FFN_CHUNK = 512
ROUTE_TILE = 1024
MOVE_TILE = 512
SC_CORES = 2
SC_SUBCORES = 16
SC_ROWS = 32
BATCH_GROUPS = 2
VMEM_LIMIT = 56 * 1024 * 1024

_BF = jnp.bfloat16
_F32 = jnp.float32


def _dot(a, b):
    return jnp.dot(a, b, preferred_element_type=_F32)


def _layer_norm(v, g, b):
    mu = jnp.mean(v, axis=-1, keepdims=True)
    d = v - mu
    var = jnp.mean(d * d, axis=-1, keepdims=True)
    return d * lax.rsqrt(var + LN_EPS) * g + b


def _sigmoid(v):
    return 1.0 / (1.0 + jnp.exp(-v))


def _const_spec(shape):
    nd = len(shape)
    return pl.BlockSpec(shape, lambda *_: (0,) * nd, pipeline_mode=pl.Buffered(1))


def _layer_spec(stacked, layer):
    rest = stacked.shape[1:]
    return pl.BlockSpec((None,) + rest, lambda *_: (layer,) + (0,) * len(rest),
                        pipeline_mode=pl.Buffered(1))


def _moe_residual(alpha, x, y0, y1, gate, g, b):
    return _layer_norm(alpha * x + (y0 * gate[:, 0:1] + y1 * gate[:, 1:2]), g, b)


def _mixer_kernel(alpha, after_moe, x_ref, *refs):
    if after_moe:
        y0_ref, y1_ref, gate_ref, pg_ref, pb_ref, *refs = refs
    (w_in_ref, w_out_ref, conv_w_ref, conv_b_ref, cg_ref, cb_ref, gn_ref, lg_ref, lb_ref, cos_ref,
     sin_ref, dmask_ref, qdec_ref, kdec_ref, sdec_ref, bd_ref, o_ref, ush, state, rbuf) = refs
    t = x_ref.shape[0]
    ubuf = ush.at[0]

    @pl.when(pl.program_id(1) == 0)
    def _():
        ubuf[0:HALO, :] = jnp.zeros((HALO, CONV_CH), _F32)
        state[...] = jnp.zeros_like(state)

    x = x_ref[...]
    if after_moe:
        x = _moe_residual(alpha, x, y0_ref[...], y1_ref[...], gate_ref[...], pg_ref[...], pb_ref[...])
    xb = x.astype(_BF)

    c0 = 2 * CONV_CH
    ab = _dot(xb, w_in_ref[:, 0:c0])
    qk = _dot(xb, w_in_ref[:, c0:c0 + 2 * RET_QK_W])
    v = _dot(xb, w_in_ref[:, c0 + 2 * RET_QK_W:c0 + 2 * RET_QK_W + RET_V_W])
    g = _dot(xb, w_in_ref[:, c0 + 2 * RET_QK_W + RET_V_W:])

    ubuf[HALO:HALO + t, :] = ab[:, :CONV_CH] * _sigmoid(ab[:, CONV_CH:])
    span = t + HALO - SUBLANES
    for r in range(1, SUBLANES):
        ush[r, 0:span, :] = ubuf[r:r + span, :]
    off = HALO - (CONV_K - 1)
    conv_b = conv_b_ref[...]
    blocks = []
    for r0 in range(0, t, CONV_ROWS):
        acc = jnp.broadcast_to(conv_b, (CONV_ROWS, CONV_CH))
        for r in range(SUBLANES):
            taps = [j for j in range(CONV_K) if (j + off) % SUBLANES == r]
            reach = max((j + off) // SUBLANES for j in taps) * SUBLANES
            seg = ush[r, r0:r0 + reach + CONV_ROWS, :]
            for j in taps:
                a = (j + off) // SUBLANES * SUBLANES
                acc = acc + conv_w_ref[j:j + 1, :] * seg[a:a + CONV_ROWS]
        blocks.append(acc)
    conv = jnp.concatenate(blocks, axis=0)
    ubuf[0:HALO, :] = ubuf[t:t + HALO, :]
    un = _layer_norm(conv, cg_ref[...], cb_ref[...])
    u_out = un * _sigmoid(un)

    cos_t = cos_ref[...]
    sin_t = sin_ref[...]
    lane = lax.broadcasted_iota(jnp.int32, (1, LANES), 1)
    first_half = (lane % RET_DK) < (RET_DK // 2)

    def rope(z):
        parts = []
        for c in range(0, RET_QK_W, LANES):
            zc = z[:, c:c + LANES]
            up = pltpu.roll(zc, LANES - RET_DK // 2, axis=1)
            dn = pltpu.roll(zc, RET_DK // 2, axis=1)
            parts.append(jnp.where(first_half, up, dn))
        return z * cos_t + jnp.concatenate(parts, axis=1) * sin_t

    q = rope(qk[:, :RET_QK_W])
    k = rope(qk[:, RET_QK_W:])

    lane_qk = lax.broadcasted_iota(jnp.int32, (1, RET_QK_W), 1)
    dmask = dmask_ref[...]
    qdec = qdec_ref[...]
    kdec = kdec_ref[...]
    nr = RET_BLOCK
    for s in range(t // nr):
        rows = slice(s * nr, (s + 1) * nr)
        q_s, k_s, v_s = q[rows], k[rows], v[rows]
        v_b = v_s.astype(_BF)
        kbd = jnp.concatenate(
            [jnp.where(lane_qk // RET_DK == h, k_s, 0.0).astype(_BF) for h in range(RET_HEADS)],
            axis=0)
        sc = lax.dot_general(q_s.astype(_BF), kbd, (((1,), (1,)), ((), ())),
                             preferred_element_type=_F32)
        p = (sc * dmask).astype(_BF)
        intra = jnp.concatenate(
            [_dot(p[:, h * nr:(h + 1) * nr], v_b[:, h * RET_DV:(h + 1) * RET_DV])
             for h in range(RET_HEADS)], axis=1)
        st = state[...]
        cross = _dot((q_s * qdec).astype(_BF), st.astype(_BF))
        kv = lax.dot_general((k_s * kdec).astype(_BF), v_b, (((0,), (0,)), ((), ())),
                             preferred_element_type=_F32)
        state[...] = st * sdec_ref[...] + kv * bd_ref[...]
        rbuf[rows, :] = intra + cross

    r = rbuf[...]
    gate = g * _sigmoid(g)
    gn = gn_ref[...]
    r_parts = []
    for h in range(RET_HEADS):
        hs = slice(h * RET_DV, (h + 1) * RET_DV)
        rh = r[:, hs]
        mu = jnp.mean(rh, axis=-1, keepdims=True)
        d = rh - mu
        var = jnp.mean(d * d, axis=-1, keepdims=True)
        r_parts.append(gate[:, hs] * (d * lax.rsqrt(var + LN_EPS) * gn[:, hs]))

    mixed = jnp.concatenate([u_out] + r_parts, axis=1).astype(_BF)
    mix = _dot(mixed, w_out_ref[...])
    o_ref[...] = _layer_norm(alpha * x + mix, lg_ref[...], lb_ref[...])


def _retention_tables(seq):
    half = RET_DK // 2
    inv_freq = ROPE_BASE ** (-jnp.arange(half, dtype=_F32) / half)
    ang = jnp.arange(seq, dtype=jnp.int32).astype(_F32)[:, None] * inv_freq[None, :]
    cos, sin = jnp.cos(ang), jnp.sin(ang)
    cos_t = jnp.tile(jnp.concatenate([cos, cos], axis=1), (1, RET_HEADS))
    sin_t = jnp.tile(jnp.concatenate([-sin, sin], axis=1), (1, RET_HEADS))

    nr = RET_BLOCK
    log_gamma = jnp.log1p(-(2.0 ** (-5.0 - jnp.arange(RET_HEADS, dtype=_F32))))
    idx = jnp.arange(nr, dtype=_F32)
    dist = jnp.abs(idx[:, None] - idx[None, :])
    visible = (jnp.arange(nr)[None, :] // CHUNK) <= (jnp.arange(nr)[:, None] // CHUNK)
    scale = RET_DK ** -0.5
    dm = jnp.exp(log_gamma[:, None, None] * dist[None]) * visible[None] * scale
    dmask = jnp.transpose(dm, (1, 0, 2)).reshape(nr, RET_HEADS * nr)
    qdec = jnp.repeat(jnp.exp(log_gamma[None, :] * (idx + 1.0)[:, None]), RET_DK, axis=1)
    kdec = jnp.repeat(jnp.exp(log_gamma[None, :] * (nr - 1.0 - idx)[:, None]), RET_DK, axis=1) * scale
    sdec = jnp.broadcast_to(jnp.repeat(jnp.exp(log_gamma * nr), RET_DK)[:, None],
                            (RET_QK_W, RET_V_W))
    bd = (jnp.arange(RET_QK_W)[:, None] // RET_DK == jnp.arange(RET_V_W)[None, :] // RET_DV)
    return cos_t, sin_t, dmask, qdec, kdec, sdec, bd.astype(_F32)


def _mixer(x, batch0, nb, layer, w_in, w_out, conv_w, conv_b, cg, cb, gn, lg, lb, tables, alpha,
           moe_part=None):
    _, s, d = x.shape
    t = min(SEQ_TILE, s)
    assert s % t == 0 and t % RET_BLOCK == 0 and RET_BLOCK % CHUNK == 0 and t % CONV_ROWS == 0
    cos_t, sin_t, dmask, qdec, kdec, sdec, bd = tables
    row = lambda a: a.reshape(1, -1)
    conv_w = jnp.pad(conv_w, ((0, HALO - CONV_K), (0, 0)))
    consts = [conv_w, row(conv_b), row(cg), row(cb), row(gn), row(lg), row(lb)]
    tail = [dmask, qdec, kdec, sdec, bd]
    pre, pre_specs = [], []
    if moe_part is not None:
        assert batch0 == 0 and x.shape[0] == nb
        y01, gates, pg, pb = moe_part
        y01 = y01.reshape(2, nb, s, d)
        pre = [y01, y01, gates.reshape(nb, s, 2), row(pg), row(pb)]
        pre_specs = [pl.BlockSpec((None, None, t, d), lambda i, j: (0, i, j, 0)),
                     pl.BlockSpec((None, None, t, d), lambda i, j: (1, i, j, 0)),
                     pl.BlockSpec((None, t, 2), lambda i, j: (i, j, 0)),
                     _const_spec((1, d)), _const_spec((1, d))]
    in_specs = ([pl.BlockSpec((None, t, d), lambda i, j: (i + batch0, j, 0))] + pre_specs
                + [_layer_spec(w_in, layer), _layer_spec(w_out, layer)]
                + [_const_spec(a.shape) for a in consts]
                + [pl.BlockSpec((t, RET_QK_W), lambda i, j: (j, 0))] * 2
                + [_const_spec(a.shape) for a in tail])
    return pl.pallas_call(
        functools.partial(_mixer_kernel, alpha, moe_part is not None),
        grid=(nb, s // t),
        in_specs=in_specs,
        out_specs=pl.BlockSpec((None, t, d), lambda i, j: (i, j, 0)),
        out_shape=jax.ShapeDtypeStruct((nb, s, d), _F32),
        scratch_shapes=[pltpu.VMEM((SUBLANES, t + HALO, CONV_CH), _F32),
                        pltpu.VMEM((RET_QK_W, RET_V_W), _F32),
                        pltpu.VMEM((t, RET_V_W), _F32)],
        compiler_params=pltpu.CompilerParams(
            dimension_semantics=("arbitrary", "arbitrary"), vmem_limit_bytes=VMEM_LIMIT),
        name="mixer",
    )(x, *pre, w_in, w_out, *consts, cos_t, sin_t, *tail)


def _swiglu_tile(xb, w1_ref, w3_ref, w2_ref):
    d_ff = w1_ref.shape[1]
    acc = None
    for c in range(0, d_ff, FFN_CHUNK):
        a = _dot(xb, w1_ref[:, c:c + FFN_CHUNK])
        g = _dot(xb, w3_ref[:, c:c + FFN_CHUNK])
        h = (a * _sigmoid(a) * g).astype(_BF)
        part = _dot(h, w2_ref[c:c + FFN_CHUNK, :])
        acc = part if acc is None else acc + part
    return acc


def _dense_ffn_kernel(alpha, n_cast, x_ref, w1_ref, w3_ref, w2_ref, lg_ref, lb_ref, *refs):
    cast_in, (o_ref, *cast_out) = refs[:n_cast], refs[n_cast:]
    x = x_ref[...]
    f = _swiglu_tile(x.astype(_BF), w1_ref, w3_ref, w2_ref)
    o_ref[...] = _layer_norm(alpha * x + f, lg_ref[...], lb_ref[...])
    for src, dst in zip(cast_in, cast_out):
        dst[...] = src[...].astype(_BF)


def _dense_ffn(x, layer, w1, w3, w2, lg, lb, alpha, cast=(), cast_layer=0):
    n, d = x.shape
    tm = min(FFN_TILE, n)
    steps = n // tm
    assert n % tm == 0 and w1.shape[2] % FFN_CHUNK == 0
    weights = [w1, w3, w2]
    norm = [lg.reshape(1, -1), lb.reshape(1, -1)]
    cast_in, cast_specs, cast_shapes = [], [], []
    for a in cast:
        cols = a.shape[-1]
        rows = a[0].size // cols
        assert rows % (steps * 2 * SUBLANES) == 0
        cast_in.append(a.reshape(-1, cols))
        cast_specs.append(pl.BlockSpec((rows // steps, cols),
                                       lambda i, base=cast_layer * steps: (base + i, 0)))
        cast_shapes.append(jax.ShapeDtypeStruct((rows, cols), _BF))
    row_spec = pl.BlockSpec((tm, d), lambda i: (i, 0))
    out = pl.pallas_call(
        functools.partial(_dense_ffn_kernel, alpha, len(cast)),
        grid=(steps,),
        in_specs=([row_spec] + [_layer_spec(w, layer) for w in weights]
                  + [_const_spec(a.shape) for a in norm] + cast_specs),
        out_specs=[row_spec] + [pl.BlockSpec(s.block_shape, lambda i: (i, 0)) for s in cast_specs],
        out_shape=[jax.ShapeDtypeStruct((n, d), _F32)] + cast_shapes,
        compiler_params=pltpu.CompilerParams(
            dimension_semantics=("arbitrary",), vmem_limit_bytes=VMEM_LIMIT),
        name="dense_ffn",
    )(x, *weights, *norm, *cast_in)
    return out[0], [o.reshape((1,) + a.shape[1:]) for o, a in zip(out[1:], cast)]


def _router_kernel(x_ref, wr_ref, tri_ref, ri_ref, rf_ref, cnt_ref, carry):
    @pl.when(pl.program_id(0) == 0)
    def _():
        carry[...] = jnp.zeros_like(carry)

    tr = x_ref.shape[0]
    logits = lax.dot_general(wr_ref[...], x_ref[...].astype(_BF), (((1,), (1,)), ((), ())),
                             preferred_element_type=_F32)
    eidx = lax.broadcasted_iota(jnp.int32, (N_EXPERTS, tr), 0)
    m0 = jnp.max(logits, axis=0, keepdims=True)
    e0 = jnp.min(jnp.where(logits == m0, eidx, N_EXPERTS), axis=0, keepdims=True)
    rest = jnp.where(eidx == e0, -jnp.inf, logits)
    m1 = jnp.max(rest, axis=0, keepdims=True)
    e1 = jnp.min(jnp.where(rest == m1, eidx, N_EXPERTS), axis=0, keepdims=True)
    tt = jnp.exp(m1 - m0)
    g0 = 1.0 / (1.0 + tt)
    g1 = tt / (1.0 + tt)

    oh0 = eidx == e0
    oh1 = eidx == e1
    member = jnp.where(oh0 | oh1, 1.0, 0.0)
    before = _dot(member.astype(_BF), tri_ref[...]) + carry[:, 0:1]
    rank0 = jnp.sum(jnp.where(oh0, before, 0.0), axis=0, keepdims=True)
    rank1 = jnp.sum(jnp.where(oh1, before, 0.0), axis=0, keepdims=True)
    carry[...] = carry[...] + jnp.sum(member, axis=1, keepdims=True)

    zi = jnp.zeros((N_EXPERTS - 4, tr), jnp.int32)
    ri_ref[...] = jnp.concatenate(
        [e0, e1, rank0.astype(jnp.int32), rank1.astype(jnp.int32), zi], axis=0)
    rf_ref[...] = jnp.concatenate([g0, g1, jnp.zeros((N_EXPERTS - 2, tr), _F32)], axis=0)
    cnt_ref[...] = carry[...]


def _router(x, wr_t):
    n, d = x.shape
    tr = min(ROUTE_TILE, n)
    assert n % tr == 0
    tri = (jnp.arange(tr)[:, None] < jnp.arange(tr)[None, :]).astype(_BF)
    return pl.pallas_call(
        _router_kernel,
        grid=(n // tr,),
        in_specs=[pl.BlockSpec((tr, d), lambda i: (i, 0)),
                  _const_spec(wr_t.shape), _const_spec(tri.shape)],
        out_specs=[pl.BlockSpec((N_EXPERTS, tr), lambda i: (0, i)),
                   pl.BlockSpec((N_EXPERTS, tr), lambda i: (0, i)),
                   pl.BlockSpec((N_EXPERTS, LANES), lambda i: (0, 0))],
        out_shape=[jax.ShapeDtypeStruct((N_EXPERTS, n), jnp.int32),
                   jax.ShapeDtypeStruct((N_EXPERTS, n), _F32),
                   jax.ShapeDtypeStruct((N_EXPERTS, LANES), _F32)],
        scratch_shapes=[pltpu.VMEM((N_EXPERTS, LANES), _F32)],
        compiler_params=pltpu.CompilerParams(
            dimension_semantics=("arbitrary",), vmem_limit_bytes=VMEM_LIMIT),
        name="router",
    )(x, wr_t, tri)


def _move_tile(n):
    tm = min(MOVE_TILE, n)
    assert n % tm == 0 and tm % SUBLANES == 0
    return tm


def _expert_kernel(te_ref, na_ref, xs_ref, w1_ref, w3_ref, w2_ref, ys_ref):
    del te_ref
    active = pl.program_id(0) < na_ref[0]

    @pl.when(active)
    def _():
        ys_ref[...] = _swiglu_tile(xs_ref[...].astype(_BF), w1_ref, w3_ref, w2_ref)

    @pl.when(jnp.logical_not(active))
    def _():
        ys_ref[...] = jnp.zeros_like(ys_ref)


def _expert_ffn(xs, layer, w1, w3, w2, tile_expert, n_active):
    d = xs.shape[1]
    tm = FFN_TILE
    d_ff = w1.shape[3]
    n_tiles = tile_expert.shape[0]
    n_slots = n_tiles * tm
    expert_block = lambda i, te, na: (layer, te[i], 0, 0)
    w1_spec = pl.BlockSpec((None, None, d, d_ff), expert_block, pipeline_mode=pl.Buffered(1))
    w3_spec = pl.BlockSpec((None, None, d, d_ff), expert_block)
    w2_spec = pl.BlockSpec((None, None, d_ff, d), expert_block)
    return pl.pallas_call(
        _expert_kernel,
        grid_spec=pltpu.PrefetchScalarGridSpec(
            num_scalar_prefetch=2,
            grid=(n_tiles,),
            in_specs=[pl.BlockSpec((tm, d), lambda i, te, na: (jnp.minimum(i, na[0] - 1), 0)),
                      w1_spec, w3_spec, w2_spec],
            out_specs=pl.BlockSpec((tm, d), lambda i, te, na: (i, 0)),
        ),
        out_shape=jax.ShapeDtypeStruct((n_slots, d), _F32),
        compiler_params=pltpu.CompilerParams(
            dimension_semantics=("arbitrary",), vmem_limit_bytes=VMEM_LIMIT),
        name="expert_ffn",
    )(tile_expert, n_active, xs, w1, w3, w2)


def _sc_gather(table, idx):
    m, d = idx.shape[0], table.shape[1]
    workers = SC_CORES * SC_SUBCORES
    per_worker = m // workers
    pairs = per_worker // (2 * SC_ROWS)
    assert m % (workers * 2 * SC_ROWS) == 0

    def body(table_hbm, idx_hbm, out_hbm, idx_a, idx_b, rows_a, rows_b, gsem_a, gsem_b, wsem_a, wsem_b):
        base = (lax.axis_index("s") * SC_CORES + lax.axis_index("c")) * per_worker
        buf_a = (idx_a, rows_a, gsem_a, wsem_a)
        buf_b = (idx_b, rows_b, gsem_b, wsem_b)

        def out_rows(c):
            return out_hbm.at[pl.ds(pl.multiple_of(base + c * SC_ROWS, SC_ROWS), SC_ROWS)]

        def start_gather(c, buf):
            idx_v, rows_v, gsem, _ = buf
            off = pl.multiple_of(base + c * SC_ROWS, SC_ROWS)
            pltpu.sync_copy(idx_hbm.at[pl.ds(off, SC_ROWS)], idx_v)
            pltpu.async_copy(table_hbm.at[idx_v], rows_v, gsem)

        def start_write(c, buf):
            idx_v, rows_v, gsem, wsem = buf
            pltpu.make_async_copy(table_hbm.at[idx_v], rows_v, gsem).wait()
            pltpu.async_copy(rows_v, out_rows(c), wsem)

        def wait_write(c, buf):
            _, rows_v, _, wsem = buf
            pltpu.make_async_copy(rows_v, out_rows(c), wsem).wait()

        start_gather(0, buf_a)

        @pl.loop(0, pairs)
        def _(p):
            c = 2 * p

            @pl.when(p > 0)
            def _():
                wait_write(c - 1, buf_b)

            start_gather(c + 1, buf_b)
            start_write(c, buf_a)
            wait_write(c, buf_a)

            @pl.when(p < pairs - 1)
            def _():
                start_gather(c + 2, buf_a)

            start_write(c + 1, buf_b)

        wait_write(2 * pairs - 1, buf_b)

    rows = pltpu.VMEM((SC_ROWS, d), table.dtype)
    index = pltpu.VMEM((SC_ROWS,), jnp.int32)
    return pl.kernel(
        body, out_type=jax.ShapeDtypeStruct((m, d), table.dtype), mesh=_sc_mesh(),
        scratch_types=[index, index, rows, rows] + [pltpu.SemaphoreType.DMA] * 4,
        name="sc_gather")(table, idx)


def _sc_mesh():
    return plsc.VectorSubcoreMesh(core_axis_name="c", subcore_axis_name="s",
                                  num_cores=SC_CORES, num_subcores=SC_SUBCORES)


def _sc_scatter(x, pos, pad_slots, n_slots):
    n, d = x.shape
    workers = SC_CORES * SC_SUBCORES
    per_worker = n // workers
    pad_per_worker = pad_slots.shape[0] // workers
    assert n % (workers * SC_ROWS) == 0 and pad_slots.shape[0] % (workers * SC_ROWS) == 0
    assert pos.shape[0] == 2 * n and n_slots == 2 * n + pad_slots.shape[0]

    pairs = per_worker // (2 * SC_ROWS)
    assert per_worker % (2 * SC_ROWS) == 0

    def body(x_hbm, pos_hbm, pad_hbm, out_hbm, i0_a, i1_a, i0_b, i1_b, rows_a, rows_b,
             lsem_a, lsem_b, ssem_a, ssem_b):
        wid = lax.axis_index("s") * SC_CORES + lax.axis_index("c")
        buf_a = (i0_a, i1_a, rows_a, lsem_a, ssem_a)
        buf_b = (i0_b, i1_b, rows_b, lsem_b, ssem_b)

        def x_rows(c):
            return x_hbm.at[pl.ds(pl.multiple_of(wid * per_worker + c * SC_ROWS, SC_ROWS), SC_ROWS)]

        def start_load(c, buf):
            i0, i1, rows_v, lsem, _ = buf
            off = pl.multiple_of(wid * per_worker + c * SC_ROWS, SC_ROWS)
            pltpu.async_copy(x_rows(c), rows_v, lsem)
            pltpu.sync_copy(pos_hbm.at[pl.ds(off, SC_ROWS)], i0)
            pltpu.sync_copy(pos_hbm.at[pl.ds(n + off, SC_ROWS)], i1)

        def start_scatter(c, buf):
            i0, i1, rows_v, lsem, ssem = buf
            pltpu.make_async_copy(x_rows(c), rows_v, lsem).wait()
            pltpu.async_copy(rows_v, out_hbm.at[i0], ssem)
            pltpu.async_copy(rows_v, out_hbm.at[i1], ssem)

        def wait_scatter(buf):
            i0, i1, rows_v, _, ssem = buf
            pltpu.make_async_copy(rows_v, out_hbm.at[i0], ssem).wait()
            pltpu.make_async_copy(rows_v, out_hbm.at[i1], ssem).wait()

        start_load(0, buf_a)

        @pl.loop(0, pairs)
        def _(p):
            c = 2 * p

            @pl.when(p > 0)
            def _():
                wait_scatter(buf_b)

            start_load(c + 1, buf_b)
            start_scatter(c, buf_a)
            wait_scatter(buf_a)

            @pl.when(p < pairs - 1)
            def _():
                start_load(c + 2, buf_a)

            start_scatter(c + 1, buf_b)

        wait_scatter(buf_b)

        @pl.loop(0, pad_per_worker // SC_ROWS)
        def _(c):
            off = pl.multiple_of(wid * pad_per_worker + c * SC_ROWS, SC_ROWS)
            pltpu.sync_copy(pad_hbm.at[pl.ds(off, SC_ROWS)], i0_a)
            pltpu.sync_copy(rows_b, out_hbm.at[i0_a])

    rows = pltpu.VMEM((SC_ROWS, d), x.dtype)
    index = pltpu.VMEM((SC_ROWS,), jnp.int32)
    return pl.kernel(
        body, out_type=jax.ShapeDtypeStruct((n_slots, d), x.dtype), mesh=_sc_mesh(),
        scratch_types=[index] * 4 + [rows, rows] + [pltpu.SemaphoreType.DMA] * 4,
        name="sc_scatter")(x, pos, pad_slots)


def _combine_kernel(alpha, n_groups, *refs):
    lg_ref, lb_ref, o_ref = refs[-3:]
    steps = pl.num_programs(0) // n_groups
    i = pl.program_id(0)
    for g in range(n_groups):
        x_ref, y0_ref, y1_ref, gate_ref = refs[4 * g:4 * g + 4]

        @pl.when((i >= g * steps) & (i < (g + 1) * steps))
        def _(x_ref=x_ref, y0_ref=y0_ref, y1_ref=y1_ref, gate_ref=gate_ref):
            o_ref[...] = _moe_residual(alpha, x_ref[...], y0_ref[...], y1_ref[...], gate_ref[...],
                                       lg_ref[...], lb_ref[...])


def _combine(parts, lg, lb, alpha):
    n, d = parts[0][0].shape
    tm = _move_tile(n)
    steps = n // tm
    in_specs, args = [], []
    for g, (x, y01, gates) in enumerate(parts):
        row = lambda i, g=g: jnp.clip(i - g * steps, 0, steps - 1)
        in_specs += [pl.BlockSpec((tm, d), lambda i, row=row: (row(i), 0)),
                     pl.BlockSpec((tm, d), lambda i, row=row: (row(i), 0)),
                     pl.BlockSpec((tm, d), lambda i, row=row: (row(i) + steps, 0)),
                     pl.BlockSpec((tm, 2), lambda i, row=row: (row(i), 0))]
        args += [x, y01, y01, gates]
    return pl.pallas_call(
        functools.partial(_combine_kernel, alpha, len(parts)),
        grid=(len(parts) * steps,),
        in_specs=in_specs + [_const_spec((1, d)), _const_spec((1, d))],
        out_specs=pl.BlockSpec((tm, d), lambda i: (i, 0)),
        out_shape=jax.ShapeDtypeStruct((len(parts) * n, d), _F32),
        compiler_params=pltpu.CompilerParams(
            dimension_semantics=("arbitrary",), vmem_limit_bytes=VMEM_LIMIT),
        name="combine",
    )(*args, lg.reshape(1, -1), lb.reshape(1, -1))


def _moe_experts(x, router, layer, w1, w3, w2):
    n, d = x.shape
    tm = FFN_TILE
    n_tiles = -(-(2 * n) // tm) + N_EXPERTS
    n_slots = n_tiles * tm

    ri, rf, cnt = _router(x, router.T.astype(_BF))
    counts = cnt[:, 0].astype(jnp.int32)
    padded = (counts + tm - 1) // tm * tm
    group_end = jnp.cumsum(padded)
    group_start = group_end - padded
    experts = jnp.arange(N_EXPERTS, dtype=jnp.int32)[:, None]

    def slot(e, rank):
        return jnp.sum(jnp.where(e[None, :] == experts, group_start[:, None], 0), axis=0) + rank

    pos = jnp.stack([slot(ri[0], ri[2]), slot(ri[1], ri[3])])
    tile_start = jnp.arange(n_tiles, dtype=jnp.int32) * tm
    tile_expert = jnp.minimum(
        jnp.sum(tile_start[:, None] >= group_end[None, :], axis=1), N_EXPERTS - 1).astype(jnp.int32)
    n_active = (group_end[-1:] // tm).astype(jnp.int32)

    pad_len = jnp.concatenate([padded - counts, n_slots - group_end[-1:]])
    pad_begin = jnp.concatenate([group_start + counts, group_end[-1:]])
    pad_end = jnp.cumsum(pad_len)
    k = jnp.arange(n_slots - 2 * n, dtype=jnp.int32)
    run = jnp.sum(k[:, None] >= pad_end[None, :], axis=1)
    pad_slots = k + jnp.sum(
        jnp.where(run[:, None] == jnp.arange(N_EXPERTS + 1)[None, :],
                  (pad_begin - (pad_end - pad_len))[None, :], 0), axis=1)
    xs = _sc_scatter(x, pos.reshape(-1), pad_slots.astype(jnp.int32), n_slots)
    ys = _expert_ffn(xs, layer, w1, w3, w2, tile_expert, n_active)
    return x, _sc_gather(ys, pos.reshape(-1)), rf[:2].T


def kernel(x, w_in, w_out, conv_w, conv_b, conv_ln_g, conv_ln_b, ret_gn_g, ln1_g, ln1_b, ln2_g,
           ln2_b, dense_w1, dense_w3, dense_w2, moe_router, moe_w1, moe_w3, moe_w2):
    b, s, d = x.shape
    depth = w_in.shape[0]
    alpha = (2.0 * depth) ** 0.25
    tables = _retention_tables(s)
    ng = BATCH_GROUPS if b % BATCH_GROUPS == 0 else 1
    bg = b // ng
    groups = [(x, g * bg, None) for g in range(ng)]
    mix_w = (w_in.astype(_BF), w_out.astype(_BF))
    dense_w = (dense_w1.astype(_BF), dense_w3.astype(_BF), dense_w2.astype(_BF))
    moe_f32 = (moe_w1, moe_w3, moe_w2)
    moe_w = {}
    for l in range(depth):
        i = l // 2
        for g, (xg, batch0, moe_part) in enumerate(groups):
            h = _mixer(xg, batch0, bg, l, *mix_w, conv_w[l], conv_b[l], conv_ln_g[l], conv_ln_b[l],
                       ret_gn_g[l], ln1_g[l], ln1_b[l], tables, alpha, moe_part).reshape(bg * s, d)
            if l % 2 == 0:
                cast = moe_f32 if g == 0 and l + 1 < depth else ()
                h, converted = _dense_ffn(h, i, *dense_w, ln2_g[l], ln2_b[l], alpha, cast, i)
                if cast:
                    moe_w[i] = converted
                groups[g] = (h.reshape(bg, s, d), 0, None)
            else:
                if i not in moe_w:
                    moe_w[i] = [a[i:i + 1].astype(_BF) for a in moe_f32]
                x_in, y01, gates = _moe_experts(h, moe_router[i], 0, *moe_w[i])
                groups[g] = (x_in.reshape(bg, s, d), 0, (y01, gates, ln2_g[l], ln2_b[l]))
    if groups[0][2] is None:
        return jnp.concatenate([xg for xg, _, _ in groups], axis=0)
    parts = [(xg.reshape(bg * s, d), part[0], part[1]) for xg, _, part in groups]
    return _combine(parts, ln2_g[-1], ln2_b[-1], alpha).reshape(b, s, d)
```

```python
import functools

import jax
import jax.numpy as jnp
from jax import lax
from jax.experimental import pallas as pl
from jax.experimental.pallas import tpu as pltpu
from jax.experimental.pallas import tpu_sc as plsc

CHUNK = 64
CONV_CH = 512
CONV_K = 31
RET_HEADS = 4
RET_DK = 64
RET_DV = 128
RET_QK_W = RET_HEADS * RET_DK
RET_V_W = RET_HEADS * RET_DV
ROPE_BASE = 10000.0
N_EXPERTS = 8
LN_EPS = 1e-5

LANES = 128
SUBLANES = 8
SEQ_TILE = 512
RET_BLOCK = 128
CONV_ROWS = 32
HALO = 32
FFN_TILE = 512
FFN_CHUNK = 512
ROUTE_TILE = 1024
MOVE_TILE = 512
SC_CORES = 2
SC_SUBCORES = 16
SC_ROWS = 32
BATCH_GROUPS = 2
VMEM_LIMIT = 56 * 1024 * 1024

_BF = jnp.bfloat16
_F32 = jnp.float32


def _dot(a, b):
    return jnp.dot(a, b, preferred_element_type=_F32)


def _layer_norm(v, g, b):
    mu = jnp.mean(v, axis=-1, keepdims=True)
    d = v - mu
    var = jnp.mean(d * d, axis=-1, keepdims=True)
    return d * lax.rsqrt(var + LN_EPS) * g + b


def _sigmoid(v):
    return 1.0 / (1.0 + jnp.exp(-v))


def _const_spec(shape):
    nd = len(shape)
    return pl.BlockSpec(shape, lambda *_: (0,) * nd, pipeline_mode=pl.Buffered(1))


def _layer_spec(stacked, layer):
    rest = stacked.shape[1:]
    return pl.BlockSpec((None,) + rest, lambda *_: (layer,) + (0,) * len(rest),
                        pipeline_mode=pl.Buffered(1))


def _moe_residual(alpha, x, y0, y1, gate, g, b):
    return _layer_norm(alpha * x + (y0 * gate[:, 0:1] + y1 * gate[:, 1:2]), g, b)


def _mixer_kernel(alpha, after_moe, x_ref, *refs):
    if after_moe:
        y0_ref, y1_ref, gate_ref, pg_ref, pb_ref, *refs = refs
    (w_in_ref, w_out_ref, conv_w_ref, conv_b_ref, cg_ref, cb_ref, gn_ref, lg_ref, lb_ref, cos_ref,
     sin_ref, dmask_ref, qdec_ref, kdec_ref, sdec_ref, bd_ref, o_ref, ush, state, rbuf) = refs
    t = x_ref.shape[0]
    ubuf = ush.at[0]

    @pl.when(pl.program_id(1) == 0)
    def _():
        ubuf[0:HALO, :] = jnp.zeros((HALO, CONV_CH), _F32)
        state[...] = jnp.zeros_like(state)

    x = x_ref[...]
    if after_moe:
        x = _moe_residual(alpha, x, y0_ref[...], y1_ref[...], gate_ref[...], pg_ref[...], pb_ref[...])
    xb = x.astype(_BF)

    c0 = 2 * CONV_CH
    ab = _dot(xb, w_in_ref[:, 0:c0])
    qk = _dot(xb, w_in_ref[:, c0:c0 + 2 * RET_QK_W])
    v = _dot(xb, w_in_ref[:, c0 + 2 * RET_QK_W:c0 + 2 * RET_QK_W + RET_V_W])
    g = _dot(xb, w_in_ref[:, c0 + 2 * RET_QK_W + RET_V_W:])

    ubuf[HALO:HALO + t, :] = ab[:, :CONV_CH] * _sigmoid(ab[:, CONV_CH:])
    span = t + HALO - SUBLANES
    for r in range(1, SUBLANES):
        ush[r, 0:span, :] = ubuf[r:r + span, :]
    off = HALO - (CONV_K - 1)
    conv_b = conv_b_ref[...]
    blocks = []
    for r0 in range(0, t, CONV_ROWS):
        acc = jnp.broadcast_to(conv_b, (CONV_ROWS, CONV_CH))
        for r in range(SUBLANES):
            taps = [j for j in range(CONV_K) if (j + off) % SUBLANES == r]
            for j in taps:
                a = r0 + (j + off) // SUBLANES * SUBLANES
                acc = acc + conv_w_ref[j:j + 1, :] * ush[r, a:a + CONV_ROWS, :]
        blocks.append(acc)
    conv = jnp.concatenate(blocks, axis=0)
    ubuf[0:HALO, :] = ubuf[t:t + HALO, :]
    un = _layer_norm(conv, cg_ref[...], cb_ref[...])
    u_out = un * _sigmoid(un)

    cos_t = cos_ref[...]
    sin_t = sin_ref[...]
    lane = lax.broadcasted_iota(jnp.int32, (1, LANES), 1)
    first_half = (lane % RET_DK) < (RET_DK // 2)

    def rope(z):
        parts = []
        for c in range(0, RET_QK_W, LANES):
            zc = z[:, c:c + LANES]
            up = pltpu.roll(zc, LANES - RET_DK // 2, axis=1)
            dn = pltpu.roll(zc, RET_DK // 2, axis=1)
            parts.append(jnp.where(first_half, up, dn))
        return z * cos_t + jnp.concatenate(parts, axis=1) * sin_t

    q = rope(qk[:, :RET_QK_W])
    k = rope(qk[:, RET_QK_W:])

    lane_qk = lax.broadcasted_iota(jnp.int32, (1, RET_QK_W), 1)
    dmask = dmask_ref[...]
    qdec = qdec_ref[...]
    kdec = kdec_ref[...]
    nr = RET_BLOCK
    for s in range(t // nr):
        rows = slice(s * nr, (s + 1) * nr)
        q_s, k_s, v_s = q[rows], k[rows], v[rows]
        v_b = v_s.astype(_BF)
        kbd = jnp.concatenate(
            [jnp.where(lane_qk // RET_DK == h, k_s, 0.0).astype(_BF) for h in range(RET_HEADS)],
            axis=0)
        sc = lax.dot_general(q_s.astype(_BF), kbd, (((1,), (1,)), ((), ())),
                             preferred_element_type=_F32)
        p = (sc * dmask).astype(_BF)
        intra = jnp.concatenate(
            [_dot(p[:, h * nr:(h + 1) * nr], v_b[:, h * RET_DV:(h + 1) * RET_DV])
             for h in range(RET_HEADS)], axis=1)
        st = state[...]
        cross = _dot((q_s * qdec).astype(_BF), st.astype(_BF))
        kv = lax.dot_general((k_s * kdec).astype(_BF), v_b, (((0,), (0,)), ((), ())),
                             preferred_element_type=_F32)
        state[...] = st * sdec_ref[...] + kv * bd_ref[...]
        rbuf[rows, :] = intra + cross

    r = rbuf[...]
    gate = g * _sigmoid(g)
    gn = gn_ref[...]
    r_parts = []
    for h in range(RET_HEADS):
        hs = slice(h * RET_DV, (h + 1) * RET_DV)
        rh = r[:, hs]
        mu = jnp.mean(rh, axis=-1, keepdims=True)
        d = rh - mu
        var = jnp.mean(d * d, axis=-1, keepdims=True)
        r_parts.append(gate[:, hs] * (d * lax.rsqrt(var + LN_EPS) * gn[:, hs]))

    mixed = jnp.concatenate([u_out] + r_parts, axis=1).astype(_BF)
    mix = _dot(mixed, w_out_ref[...])
    o_ref[...] = _layer_norm(alpha * x + mix, lg_ref[...], lb_ref[...])


def _retention_tables(seq):
    half = RET_DK // 2
    inv_freq = ROPE_BASE ** (-jnp.arange(half, dtype=_F32) / half)
    ang = jnp.arange(seq, dtype=jnp.int32).astype(_F32)[:, None] * inv_freq[None, :]
    cos, sin = jnp.cos(ang), jnp.sin(ang)
    cos_t = jnp.tile(jnp.concatenate([cos, cos], axis=1), (1, RET_HEADS))
    sin_t = jnp.tile(jnp.concatenate([-sin, sin], axis=1), (1, RET_HEADS))

    nr = RET_BLOCK
    log_gamma = jnp.log1p(-(2.0 ** (-5.0 - jnp.arange(RET_HEADS, dtype=_F32))))
    idx = jnp.arange(nr, dtype=_F32)
    dist = jnp.abs(idx[:, None] - idx[None, :])
    visible = (jnp.arange(nr)[None, :] // CHUNK) <= (jnp.arange(nr)[:, None] // CHUNK)
    scale = RET_DK ** -0.5
    dm = jnp.exp(log_gamma[:, None, None] * dist[None]) * visible[None] * scale
    dmask = jnp.transpose(dm, (1, 0, 2)).reshape(nr, RET_HEADS * nr)
    qdec = jnp.repeat(jnp.exp(log_gamma[None, :] * (idx + 1.0)[:, None]), RET_DK, axis=1)
    kdec = jnp.repeat(jnp.exp(log_gamma[None, :] * (nr - 1.0 - idx)[:, None]), RET_DK, axis=1) * scale
    sdec = jnp.broadcast_to(jnp.repeat(jnp.exp(log_gamma * nr), RET_DK)[:, None],
                            (RET_QK_W, RET_V_W))
    bd = (jnp.arange(RET_QK_W)[:, None] // RET_DK == jnp.arange(RET_V_W)[None, :] // RET_DV)
    return cos_t, sin_t, dmask, qdec, kdec, sdec, bd.astype(_F32)


def _mixer(x, batch0, nb, layer, w_in, w_out, conv_w, conv_b, cg, cb, gn, lg, lb, tables, alpha,
           moe_part=None):
    _, s, d = x.shape
    t = min(SEQ_TILE, s)
    assert s % t == 0 and t % RET_BLOCK == 0 and RET_BLOCK % CHUNK == 0 and t % CONV_ROWS == 0
    cos_t, sin_t, dmask, qdec, kdec, sdec, bd = tables
    row = lambda a: a.reshape(1, -1)
    conv_w = jnp.pad(conv_w, ((0, HALO - CONV_K), (0, 0)))
    consts = [conv_w, row(conv_b), row(cg), row(cb), row(gn), row(lg), row(lb)]
    tail = [dmask, qdec, kdec, sdec, bd]
    pre, pre_specs = [], []
    if moe_part is not None:
        assert batch0 == 0 and x.shape[0] == nb
        y01, gates, pg, pb = moe_part
        y01 = y01.reshape(2, nb, s, d)
        pre = [y01, y01, gates.reshape(nb, s, 2), row(pg), row(pb)]
        pre_specs = [pl.BlockSpec((None, None, t, d), lambda i, j: (0, i, j, 0)),
                     pl.BlockSpec((None, None, t, d), lambda i, j: (1, i, j, 0)),
                     pl.BlockSpec((None, t, 2), lambda i, j: (i, j, 0)),
                     _const_spec((1, d)), _const_spec((1, d))]
    in_specs = ([pl.BlockSpec((None, t, d), lambda i, j: (i + batch0, j, 0))] + pre_specs
                + [_layer_spec(w_in, layer), _layer_spec(w_out, layer)]
                + [_const_spec(a.shape) for a in consts]
                + [pl.BlockSpec((t, RET_QK_W), lambda i, j: (j, 0))] * 2
                + [_const_spec(a.shape) for a in tail])
    return pl.pallas_call(
        functools.partial(_mixer_kernel, alpha, moe_part is not None),
        grid=(nb, s // t),
        in_specs=in_specs,
        out_specs=pl.BlockSpec((None, t, d), lambda i, j: (i, j, 0)),
        out_shape=jax.ShapeDtypeStruct((nb, s, d), _F32),
        scratch_shapes=[pltpu.VMEM((SUBLANES, t + HALO, CONV_CH), _F32),
                        pltpu.VMEM((RET_QK_W, RET_V_W), _F32),
                        pltpu.VMEM((t, RET_V_W), _F32)],
        compiler_params=pltpu.CompilerParams(
            dimension_semantics=("arbitrary", "arbitrary"), vmem_limit_bytes=VMEM_LIMIT),
        name="mixer",
    )(x, *pre, w_in, w_out, *consts, cos_t, sin_t, *tail)


def _swiglu_tile(xb, w1_ref, w3_ref, w2_ref):
    d_ff = w1_ref.shape[1]
    acc = None
    for c in range(0, d_ff, FFN_CHUNK):
        a = _dot(xb, w1_ref[:, c:c + FFN_CHUNK])
        g = _dot(xb, w3_ref[:, c:c + FFN_CHUNK])
        h = (a * _sigmoid(a) * g).astype(_BF)
        part = _dot(h, w2_ref[c:c + FFN_CHUNK, :])
        acc = part if acc is None else acc + part
    return acc


def _dense_ffn_kernel(alpha, n_cast, x_ref, w1_ref, w3_ref, w2_ref, lg_ref, lb_ref, *refs):
    cast_in, (o_ref, *cast_out) = refs[:n_cast], refs[n_cast:]
    x = x_ref[...]
    f = _swiglu_tile(x.astype(_BF), w1_ref, w3_ref, w2_ref)
    o_ref[...] = _layer_norm(alpha * x + f, lg_ref[...], lb_ref[...])
    for src, dst in zip(cast_in, cast_out):
        dst[...] = src[...].astype(_BF)


def _dense_ffn(x, layer, w1, w3, w2, lg, lb, alpha, cast=(), cast_layer=0):
    n, d = x.shape
    tm = min(FFN_TILE, n)
    steps = n // tm
    assert n % tm == 0 and w1.shape[2] % FFN_CHUNK == 0
    weights = [w1, w3, w2]
    norm = [lg.reshape(1, -1), lb.reshape(1, -1)]
    cast_in, cast_specs, cast_shapes = [], [], []
    for a in cast:
        cols = a.shape[-1]
        rows = a[0].size // cols
        assert rows % (steps * 2 * SUBLANES) == 0
        cast_in.append(a.reshape(-1, cols))
        cast_specs.append(pl.BlockSpec((rows // steps, cols),
                                       lambda i, base=cast_layer * steps: (base + i, 0)))
        cast_shapes.append(jax.ShapeDtypeStruct((rows, cols), _BF))
    row_spec = pl.BlockSpec((tm, d), lambda i: (i, 0))
    out = pl.pallas_call(
        functools.partial(_dense_ffn_kernel, alpha, len(cast)),
        grid=(steps,),
        in_specs=([row_spec] + [_layer_spec(w, layer) for w in weights]
                  + [_const_spec(a.shape) for a in norm] + cast_specs),
        out_specs=[row_spec] + [pl.BlockSpec(s.block_shape, lambda i: (i, 0)) for s in cast_specs],
        out_shape=[jax.ShapeDtypeStruct((n, d), _F32)] + cast_shapes,
        compiler_params=pltpu.CompilerParams(
            dimension_semantics=("arbitrary",), vmem_limit_bytes=VMEM_LIMIT),
        name="dense_ffn",
    )(x, *weights, *norm, *cast_in)
    return out[0], [o.reshape((1,) + a.shape[1:]) for o, a in zip(out[1:], cast)]


def _router_kernel(x_ref, wr_ref, tri_ref, ri_ref, rf_ref, cnt_ref, carry):
    @pl.when(pl.program_id(0) == 0)
    def _():
        carry[...] = jnp.zeros_like(carry)

    tr = x_ref.shape[0]
    logits = lax.dot_general(wr_ref[...], x_ref[...].astype(_BF), (((1,), (1,)), ((), ())),
                             preferred_element_type=_F32)
    eidx = lax.broadcasted_iota(jnp.int32, (N_EXPERTS, tr), 0)
    m0 = jnp.max(logits, axis=0, keepdims=True)
    e0 = jnp.min(jnp.where(logits == m0, eidx, N_EXPERTS), axis=0, keepdims=True)
    rest = jnp.where(eidx == e0, -jnp.inf, logits)
    m1 = jnp.max(rest, axis=0, keepdims=True)
    e1 = jnp.min(jnp.where(rest == m1, eidx, N_EXPERTS), axis=0, keepdims=True)
    tt = jnp.exp(m1 - m0)
    g0 = 1.0 / (1.0 + tt)
    g1 = tt / (1.0 + tt)

    oh0 = eidx == e0
    oh1 = eidx == e1
    member = jnp.where(oh0 | oh1, 1.0, 0.0)
    before = _dot(member.astype(_BF), tri_ref[...]) + carry[:, 0:1]
    rank0 = jnp.sum(jnp.where(oh0, before, 0.0), axis=0, keepdims=True)
    rank1 = jnp.sum(jnp.where(oh1, before, 0.0), axis=0, keepdims=True)
    carry[...] = carry[...] + jnp.sum(member, axis=1, keepdims=True)

    zi = jnp.zeros((N_EXPERTS - 4, tr), jnp.int32)
    ri_ref[...] = jnp.concatenate(
        [e0, e1, rank0.astype(jnp.int32), rank1.astype(jnp.int32), zi], axis=0)
    rf_ref[...] = jnp.concatenate([g0, g1, jnp.zeros((N_EXPERTS - 2, tr), _F32)], axis=0)
    cnt_ref[...] = carry[...]


def _router(x, wr_t):
    n, d = x.shape
    tr = min(ROUTE_TILE, n)
    assert n % tr == 0
    tri = (jnp.arange(tr)[:, None] < jnp.arange(tr)[None, :]).astype(_BF)
    return pl.pallas_call(
        _router_kernel,
        grid=(n // tr,),
        in_specs=[pl.BlockSpec((tr, d), lambda i: (i, 0)),
                  _const_spec(wr_t.shape), _const_spec(tri.shape)],
        out_specs=[pl.BlockSpec((N_EXPERTS, tr), lambda i: (0, i)),
                   pl.BlockSpec((N_EXPERTS, tr), lambda i: (0, i)),
                   pl.BlockSpec((N_EXPERTS, LANES), lambda i: (0, 0))],
        out_shape=[jax.ShapeDtypeStruct((N_EXPERTS, n), jnp.int32),
                   jax.ShapeDtypeStruct((N_EXPERTS, n), _F32),
                   jax.ShapeDtypeStruct((N_EXPERTS, LANES), _F32)],
        scratch_shapes=[pltpu.VMEM((N_EXPERTS, LANES), _F32)],
        compiler_params=pltpu.CompilerParams(
            dimension_semantics=("arbitrary",), vmem_limit_bytes=VMEM_LIMIT),
        name="router",
    )(x, wr_t, tri)


def _move_tile(n):
    tm = min(MOVE_TILE, n)
    assert n % tm == 0 and tm % SUBLANES == 0
    return tm


def _expert_kernel(te_ref, na_ref, xs_ref, w1_ref, w3_ref, w2_ref, ys_ref):
    del te_ref
    active = pl.program_id(0) < na_ref[0]

    @pl.when(active)
    def _():
        ys_ref[...] = _swiglu_tile(xs_ref[...].astype(_BF), w1_ref, w3_ref, w2_ref)

    @pl.when(jnp.logical_not(active))
    def _():
        ys_ref[...] = jnp.zeros_like(ys_ref)


def _expert_ffn(xs, layer, w1, w3, w2, tile_expert, n_active):
    d = xs.shape[1]
    tm = FFN_TILE
    d_ff = w1.shape[3]
    n_tiles = tile_expert.shape[0]
    n_slots = n_tiles * tm
    expert_block = lambda i, te, na: (layer, te[i], 0, 0)
    w1_spec = pl.BlockSpec((None, None, d, d_ff), expert_block, pipeline_mode=pl.Buffered(1))
    w3_spec = pl.BlockSpec((None, None, d, d_ff), expert_block)
    w2_spec = pl.BlockSpec((None, None, d_ff, d), expert_block)
    return pl.pallas_call(
        _expert_kernel,
        grid_spec=pltpu.PrefetchScalarGridSpec(
            num_scalar_prefetch=2,
            grid=(n_tiles,),
            in_specs=[pl.BlockSpec((tm, d), lambda i, te, na: (jnp.minimum(i, na[0] - 1), 0)),
                      w1_spec, w3_spec, w2_spec],
            out_specs=pl.BlockSpec((tm, d), lambda i, te, na: (i, 0)),
        ),
        out_shape=jax.ShapeDtypeStruct((n_slots, d), _F32),
        compiler_params=pltpu.CompilerParams(
            dimension_semantics=("arbitrary",), vmem_limit_bytes=VMEM_LIMIT),
        name="expert_ffn",
    )(tile_expert, n_active, xs, w1, w3, w2)


def _sc_gather(table, idx):
    m, d = idx.shape[0], table.shape[1]
    workers = SC_CORES * SC_SUBCORES
    per_worker = m // workers
    pairs = per_worker // (2 * SC_ROWS)
    assert m % (workers * 2 * SC_ROWS) == 0

    def body(table_hbm, idx_hbm, out_hbm, idx_a, idx_b, rows_a, rows_b, gsem_a, gsem_b, wsem_a, wsem_b):
        base = (lax.axis_index("s") * SC_CORES + lax.axis_index("c")) * per_worker
        buf_a = (idx_a, rows_a, gsem_a, wsem_a)
        buf_b = (idx_b, rows_b, gsem_b, wsem_b)

        def out_rows(c):
            return out_hbm.at[pl.ds(pl.multiple_of(base + c * SC_ROWS, SC_ROWS), SC_ROWS)]

        def start_gather(c, buf):
            idx_v, rows_v, gsem, _ = buf
            off = pl.multiple_of(base + c * SC_ROWS, SC_ROWS)
            pltpu.sync_copy(idx_hbm.at[pl.ds(off, SC_ROWS)], idx_v)
            pltpu.async_copy(table_hbm.at[idx_v], rows_v, gsem)

        def start_write(c, buf):
            idx_v, rows_v, gsem, wsem = buf
            pltpu.make_async_copy(table_hbm.at[idx_v], rows_v, gsem).wait()
            pltpu.async_copy(rows_v, out_rows(c), wsem)

        def wait_write(c, buf):
            _, rows_v, _, wsem = buf
            pltpu.make_async_copy(rows_v, out_rows(c), wsem).wait()

        start_gather(0, buf_a)

        @pl.loop(0, pairs)
        def _(p):
            c = 2 * p

            @pl.when(p > 0)
            def _():
                wait_write(c - 1, buf_b)

            start_gather(c + 1, buf_b)
            start_write(c, buf_a)
            wait_write(c, buf_a)

            @pl.when(p < pairs - 1)
            def _():
                start_gather(c + 2, buf_a)

            start_write(c + 1, buf_b)

        wait_write(2 * pairs - 1, buf_b)

    rows = pltpu.VMEM((SC_ROWS, d), table.dtype)
    index = pltpu.VMEM((SC_ROWS,), jnp.int32)
    return pl.kernel(
        body, out_type=jax.ShapeDtypeStruct((m, d), table.dtype), mesh=_sc_mesh(),
        scratch_types=[index, index, rows, rows] + [pltpu.SemaphoreType.DMA] * 4,
        name="sc_gather")(table, idx)


def _sc_mesh():
    return plsc.VectorSubcoreMesh(core_axis_name="c", subcore_axis_name="s",
                                  num_cores=SC_CORES, num_subcores=SC_SUBCORES)


def _sc_scatter(x, pos, pad_slots, n_slots):
    n, d = x.shape
    workers = SC_CORES * SC_SUBCORES
    per_worker = n // workers
    pad_per_worker = pad_slots.shape[0] // workers
    assert n % (workers * SC_ROWS) == 0 and pad_slots.shape[0] % (workers * SC_ROWS) == 0
    assert pos.shape[0] == 2 * n and n_slots == 2 * n + pad_slots.shape[0]

    pairs = per_worker // (2 * SC_ROWS)
    assert per_worker % (2 * SC_ROWS) == 0

    def body(x_hbm, pos_hbm, pad_hbm, out_hbm, i0_a, i1_a, i0_b, i1_b, rows_a, rows_b,
             lsem_a, lsem_b, ssem_a, ssem_b):
        wid = lax.axis_index("s") * SC_CORES + lax.axis_index("c")
        buf_a = (i0_a, i1_a, rows_a, lsem_a, ssem_a)
        buf_b = (i0_b, i1_b, rows_b, lsem_b, ssem_b)

        def x_rows(c):
            return x_hbm.at[pl.ds(pl.multiple_of(wid * per_worker + c * SC_ROWS, SC_ROWS), SC_ROWS)]

        def start_load(c, buf):
            i0, i1, rows_v, lsem, _ = buf
            off = pl.multiple_of(wid * per_worker + c * SC_ROWS, SC_ROWS)
            pltpu.async_copy(x_rows(c), rows_v, lsem)
            pltpu.sync_copy(pos_hbm.at[pl.ds(off, SC_ROWS)], i0)
            pltpu.sync_copy(pos_hbm.at[pl.ds(n + off, SC_ROWS)], i1)

        def start_scatter(c, buf):
            i0, i1, rows_v, lsem, ssem = buf
            pltpu.make_async_copy(x_rows(c), rows_v, lsem).wait()
            pltpu.async_copy(rows_v, out_hbm.at[i0], ssem)
            pltpu.async_copy(rows_v, out_hbm.at[i1], ssem)

        def wait_scatter(buf):
            i0, i1, rows_v, _, ssem = buf
            pltpu.make_async_copy(rows_v, out_hbm.at[i0], ssem).wait()
            pltpu.make_async_copy(rows_v, out_hbm.at[i1], ssem).wait()

        start_load(0, buf_a)

        @pl.loop(0, pairs)
        def _(p):
            c = 2 * p

            @pl.when(p > 0)
            def _():
                wait_scatter(buf_b)

            start_load(c + 1, buf_b)
            start_scatter(c, buf_a)
            wait_scatter(buf_a)

            @pl.when(p < pairs - 1)
            def _():
                start_load(c + 2, buf_a)

            start_scatter(c + 1, buf_b)

        wait_scatter(buf_b)

        @pl.loop(0, pad_per_worker // SC_ROWS)
        def _(c):
            off = pl.multiple_of(wid * pad_per_worker + c * SC_ROWS, SC_ROWS)
            pltpu.sync_copy(pad_hbm.at[pl.ds(off, SC_ROWS)], i0_a)
            pltpu.sync_copy(rows_b, out_hbm.at[i0_a])

    rows = pltpu.VMEM((SC_ROWS, d), x.dtype)
    index = pltpu.VMEM((SC_ROWS,), jnp.int32)
    return pl.kernel(
        body, out_type=jax.ShapeDtypeStruct((n_slots, d), x.dtype), mesh=_sc_mesh(),
        scratch_types=[index] * 4 + [rows, rows] + [pltpu.SemaphoreType.DMA] * 4,
        name="sc_scatter")(x, pos, pad_slots)


def _combine_kernel(alpha, n_groups, *refs):
    lg_ref, lb_ref, o_ref = refs[-3:]
    steps = pl.num_programs(0) // n_groups
    i = pl.program_id(0)
    for g in range(n_groups):
        x_ref, y0_ref, y1_ref, gate_ref = refs[4 * g:4 * g + 4]

        @pl.when((i >= g * steps) & (i < (g + 1) * steps))
        def _(x_ref=x_ref, y0_ref=y0_ref, y1_ref=y1_ref, gate_ref=gate_ref):
            o_ref[...] = _moe_residual(alpha, x_ref[...], y0_ref[...], y1_ref[...], gate_ref[...],
                                       lg_ref[...], lb_ref[...])


def _combine(parts, lg, lb, alpha):
    n, d = parts[0][0].shape
    tm = _move_tile(n)
    steps = n // tm
    in_specs, args = [], []
    for g, (x, y01, gates) in enumerate(parts):
        row = lambda i, g=g: jnp.clip(i - g * steps, 0, steps - 1)
        in_specs += [pl.BlockSpec((tm, d), lambda i, row=row: (row(i), 0)),
                     pl.BlockSpec((tm, d), lambda i, row=row: (row(i), 0)),
                     pl.BlockSpec((tm, d), lambda i, row=row: (row(i) + steps, 0)),
                     pl.BlockSpec((tm, 2), lambda i, row=row: (row(i), 0))]
        args += [x, y01, y01, gates]
    return pl.pallas_call(
        functools.partial(_combine_kernel, alpha, len(parts)),
        grid=(len(parts) * steps,),
        in_specs=in_specs + [_const_spec((1, d)), _const_spec((1, d))],
        out_specs=pl.BlockSpec((tm, d), lambda i: (i, 0)),
        out_shape=jax.ShapeDtypeStruct((len(parts) * n, d), _F32),
        compiler_params=pltpu.CompilerParams(
            dimension_semantics=("arbitrary",), vmem_limit_bytes=VMEM_LIMIT),
        name="combine",
    )(*args, lg.reshape(1, -1), lb.reshape(1, -1))


def _moe_experts(x, router, layer, w1, w3, w2):
    n, d = x.shape
    tm = FFN_TILE
    n_tiles = -(-(2 * n) // tm) + N_EXPERTS
    n_slots = n_tiles * tm

    ri, rf, cnt = _router(x, router.T.astype(_BF))
    counts = cnt[:, 0].astype(jnp.int32)
    padded = (counts + tm - 1) // tm * tm
    group_end = jnp.cumsum(padded)
    group_start = group_end - padded
    experts = jnp.arange(N_EXPERTS, dtype=jnp.int32)[:, None]

    def slot(e, rank):
        return jnp.sum(jnp.where(e[None, :] == experts, group_start[:, None], 0), axis=0) + rank

    pos = jnp.stack([slot(ri[0], ri[2]), slot(ri[1], ri[3])])
    tile_start = jnp.arange(n_tiles, dtype=jnp.int32) * tm
    tile_expert = jnp.minimum(
        jnp.sum(tile_start[:, None] >= group_end[None, :], axis=1), N_EXPERTS - 1).astype(jnp.int32)
    n_active = (group_end[-1:] // tm).astype(jnp.int32)

    pad_len = jnp.concatenate([padded - counts, n_slots - group_end[-1:]])
    pad_begin = jnp.concatenate([group_start + counts, group_end[-1:]])
    pad_end = jnp.cumsum(pad_len)
    k = jnp.arange(n_slots - 2 * n, dtype=jnp.int32)
    run = jnp.sum(k[:, None] >= pad_end[None, :], axis=1)
    pad_slots = k + jnp.sum(
        jnp.where(run[:, None] == jnp.arange(N_EXPERTS + 1)[None, :],
                  (pad_begin - (pad_end - pad_len))[None, :], 0), axis=1)
    xs = _sc_scatter(x, pos.reshape(-1), pad_slots.astype(jnp.int32), n_slots)
    ys = _expert_ffn(xs, layer, w1, w3, w2, tile_expert, n_active)
    return x, _sc_gather(ys, pos.reshape(-1)), rf[:2].T


def kernel(x, w_in, w_out, conv_w, conv_b, conv_ln_g, conv_ln_b, ret_gn_g, ln1_g, ln1_b, ln2_g,
           ln2_b, dense_w1, dense_w3, dense_w2, moe_router, moe_w1, moe_w3, moe_w2):
    b, s, d = x.shape
    depth = w_in.shape[0]
    alpha = (2.0 * depth) ** 0.25
    tables = _retention_tables(s)
    ng = BATCH_GROUPS if b % BATCH_GROUPS == 0 else 1
    bg = b // ng
    groups = [(x, g * bg, None) for g in range(ng)]
    mix_w = (w_in.astype(_BF), w_out.astype(_BF))
    dense_w = (dense_w1.astype(_BF), dense_w3.astype(_BF), dense_w2.astype(_BF))
    moe_f32 = (moe_w1, moe_w3, moe_w2)
    moe_w = {}
    for l in range(depth):
        i = l // 2
        for g, (xg, batch0, moe_part) in enumerate(groups):
            h = _mixer(xg, batch0, bg, l, *mix_w, conv_w[l], conv_b[l], conv_ln_g[l], conv_ln_b[l],
                       ret_gn_g[l], ln1_g[l], ln1_b[l], tables, alpha, moe_part).reshape(bg * s, d)
            if l % 2 == 0:
                cast = moe_f32 if g == 0 and l + 1 < depth else ()
                h, converted = _dense_ffn(h, i, *dense_w, ln2_g[l], ln2_b[l], alpha, cast, i)
                if cast:
                    moe_w[i] = converted
                groups[g] = (h.reshape(bg, s, d), 0, None)
            else:
                if i not in moe_w:
                    moe_w[i] = [a[i:i + 1].astype(_BF) for a in moe_f32]
                x_in, y01, gates = _moe_experts(h, moe_router[i], 0, *moe_w[i])
                groups[g] = (x_in.reshape(bg, s, d), 0, (y01, gates, ln2_g[l], ln2_b[l]))
    if groups[0][2] is None:
        return jnp.concatenate([xg for xg, _, _ in groups], axis=0)
    parts = [(xg.reshape(bg * s, d), part[0], part[1]) for xg, _, part in groups]
    return _combine(parts, ln2_g[-1], ln2_b[-1], alpha).reshape(b, s, d)
```

```python
import functools

import jax
import jax.numpy as jnp
from jax import lax
from jax.experimental import pallas as pl
from jax.experimental.pallas import tpu as pltpu
from jax.experimental.pallas import tpu_sc as plsc

CHUNK = 64
CONV_CH = 512
CONV_K = 31
RET_HEADS = 4
RET_DK = 64
RET_DV = 128
RET_QK_W = RET_HEADS * RET_DK
RET_V_W = RET_HEADS * RET_DV
ROPE_BASE = 10000.0
N_EXPERTS = 8
LN_EPS = 1e-5

LANES = 128
SUBLANES = 8
SEQ_TILE = 512
RET_BLOCK = 128
CONV_ROWS = 32
HALO = 32
FFN_TILE = 512
FFN_CHUNK = 512
ROUTE_TILE = 1024
MOVE_TILE = 512
SC_CORES = 2
SC_SUBCORES = 16
SC_ROWS = 32
BATCH_GROUPS = 2
VMEM_LIMIT = 56 * 1024 * 1024

_BF = jnp.bfloat16
_F32 = jnp.float32


def _dot(a, b):
    return jnp.dot(a, b, preferred_element_type=_F32)


def _layer_norm(v, g, b):
    mu = jnp.mean(v, axis=-1, keepdims=True)
    d = v - mu
    var = jnp.mean(d * d, axis=-1, keepdims=True)
    return d * lax.rsqrt(var + LN_EPS) * g + b


def _sigmoid(v):
    return 1.0 / (1.0 + jnp.exp(-v))


def _const_spec(shape):
    nd = len(shape)
    return pl.BlockSpec(shape, lambda *_: (0,) * nd, pipeline_mode=pl.Buffered(1))


def _layer_spec(stacked, layer):
    rest = stacked.shape[1:]
    return pl.BlockSpec((None,) + rest, lambda *_: (layer,) + (0,) * len(rest),
                        pipeline_mode=pl.Buffered(1))


def _moe_residual(alpha, x, y0, y1, gate, g, b):
    return _layer_norm(alpha * x + (y0 * gate[:, 0:1] + y1 * gate[:, 1:2]), g, b)


def _mixer_kernel(alpha, after_moe, x_ref, *refs):
    if after_moe:
        y0_ref, y1_ref, gate_ref, pg_ref, pb_ref, *refs = refs
    (w_in_ref, w_out_ref, conv_w_ref, conv_b_ref, cg_ref, cb_ref, gn_ref, lg_ref, lb_ref, cos_ref,
     sin_ref, dmask_ref, qdec_ref, kdec_ref, sdec_ref, bd_ref, o_ref, ush, state, rbuf) = refs
    t = x_ref.shape[0]
    ubuf = ush.at[0]

    @pl.when(pl.program_id(1) == 0)
    def _():
        ubuf[0:HALO, :] = jnp.zeros((HALO, CONV_CH), _F32)
        state[...] = jnp.zeros_like(state)

    x = x_ref[...]
    if after_moe:
        x = _moe_residual(alpha, x, y0_ref[...], y1_ref[...], gate_ref[...], pg_ref[...], pb_ref[...])
    xb = x.astype(_BF)

    c0 = 2 * CONV_CH
    ab = _dot(xb, w_in_ref[:, 0:c0])
    qk = _dot(xb, w_in_ref[:, c0:c0 + 2 * RET_QK_W])
    v = _dot(xb, w_in_ref[:, c0 + 2 * RET_QK_W:c0 + 2 * RET_QK_W + RET_V_W])
    g = _dot(xb, w_in_ref[:, c0 + 2 * RET_QK_W + RET_V_W:])

    ubuf[HALO:HALO + t, :] = ab[:, :CONV_CH] * _sigmoid(ab[:, CONV_CH:])
    span = t + HALO - SUBLANES
    for r in range(1, SUBLANES):
        ush[r, 0:span, :] = ubuf[r:r + span, :]
    off = HALO - (CONV_K - 1)
    conv_b = conv_b_ref[...]
    blocks = []
    for r0 in range(0, t, CONV_ROWS):
        acc = jnp.broadcast_to(conv_b, (CONV_ROWS, CONV_CH))
        for r in range(SUBLANES):
            taps = [j for j in range(CONV_K) if (j + off) % SUBLANES == r]
            for j in taps:
                a = r0 + (j + off) // SUBLANES * SUBLANES
                acc = acc + conv_w_ref[j:j + 1, :] * ush[r, a:a + CONV_ROWS, :]
        blocks.append(acc)
    conv = jnp.concatenate(blocks, axis=0)
    ubuf[0:HALO, :] = ubuf[t:t + HALO, :]
    un = _layer_norm(conv, cg_ref[...], cb_ref[...])
    u_out = un * _sigmoid(un)

    cos_t = cos_ref[...]
    sin_t = sin_ref[...]
    lane = lax.broadcasted_iota(jnp.int32, (1, LANES), 1)
    first_half = (lane % RET_DK) < (RET_DK // 2)

    def rope(z):
        parts = []
        for c in range(0, RET_QK_W, LANES):
            zc = z[:, c:c + LANES]
            up = pltpu.roll(zc, LANES - RET_DK // 2, axis=1)
            dn = pltpu.roll(zc, RET_DK // 2, axis=1)
            parts.append(jnp.where(first_half, up, dn))
        return z * cos_t + jnp.concatenate(parts, axis=1) * sin_t

    q = rope(qk[:, :RET_QK_W])
    k = rope(qk[:, RET_QK_W:])

    lane_qk = lax.broadcasted_iota(jnp.int32, (1, RET_QK_W), 1)
    dmask = dmask_ref[...]
    qdec = qdec_ref[...]
    kdec = kdec_ref[...]
    nr = RET_BLOCK
    for s in range(t // nr):
        rows = slice(s * nr, (s + 1) * nr)
        q_s, k_s, v_s = q[rows], k[rows], v[rows]
        v_b = v_s.astype(_BF)
        kbd = jnp.concatenate(
            [jnp.where(lane_qk // RET_DK == h, k_s, 0.0).astype(_BF) for h in range(RET_HEADS)],
            axis=0)
        sc = lax.dot_general(q_s.astype(_BF), kbd, (((1,), (1,)), ((), ())),
                             preferred_element_type=_F32)
        p = (sc * dmask).astype(_BF)
        intra = jnp.concatenate(
            [_dot(p[:, h * nr:(h + 1) * nr], v_b[:, h * RET_DV:(h + 1) * RET_DV])
             for h in range(RET_HEADS)], axis=1)
        st = state[...]
        cross = _dot((q_s * qdec).astype(_BF), st.astype(_BF))
        kv = lax.dot_general((k_s * kdec).astype(_BF), v_b, (((0,), (0,)), ((), ())),
                             preferred_element_type=_F32)
        state[...] = st * sdec_ref[...] + kv * bd_ref[...]
        rbuf[rows, :] = intra + cross

    r = rbuf[...]
    gate = g * _sigmoid(g)
    gn = gn_ref[...]
    r_parts = []
    for h in range(RET_HEADS):
        hs = slice(h * RET_DV, (h + 1) * RET_DV)
        rh = r[:, hs]
        mu = jnp.mean(rh, axis=-1, keepdims=True)
        d = rh - mu
        var = jnp.mean(d * d, axis=-1, keepdims=True)
        r_parts.append(gate[:, hs] * (d * lax.rsqrt(var + LN_EPS) * gn[:, hs]))

    mixed = jnp.concatenate([u_out] + r_parts, axis=1).astype(_BF)
    mix = _dot(mixed, w_out_ref[...])
    o_ref[...] = _layer_norm(alpha * x + mix, lg_ref[...], lb_ref[...])


def _retention_tables(seq):
    half = RET_DK // 2
    inv_freq = ROPE_BASE ** (-jnp.arange(half, dtype=_F32) / half)
    ang = jnp.arange(seq, dtype=jnp.int32).astype(_F32)[:, None] * inv_freq[None, :]
    cos, sin = jnp.cos(ang), jnp.sin(ang)
    cos_t = jnp.tile(jnp.concatenate([cos, cos], axis=1), (1, RET_HEADS))
    sin_t = jnp.tile(jnp.concatenate([-sin, sin], axis=1), (1, RET_HEADS))

    nr = RET_BLOCK
    log_gamma = jnp.log1p(-(2.0 ** (-5.0 - jnp.arange(RET_HEADS, dtype=_F32))))
    idx = jnp.arange(nr, dtype=_F32)
    dist = jnp.abs(idx[:, None] - idx[None, :])
    visible = (jnp.arange(nr)[None, :] // CHUNK) <= (jnp.arange(nr)[:, None] // CHUNK)
    scale = RET_DK ** -0.5
    dm = jnp.exp(log_gamma[:, None, None] * dist[None]) * visible[None] * scale
    dmask = jnp.transpose(dm, (1, 0, 2)).reshape(nr, RET_HEADS * nr)
    qdec = jnp.repeat(jnp.exp(log_gamma[None, :] * (idx + 1.0)[:, None]), RET_DK, axis=1)
    kdec = jnp.repeat(jnp.exp(log_gamma[None, :] * (nr - 1.0 - idx)[:, None]), RET_DK, axis=1) * scale
    sdec = jnp.broadcast_to(jnp.repeat(jnp.exp(log_gamma * nr), RET_DK)[:, None],
                            (RET_QK_W, RET_V_W))
    bd = (jnp.arange(RET_QK_W)[:, None] // RET_DK == jnp.arange(RET_V_W)[None, :] // RET_DV)
    return cos_t, sin_t, dmask, qdec, kdec, sdec, bd.astype(_F32)


def _mixer(x, batch0, nb, layer, w_in, w_out, conv_w, conv_b, cg, cb, gn, lg, lb, tables, alpha,
           moe_part=None):
    _, s, d = x.shape
    t = min(SEQ_TILE, s)
    assert s % t == 0 and t % RET_BLOCK == 0 and RET_BLOCK % CHUNK == 0 and t % CONV_ROWS == 0
    cos_t, sin_t, dmask, qdec, kdec, sdec, bd = tables
    row = lambda a: a.reshape(1, -1)
    conv_w = jnp.pad(conv_w, ((0, HALO - CONV_K), (0, 0)))
    consts = [conv_w, row(conv_b), row(cg), row(cb), row(gn), row(lg), row(lb)]
    tail = [dmask, qdec, kdec, sdec, bd]
    pre, pre_specs = [], []
    if moe_part is not None:
        assert batch0 == 0 and x.shape[0] == nb
        y01, gates, pg, pb = moe_part
        y01 = y01.reshape(2, nb, s, d)
        pre = [y01, y01, gates.reshape(nb, s, 2), row(pg), row(pb)]
        pre_specs = [pl.BlockSpec((None, None, t, d), lambda i, j: (0, i, j, 0)),
                     pl.BlockSpec((None, None, t, d), lambda i, j: (1, i, j, 0)),
                     pl.BlockSpec((None, t, 2), lambda i, j: (i, j, 0)),
                     _const_spec((1, d)), _const_spec((1, d))]
    in_specs = ([pl.BlockSpec((None, t, d), lambda i, j: (i + batch0, j, 0))] + pre_specs
                + [_layer_spec(w_in, layer), _layer_spec(w_out, layer)]
                + [_const_spec(a.shape) for a in consts]
                + [pl.BlockSpec((t, RET_QK_W), lambda i, j: (j, 0))] * 2
                + [_const_spec(a.shape) for a in tail])
    return pl.pallas_call(
        functools.partial(_mixer_kernel, alpha, moe_part is not None),
        grid=(nb, s // t),
        in_specs=in_specs,
        out_specs=pl.BlockSpec((None, t, d), lambda i, j: (i, j, 0)),
        out_shape=jax.ShapeDtypeStruct((nb, s, d), _F32),
        scratch_shapes=[pltpu.VMEM((SUBLANES, t + HALO, CONV_CH), _F32),
                        pltpu.VMEM((RET_QK_W, RET_V_W), _F32),
                        pltpu.VMEM((t, RET_V_W), _F32)],
        compiler_params=pltpu.CompilerParams(
            dimension_semantics=("arbitrary", "arbitrary"), vmem_limit_bytes=VMEM_LIMIT),
        name="mixer",
    )(x, *pre, w_in, w_out, *consts, cos_t, sin_t, *tail)


def _swiglu_tile(xb, w1_ref, w3_ref, w2_ref):
    d_ff = w1_ref.shape[1]
    acc = None
    for c in range(0, d_ff, FFN_CHUNK):
        a = _dot(xb, w1_ref[:, c:c + FFN_CHUNK])
        g = _dot(xb, w3_ref[:, c:c + FFN_CHUNK])
        h = (a * _sigmoid(a) * g).astype(_BF)
        part = _dot(h, w2_ref[c:c + FFN_CHUNK, :])
        acc = part if acc is None else acc + part
    return acc


def _dense_ffn_kernel(alpha, n_cast, x_ref, w1_ref, w3_ref, w2_ref, lg_ref, lb_ref, *refs):
    cast_in, (o_ref, *cast_out) = refs[:n_cast], refs[n_cast:]
    x = x_ref[...]
    f = _swiglu_tile(x.astype(_BF), w1_ref, w3_ref, w2_ref)
    o_ref[...] = _layer_norm(alpha * x + f, lg_ref[...], lb_ref[...])
    for src, dst in zip(cast_in, cast_out):
        dst[...] = src[...].astype(_BF)


def _dense_ffn(x, layer, w1, w3, w2, lg, lb, alpha, cast=(), cast_layer=0):
    n, d = x.shape
    tm = min(FFN_TILE, n)
    steps = n // tm
    assert n % tm == 0 and w1.shape[2] % FFN_CHUNK == 0
    weights = [w1, w3, w2]
    norm = [lg.reshape(1, -1), lb.reshape(1, -1)]
    cast_in, cast_specs, cast_shapes = [], [], []
    for a in cast:
        cols = a.shape[-1]
        rows = a[0].size // cols
        assert rows % (steps * 2 * SUBLANES) == 0
        cast_in.append(a.reshape(-1, cols))
        cast_specs.append(pl.BlockSpec((rows // steps, cols),
                                       lambda i, base=cast_layer * steps: (base + i, 0)))
        cast_shapes.append(jax.ShapeDtypeStruct((rows, cols), _BF))
    row_spec = pl.BlockSpec((tm, d), lambda i: (i, 0))
    out = pl.pallas_call(
        functools.partial(_dense_ffn_kernel, alpha, len(cast)),
        grid=(steps,),
        in_specs=([row_spec] + [_layer_spec(w, layer) for w in weights]
                  + [_const_spec(a.shape) for a in norm] + cast_specs),
        out_specs=[row_spec] + [pl.BlockSpec(s.block_shape, lambda i: (i, 0)) for s in cast_specs],
        out_shape=[jax.ShapeDtypeStruct((n, d), _F32)] + cast_shapes,
        compiler_params=pltpu.CompilerParams(
            dimension_semantics=("arbitrary",), vmem_limit_bytes=VMEM_LIMIT),
        name="dense_ffn",
    )(x, *weights, *norm, *cast_in)
    return out[0], [o.reshape((1,) + a.shape[1:]) for o, a in zip(out[1:], cast)]


def _router_kernel(x_ref, wr_ref, tri_ref, ri_ref, rf_ref, cnt_ref, carry):
    @pl.when(pl.program_id(0) == 0)
    def _():
        carry[...] = jnp.zeros_like(carry)

    tr = x_ref.shape[0]
    logits = lax.dot_general(wr_ref[...], x_ref[...].astype(_BF), (((1,), (1,)), ((), ())),
                             preferred_element_type=_F32)
    eidx = lax.broadcasted_iota(jnp.int32, (N_EXPERTS, tr), 0)
    m0 = jnp.max(logits, axis=0, keepdims=True)
    e0 = jnp.min(jnp.where(logits == m0, eidx, N_EXPERTS), axis=0, keepdims=True)
    rest = jnp.where(eidx == e0, -jnp.inf, logits)
    m1 = jnp.max(rest, axis=0, keepdims=True)
    e1 = jnp.min(jnp.where(rest == m1, eidx, N_EXPERTS), axis=0, keepdims=True)
    tt = jnp.exp(m1 - m0)
    g0 = 1.0 / (1.0 + tt)
    g1 = tt / (1.0 + tt)

    oh0 = eidx == e0
    oh1 = eidx == e1
    member = jnp.where(oh0 | oh1, 1.0, 0.0)
    before = _dot(member.astype(_BF), tri_ref[...]) + carry[:, 0:1]
    rank0 = jnp.sum(jnp.where(oh0, before, 0.0), axis=0, keepdims=True)
    rank1 = jnp.sum(jnp.where(oh1, before, 0.0), axis=0, keepdims=True)
    carry[...] = carry[...] + jnp.sum(member, axis=1, keepdims=True)

    zi = jnp.zeros((N_EXPERTS - 4, tr), jnp.int32)
    ri_ref[...] = jnp.concatenate(
        [e0, e1, rank0.astype(jnp.int32), rank1.astype(jnp.int32), zi], axis=0)
    rf_ref[...] = jnp.concatenate([g0, g1, jnp.zeros((N_EXPERTS - 2, tr), _F32)], axis=0)
    cnt_ref[...] = carry[...]


def _router(x, wr_t):
    n, d = x.shape
    tr = min(ROUTE_TILE, n)
    assert n % tr == 0
    tri = (jnp.arange(tr)[:, None] < jnp.arange(tr)[None, :]).astype(_BF)
    return pl.pallas_call(
        _router_kernel,
        grid=(n // tr,),
        in_specs=[pl.BlockSpec((tr, d), lambda i: (i, 0)),
                  _const_spec(wr_t.shape), _const_spec(tri.shape)],
        out_specs=[pl.BlockSpec((N_EXPERTS, tr), lambda i: (0, i)),
                   pl.BlockSpec((N_EXPERTS, tr), lambda i: (0, i)),
                   pl.BlockSpec((N_EXPERTS, LANES), lambda i: (0, 0))],
        out_shape=[jax.ShapeDtypeStruct((N_EXPERTS, n), jnp.int32),
                   jax.ShapeDtypeStruct((N_EXPERTS, n), _F32),
                   jax.ShapeDtypeStruct((N_EXPERTS, LANES), _F32)],
        scratch_shapes=[pltpu.VMEM((N_EXPERTS, LANES), _F32)],
        compiler_params=pltpu.CompilerParams(
            dimension_semantics=("arbitrary",), vmem_limit_bytes=VMEM_LIMIT),
        name="router",
    )(x, wr_t, tri)


def _move_tile(n):
    tm = min(MOVE_TILE, n)
    assert n % tm == 0 and tm % SUBLANES == 0
    return tm


def _expert_kernel(te_ref, na_ref, xs_ref, w1_ref, w3_ref, w2_ref, ys_ref):
    del te_ref
    active = pl.program_id(0) < na_ref[0]

    @pl.when(active)
    def _():
        ys_ref[...] = _swiglu_tile(xs_ref[...].astype(_BF), w1_ref, w3_ref, w2_ref)

    @pl.when(jnp.logical_not(active))
    def _():
        ys_ref[...] = jnp.zeros_like(ys_ref)


def _expert_ffn(xs, layer, w1, w3, w2, tile_expert, n_active):
    d = xs.shape[1]
    tm = FFN_TILE
    d_ff = w1.shape[3]
    n_tiles = tile_expert.shape[0]
    n_slots = n_tiles * tm
    expert_block = lambda i, te, na: (layer, te[i], 0, 0)
    w1_spec = pl.BlockSpec((None, None, d, d_ff), expert_block, pipeline_mode=pl.Buffered(1))
    w3_spec = pl.BlockSpec((None, None, d, d_ff), expert_block)
    w2_spec = pl.BlockSpec((None, None, d_ff, d), expert_block)
    return pl.pallas_call(
        _expert_kernel,
        grid_spec=pltpu.PrefetchScalarGridSpec(
            num_scalar_prefetch=2,
            grid=(n_tiles,),
            in_specs=[pl.BlockSpec((tm, d), lambda i, te, na: (jnp.minimum(i, na[0] - 1), 0)),
                      w1_spec, w3_spec, w2_spec],
            out_specs=pl.BlockSpec((tm, d), lambda i, te, na: (i, 0)),
        ),
        out_shape=jax.ShapeDtypeStruct((n_slots, d), _F32),
        compiler_params=pltpu.CompilerParams(
            dimension_semantics=("arbitrary",), vmem_limit_bytes=VMEM_LIMIT),
        name="expert_ffn",
    )(tile_expert, n_active, xs, w1, w3, w2)


def _sc_gather(table, idx):
    m, d = idx.shape[0], table.shape[1]
    workers = SC_CORES * SC_SUBCORES
    per_worker = m // workers
    pairs = per_worker // (2 * SC_ROWS)
    assert m % (workers * 2 * SC_ROWS) == 0

    def body(table_hbm, idx_hbm, out_hbm, idx_a, idx_b, rows_a, rows_b, gsem_a, gsem_b, wsem_a, wsem_b):
        base = (lax.axis_index("s") * SC_CORES + lax.axis_index("c")) * per_worker
        buf_a = (idx_a, rows_a, gsem_a, wsem_a)
        buf_b = (idx_b, rows_b, gsem_b, wsem_b)

        def out_rows(c):
            return out_hbm.at[pl.ds(pl.multiple_of(base + c * SC_ROWS, SC_ROWS), SC_ROWS)]

        def start_gather(c, buf):
            idx_v, rows_v, gsem, _ = buf
            off = pl.multiple_of(base + c * SC_ROWS, SC_ROWS)
            pltpu.sync_copy(idx_hbm.at[pl.ds(off, SC_ROWS)], idx_v)
            pltpu.async_copy(table_hbm.at[idx_v], rows_v, gsem)

        def start_write(c, buf):
            idx_v, rows_v, gsem, wsem = buf
            pltpu.make_async_copy(table_hbm.at[idx_v], rows_v, gsem).wait()
            pltpu.async_copy(rows_v, out_rows(c), wsem)

        def wait_write(c, buf):
            _, rows_v, _, wsem = buf
            pltpu.make_async_copy(rows_v, out_rows(c), wsem).wait()

        start_gather(0, buf_a)

        @pl.loop(0, pairs)
        def _(p):
            c = 2 * p

            @pl.when(p > 0)
            def _():
                wait_write(c - 1, buf_b)

            start_gather(c + 1, buf_b)
            start_write(c, buf_a)
            wait_write(c, buf_a)

            @pl.when(p < pairs - 1)
            def _():
                start_gather(c + 2, buf_a)

            start_write(c + 1, buf_b)

        wait_write(2 * pairs - 1, buf_b)

    rows = pltpu.VMEM((SC_ROWS, d), table.dtype)
    index = pltpu.VMEM((SC_ROWS,), jnp.int32)
    return pl.kernel(
        body, out_type=jax.ShapeDtypeStruct((m, d), table.dtype), mesh=_sc_mesh(),
        scratch_types=[index, index, rows, rows] + [pltpu.SemaphoreType.DMA] * 4,
        name="sc_gather")(table, idx)


def _sc_mesh():
    return plsc.VectorSubcoreMesh(core_axis_name="c", subcore_axis_name="s",
                                  num_cores=SC_CORES, num_subcores=SC_SUBCORES)


def _sc_scatter(x, pos, pad_slots, n_slots):
    n, d = x.shape
    workers = SC_CORES * SC_SUBCORES
    per_worker = n // workers
    pad_per_worker = pad_slots.shape[0] // workers
    assert n % (workers * SC_ROWS) == 0 and pad_slots.shape[0] % (workers * SC_ROWS) == 0
    assert pos.shape[0] == 2 * n and n_slots == 2 * n + pad_slots.shape[0]

    pairs = per_worker // (2 * SC_ROWS)
    assert per_worker % (2 * SC_ROWS) == 0

    def body(x_hbm, pos_hbm, pad_hbm, out_hbm, i0_a, i1_a, i0_b, i1_b, rows_a, rows_b,
             lsem_a, lsem_b, ssem_a, ssem_b):
        wid = lax.axis_index("s") * SC_CORES + lax.axis_index("c")
        buf_a = (i0_a, i1_a, rows_a, lsem_a, ssem_a)
        buf_b = (i0_b, i1_b, rows_b, lsem_b, ssem_b)

        def x_rows(c):
            return x_hbm.at[pl.ds(pl.multiple_of(wid * per_worker + c * SC_ROWS, SC_ROWS), SC_ROWS)]

        def start_load(c, buf):
            i0, i1, rows_v, lsem, _ = buf
            off = pl.multiple_of(wid * per_worker + c * SC_ROWS, SC_ROWS)
            pltpu.async_copy(x_rows(c), rows_v, lsem)
            pltpu.sync_copy(pos_hbm.at[pl.ds(off, SC_ROWS)], i0)
            pltpu.sync_copy(pos_hbm.at[pl.ds(n + off, SC_ROWS)], i1)

        def start_scatter(c, buf):
            i0, i1, rows_v, lsem, ssem = buf
            pltpu.make_async_copy(x_rows(c), rows_v, lsem).wait()
            pltpu.async_copy(rows_v, out_hbm.at[i0], ssem)
            pltpu.async_copy(rows_v, out_hbm.at[i1], ssem)

        def wait_scatter(buf):
            i0, i1, rows_v, _, ssem = buf
            pltpu.make_async_copy(rows_v, out_hbm.at[i0], ssem).wait()
            pltpu.make_async_copy(rows_v, out_hbm.at[i1], ssem).wait()

        start_load(0, buf_a)

        @pl.loop(0, pairs)
        def _(p):
            c = 2 * p

            @pl.when(p > 0)
            def _():
                wait_scatter(buf_b)

            start_load(c + 1, buf_b)
            start_scatter(c, buf_a)
            wait_scatter(buf_a)

            @pl.when(p < pairs - 1)
            def _():
                start_load(c + 2, buf_a)

            start_scatter(c + 1, buf_b)

        wait_scatter(buf_b)

        @pl.loop(0, pad_per_worker // SC_ROWS)
        def _(c):
            off = pl.multiple_of(wid * pad_per_worker + c * SC_ROWS, SC_ROWS)
            pltpu.sync_copy(pad_hbm.at[pl.ds(off, SC_ROWS)], i0_a)
            pltpu.sync_copy(rows_b, out_hbm.at[i0_a])

    rows = pltpu.VMEM((SC_ROWS, d), x.dtype)
    index = pltpu.VMEM((SC_ROWS,), jnp.int32)
    return pl.kernel(
        body, out_type=jax.ShapeDtypeStruct((n_slots, d), x.dtype), mesh=_sc_mesh(),
        scratch_types=[index] * 4 + [rows, rows] + [pltpu.SemaphoreType.DMA] * 4,
        name="sc_scatter")(x, pos, pad_slots)


def _combine_kernel(alpha, steps, x_ref, y0_ref, y1_ref, gate_ref, lg_ref, lb_ref, *refs):
    o_ref = refs[-1]

    @pl.when(pl.program_id(0) < steps)
    def _():
        o_ref[...] = _moe_residual(alpha, x_ref[...], y0_ref[...], y1_ref[...], gate_ref[...],
                                   lg_ref[...], lb_ref[...])

    @pl.when(pl.program_id(0) >= steps)
    def _():
        o_ref[...] = jnp.zeros_like(o_ref)


def _combine(parts, lg, lb, alpha):
    n, d = parts[0][0].shape
    tm = _move_tile(n)
    steps = n // tm
    norm = [lg.reshape(1, -1), lb.reshape(1, -1)]
    out = None
    for g, (x, y01, gates) in enumerate(parts):
        row = lambda i: jnp.minimum(i, steps - 1)
        in_specs = [pl.BlockSpec((tm, d), lambda i: (row(i), 0)),
                    pl.BlockSpec((tm, d), lambda i: (row(i), 0)),
                    pl.BlockSpec((tm, d), lambda i: (row(i) + steps, 0)),
                    pl.BlockSpec((tm, 2), lambda i: (row(i), 0)),
                    _const_spec((1, d)), _const_spec((1, d))]
        args = [x, y01, y01, gates, *norm]
        aliases = {}
        if g > 0:
            in_specs.append(pl.BlockSpec(memory_space=pl.ANY))
            aliases = {len(args): 0}
            args.append(out)
        out = pl.pallas_call(
            functools.partial(_combine_kernel, alpha, steps),
            grid=(len(parts) * steps if g == 0 else steps,),
            in_specs=in_specs,
            out_specs=pl.BlockSpec((tm, d), lambda i, g=g: (i + g * steps, 0)),
            out_shape=jax.ShapeDtypeStruct((len(parts) * n, d), _F32),
            input_output_aliases=aliases,
            compiler_params=pltpu.CompilerParams(
                dimension_semantics=("arbitrary",), vmem_limit_bytes=VMEM_LIMIT),
            name="combine",
        )(*args)
    return out


def _moe_experts(x, router, layer, w1, w3, w2):
    n, d = x.shape
    tm = FFN_TILE
    n_tiles = -(-(2 * n) // tm) + N_EXPERTS
    n_slots = n_tiles * tm

    ri, rf, cnt = _router(x, router.T.astype(_BF))
    counts = cnt[:, 0].astype(jnp.int32)
    padded = (counts + tm - 1) // tm * tm
    group_end = jnp.cumsum(padded)
    group_start = group_end - padded
    experts = jnp.arange(N_EXPERTS, dtype=jnp.int32)[:, None]

    def slot(e, rank):
        return jnp.sum(jnp.where(e[None, :] == experts, group_start[:, None], 0), axis=0) + rank

    pos = jnp.stack([slot(ri[0], ri[2]), slot(ri[1], ri[3])])
    tile_start = jnp.arange(n_tiles, dtype=jnp.int32) * tm
    tile_expert = jnp.minimum(
        jnp.sum(tile_start[:, None] >= group_end[None, :], axis=1), N_EXPERTS - 1).astype(jnp.int32)
    n_active = (group_end[-1:] // tm).astype(jnp.int32)

    pad_len = jnp.concatenate([padded - counts, n_slots - group_end[-1:]])
    pad_begin = jnp.concatenate([group_start + counts, group_end[-1:]])
    pad_end = jnp.cumsum(pad_len)
    k = jnp.arange(n_slots - 2 * n, dtype=jnp.int32)
    run = jnp.sum(k[:, None] >= pad_end[None, :], axis=1)
    pad_slots = k + jnp.sum(
        jnp.where(run[:, None] == jnp.arange(N_EXPERTS + 1)[None, :],
                  (pad_begin - (pad_end - pad_len))[None, :], 0), axis=1)
    xs = _sc_scatter(x, pos.reshape(-1), pad_slots.astype(jnp.int32), n_slots)
    ys = _expert_ffn(xs, layer, w1, w3, w2, tile_expert, n_active)
    return x, _sc_gather(ys, pos.reshape(-1)), rf[:2].T


def kernel(x, w_in, w_out, conv_w, conv_b, conv_ln_g, conv_ln_b, ret_gn_g, ln1_g, ln1_b, ln2_g,
           ln2_b, dense_w1, dense_w3, dense_w2, moe_router, moe_w1, moe_w3, moe_w2):
    b, s, d = x.shape
    depth = w_in.shape[0]
    alpha = (2.0 * depth) ** 0.25
    tables = _retention_tables(s)
    ng = BATCH_GROUPS if b % BATCH_GROUPS == 0 else 1
    bg = b // ng
    groups = [(x, g * bg, None) for g in range(ng)]
    mix_w = (w_in.astype(_BF), w_out.astype(_BF))
    dense_w = (dense_w1.astype(_BF), dense_w3.astype(_BF), dense_w2.astype(_BF))
    moe_f32 = (moe_w1, moe_w3, moe_w2)
    moe_w = {}
    for l in range(depth):
        i = l // 2
        for g, (xg, batch0, moe_part) in enumerate(groups):
            h = _mixer(xg, batch0, bg, l, *mix_w, conv_w[l], conv_b[l], conv_ln_g[l], conv_ln_b[l],
                       ret_gn_g[l], ln1_g[l], ln1_b[l], tables, alpha, moe_part).reshape(bg * s, d)
            if l % 2 == 0:
                cast = moe_f32 if g == 0 and l + 1 < depth else ()
                h, converted = _dense_ffn(h, i, *dense_w, ln2_g[l], ln2_b[l], alpha, cast, i)
                if cast:
                    moe_w[i] = converted
                groups[g] = (h.reshape(bg, s, d), 0, None)
            else:
                if i not in moe_w:
                    moe_w[i] = [a[i:i + 1].astype(_BF) for a in moe_f32]
                x_in, y01, gates = _moe_experts(h, moe_router[i], 0, *moe_w[i])
                groups[g] = (x_in.reshape(bg, s, d), 0, (y01, gates, ln2_g[l], ln2_b[l]))
    if groups[0][2] is None:
        return jnp.concatenate([xg for xg, _, _ in groups], axis=0)
    parts = [(xg.reshape(bg * s, d), part[0], part[1]) for xg, _, part in groups]
    return _combine(parts, ln2_g[-1], ln2_b[-1], alpha).reshape(b, s, d)
```

```python
import functools

import jax
import jax.numpy as jnp
from jax import lax
from jax.experimental import pallas as pl
from jax.experimental.pallas import tpu as pltpu
from jax.experimental.pallas import tpu_sc as plsc

CHUNK = 64
CONV_CH = 512
CONV_K = 31
RET_HEADS = 4
RET_DK = 64
RET_DV = 128
RET_QK_W = RET_HEADS * RET_DK
RET_V_W = RET_HEADS * RET_DV
ROPE_BASE = 10000.0
N_EXPERTS = 8
LN_EPS = 1e-5

LANES = 128
SUBLANES = 8
SEQ_TILE = 512
RET_BLOCK = 128
CONV_ROWS = 32
HALO = 32
FFN_TILE = 512
FFN_CHUNK = 512
ROUTE_TILE = 1024
MOVE_TILE = 512
SC_CORES = 2
SC_SUBCORES = 16
SC_ROWS = 32
BATCH_GROUPS = 2
VMEM_LIMIT = 56 * 1024 * 1024

_BF = jnp.bfloat16
_F32 = jnp.float32


def _dot(a, b):
    return jnp.dot(a, b, preferred_element_type=_F32)


def _layer_norm(v, g, b):
    mu = jnp.mean(v, axis=-1, keepdims=True)
    d = v - mu
    var = jnp.mean(d * d, axis=-1, keepdims=True)
    return d * lax.rsqrt(var + LN_EPS) * g + b


def _sigmoid(v):
    return 1.0 / (1.0 + jnp.exp(-v))


def _const_spec(shape):
    nd = len(shape)
    return pl.BlockSpec(shape, lambda *_: (0,) * nd, pipeline_mode=pl.Buffered(1))


def _layer_spec(stacked, layer):
    rest = stacked.shape[1:]
    return pl.BlockSpec((None,) + rest, lambda *_: (layer,) + (0,) * len(rest),
                        pipeline_mode=pl.Buffered(1))


def _moe_residual(alpha, x, y0, y1, gate, g, b):
    return _layer_norm(alpha * x + (y0 * gate[:, 0:1] + y1 * gate[:, 1:2]), g, b)


def _mixer_kernel(alpha, after_moe, x_ref, *refs):
    if after_moe:
        y0_ref, y1_ref, gate_ref, pg_ref, pb_ref, *refs = refs
    (w_in_ref, w_out_ref, conv_w_ref, conv_b_ref, cg_ref, cb_ref, gn_ref, lg_ref, lb_ref, cos_ref,
     sin_ref, dmask_ref, qdec_ref, kdec_ref, sdec_ref, bd_ref, o_ref, ush, state, rbuf) = refs
    t = x_ref.shape[0]
    ubuf = ush.at[0]

    @pl.when(pl.program_id(1) == 0)
    def _():
        ubuf[0:HALO, :] = jnp.zeros((HALO, CONV_CH), _F32)
        state[...] = jnp.zeros_like(state)

    x = x_ref[...]
    if after_moe:
        x = _moe_residual(alpha, x, y0_ref[...], y1_ref[...], gate_ref[...], pg_ref[...], pb_ref[...])
    xb = x.astype(_BF)

    c0 = 2 * CONV_CH
    ab = _dot(xb, w_in_ref[:, 0:c0])
    qk = _dot(xb, w_in_ref[:, c0:c0 + 2 * RET_QK_W])
    v = _dot(xb, w_in_ref[:, c0 + 2 * RET_QK_W:c0 + 2 * RET_QK_W + RET_V_W])
    g = _dot(xb, w_in_ref[:, c0 + 2 * RET_QK_W + RET_V_W:])

    ubuf[HALO:HALO + t, :] = ab[:, :CONV_CH] * _sigmoid(ab[:, CONV_CH:])
    span = t + HALO - SUBLANES
    for r in range(1, SUBLANES):
        ush[r, 0:span, :] = ubuf[r:r + span, :]
    off = HALO - (CONV_K - 1)
    conv_b = conv_b_ref[...]
    blocks = []
    for r0 in range(0, t, CONV_ROWS):
        acc = jnp.broadcast_to(conv_b, (CONV_ROWS, CONV_CH))
        for r in range(SUBLANES):
            taps = [j for j in range(CONV_K) if (j + off) % SUBLANES == r]
            for j in taps:
                a = r0 + (j + off) // SUBLANES * SUBLANES
                acc = acc + conv_w_ref[j:j + 1, :] * ush[r, a:a + CONV_ROWS, :]
        blocks.append(acc)
    conv = jnp.concatenate(blocks, axis=0)
    ubuf[0:HALO, :] = ubuf[t:t + HALO, :]
    un = _layer_norm(conv, cg_ref[...], cb_ref[...])
    u_out = un * _sigmoid(un)

    cos_t = cos_ref[...]
    sin_t = sin_ref[...]
    lane = lax.broadcasted_iota(jnp.int32, (1, LANES), 1)
    first_half = (lane % RET_DK) < (RET_DK // 2)

    def rope(z):
        parts = []
        for c in range(0, RET_QK_W, LANES):
            zc = z[:, c:c + LANES]
            up = pltpu.roll(zc, LANES - RET_DK // 2, axis=1)
            dn = pltpu.roll(zc, RET_DK // 2, axis=1)
            parts.append(jnp.where(first_half, up, dn))
        return z * cos_t + jnp.concatenate(parts, axis=1) * sin_t

    q = rope(qk[:, :RET_QK_W])
    k = rope(qk[:, RET_QK_W:])

    lane_qk = lax.broadcasted_iota(jnp.int32, (1, RET_QK_W), 1)
    dmask = dmask_ref[...]
    qdec = qdec_ref[...]
    kdec = kdec_ref[...]
    nr = RET_BLOCK
    for s in range(t // nr):
        rows = slice(s * nr, (s + 1) * nr)
        q_s, k_s, v_s = q[rows], k[rows], v[rows]
        v_b = v_s.astype(_BF)
        kbd = jnp.concatenate(
            [jnp.where(lane_qk // RET_DK == h, k_s, 0.0).astype(_BF) for h in range(RET_HEADS)],
            axis=0)
        sc = lax.dot_general(q_s.astype(_BF), kbd, (((1,), (1,)), ((), ())),
                             preferred_element_type=_F32)
        p = (sc * dmask).astype(_BF)
        intra = jnp.concatenate(
            [_dot(p[:, h * nr:(h + 1) * nr], v_b[:, h * RET_DV:(h + 1) * RET_DV])
             for h in range(RET_HEADS)], axis=1)
        st = state[...]
        cross = _dot((q_s * qdec).astype(_BF), st.astype(_BF))
        kv = lax.dot_general((k_s * kdec).astype(_BF), v_b, (((0,), (0,)), ((), ())),
                             preferred_element_type=_F32)
        state[...] = st * sdec_ref[...] + kv * bd_ref[...]
        rbuf[rows, :] = intra + cross

    r = rbuf[...]
    gate = g * _sigmoid(g)
    gn = gn_ref[...]
    r_parts = []
    for h in range(RET_HEADS):
        hs = slice(h * RET_DV, (h + 1) * RET_DV)
        rh = r[:, hs]
        mu = jnp.mean(rh, axis=-1, keepdims=True)
        d = rh - mu
        var = jnp.mean(d * d, axis=-1, keepdims=True)
        r_parts.append(gate[:, hs] * (d * lax.rsqrt(var + LN_EPS) * gn[:, hs]))

    mixed = jnp.concatenate([u_out] + r_parts, axis=1).astype(_BF)
    mix = _dot(mixed, w_out_ref[...])
    o_ref[...] = _layer_norm(alpha * x + mix, lg_ref[...], lb_ref[...])


def _retention_tables(seq):
    half = RET_DK // 2
    inv_freq = ROPE_BASE ** (-jnp.arange(half, dtype=_F32) / half)
    ang = jnp.arange(seq, dtype=jnp.int32).astype(_F32)[:, None] * inv_freq[None, :]
    cos, sin = jnp.cos(ang), jnp.sin(ang)
    cos_t = jnp.tile(jnp.concatenate([cos, cos], axis=1), (1, RET_HEADS))
    sin_t = jnp.tile(jnp.concatenate([-sin, sin], axis=1), (1, RET_HEADS))

    nr = RET_BLOCK
    log_gamma = jnp.log1p(-(2.0 ** (-5.0 - jnp.arange(RET_HEADS, dtype=_F32))))
    idx = jnp.arange(nr, dtype=_F32)
    dist = jnp.abs(idx[:, None] - idx[None, :])
    visible = (jnp.arange(nr)[None, :] // CHUNK) <= (jnp.arange(nr)[:, None] // CHUNK)
    scale = RET_DK ** -0.5
    dm = jnp.exp(log_gamma[:, None, None] * dist[None]) * visible[None] * scale
    dmask = jnp.transpose(dm, (1, 0, 2)).reshape(nr, RET_HEADS * nr)
    qdec = jnp.repeat(jnp.exp(log_gamma[None, :] * (idx + 1.0)[:, None]), RET_DK, axis=1)
    kdec = jnp.repeat(jnp.exp(log_gamma[None, :] * (nr - 1.0 - idx)[:, None]), RET_DK, axis=1) * scale
    sdec = jnp.broadcast_to(jnp.repeat(jnp.exp(log_gamma * nr), RET_DK)[:, None],
                            (RET_QK_W, RET_V_W))
    bd = (jnp.arange(RET_QK_W)[:, None] // RET_DK == jnp.arange(RET_V_W)[None, :] // RET_DV)
    return cos_t, sin_t, dmask, qdec, kdec, sdec, bd.astype(_F32)


def _mixer(x, batch0, nb, layer, w_in, w_out, conv_w, conv_b, cg, cb, gn, lg, lb, tables, alpha,
           moe_part=None):
    _, s, d = x.shape
    t = min(SEQ_TILE, s)
    assert s % t == 0 and t % RET_BLOCK == 0 and RET_BLOCK % CHUNK == 0 and t % CONV_ROWS == 0
    cos_t, sin_t, dmask, qdec, kdec, sdec, bd = tables
    row = lambda a: a.reshape(1, -1)
    conv_w = jnp.pad(conv_w, ((0, HALO - CONV_K), (0, 0)))
    consts = [conv_w, row(conv_b), row(cg), row(cb), row(gn), row(lg), row(lb)]
    tail = [dmask, qdec, kdec, sdec, bd]
    pre, pre_specs = [], []
    if moe_part is not None:
        assert batch0 == 0 and x.shape[0] == nb
        y01, gates, pg, pb = moe_part
        y01 = y01.reshape(2, nb, s, d)
        pre = [y01, y01, gates.reshape(nb, s, 2), row(pg), row(pb)]
        pre_specs = [pl.BlockSpec((None, None, t, d), lambda i, j: (0, i, j, 0)),
                     pl.BlockSpec((None, None, t, d), lambda i, j: (1, i, j, 0)),
                     pl.BlockSpec((None, t, 2), lambda i, j: (i, j, 0)),
                     _const_spec((1, d)), _const_spec((1, d))]
    in_specs = ([pl.BlockSpec((None, t, d), lambda i, j: (i + batch0, j, 0))] + pre_specs
                + [_layer_spec(w_in, layer), _layer_spec(w_out, layer)]
                + [_const_spec(a.shape) for a in consts]
                + [pl.BlockSpec((t, RET_QK_W), lambda i, j: (j, 0))] * 2
                + [_const_spec(a.shape) for a in tail])
    return pl.pallas_call(
        functools.partial(_mixer_kernel, alpha, moe_part is not None),
        grid=(nb, s // t),
        in_specs=in_specs,
        out_specs=pl.BlockSpec((None, t, d), lambda i, j: (i, j, 0)),
        out_shape=jax.ShapeDtypeStruct((nb, s, d), _F32),
        scratch_shapes=[pltpu.VMEM((SUBLANES, t + HALO, CONV_CH), _F32),
                        pltpu.VMEM((RET_QK_W, RET_V_W), _F32),
                        pltpu.VMEM((t, RET_V_W), _F32)],
        compiler_params=pltpu.CompilerParams(
            dimension_semantics=("arbitrary", "arbitrary"), vmem_limit_bytes=VMEM_LIMIT),
        name="mixer",
    )(x, *pre, w_in, w_out, *consts, cos_t, sin_t, *tail)


def _swiglu_tile(xb, w1_ref, w3_ref, w2_ref):
    d_ff = w1_ref.shape[1]
    acc = None
    for c in range(0, d_ff, FFN_CHUNK):
        a = _dot(xb, w1_ref[:, c:c + FFN_CHUNK])
        g = _dot(xb, w3_ref[:, c:c + FFN_CHUNK])
        h = (a * _sigmoid(a) * g).astype(_BF)
        part = _dot(h, w2_ref[c:c + FFN_CHUNK, :])
        acc = part if acc is None else acc + part
    return acc


def _dense_ffn_kernel(alpha, n_cast, x_ref, w1_ref, w3_ref, w2_ref, lg_ref, lb_ref, *refs):
    cast_in, (o_ref, *cast_out) = refs[:n_cast], refs[n_cast:]
    x = x_ref[...]
    f = _swiglu_tile(x.astype(_BF), w1_ref, w3_ref, w2_ref)
    o_ref[...] = _layer_norm(alpha * x + f, lg_ref[...], lb_ref[...])
    for src, dst in zip(cast_in, cast_out):
        dst[...] = src[...].astype(_BF)


def _dense_ffn(x, layer, w1, w3, w2, lg, lb, alpha, cast=(), cast_layer=0):
    n, d = x.shape
    tm = min(FFN_TILE, n)
    steps = n // tm
    assert n % tm == 0 and w1.shape[2] % FFN_CHUNK == 0
    weights = [w1, w3, w2]
    norm = [lg.reshape(1, -1), lb.reshape(1, -1)]
    cast_in, cast_specs, cast_shapes = [], [], []
    for a in cast:
        cols = a.shape[-1]
        rows = a[0].size // cols
        assert rows % (steps * 2 * SUBLANES) == 0
        cast_in.append(a.reshape(-1, cols))
        cast_specs.append(pl.BlockSpec((rows // steps, cols),
                                       lambda i, base=cast_layer * steps: (base + i, 0)))
        cast_shapes.append(jax.ShapeDtypeStruct((rows, cols), _BF))
    row_spec = pl.BlockSpec((tm, d), lambda i: (i, 0))
    out = pl.pallas_call(
        functools.partial(_dense_ffn_kernel, alpha, len(cast)),
        grid=(steps,),
        in_specs=([row_spec] + [_layer_spec(w, layer) for w in weights]
                  + [_const_spec(a.shape) for a in norm] + cast_specs),
        out_specs=[row_spec] + [pl.BlockSpec(s.block_shape, lambda i: (i, 0)) for s in cast_specs],
        out_shape=[jax.ShapeDtypeStruct((n, d), _F32)] + cast_shapes,
        compiler_params=pltpu.CompilerParams(
            dimension_semantics=("arbitrary",), vmem_limit_bytes=VMEM_LIMIT),
        name="dense_ffn",
    )(x, *weights, *norm, *cast_in)
    return out[0], [o.reshape((1,) + a.shape[1:]) for o, a in zip(out[1:], cast)]


def _router_kernel(x_ref, wr_ref, tri_ref, ri_ref, rf_ref, cnt_ref, carry):
    @pl.when(pl.program_id(0) == 0)
    def _():
        carry[...] = jnp.zeros_like(carry)

    tr = x_ref.shape[0]
    logits = lax.dot_general(wr_ref[...], x_ref[...].astype(_BF), (((1,), (1,)), ((), ())),
                             preferred_element_type=_F32)
    eidx = lax.broadcasted_iota(jnp.int32, (N_EXPERTS, tr), 0)
    m0 = jnp.max(logits, axis=0, keepdims=True)
    e0 = jnp.min(jnp.where(logits == m0, eidx, N_EXPERTS), axis=0, keepdims=True)
    rest = jnp.where(eidx == e0, -jnp.inf, logits)
    m1 = jnp.max(rest, axis=0, keepdims=True)
    e1 = jnp.min(jnp.where(rest == m1, eidx, N_EXPERTS), axis=0, keepdims=True)
    tt = jnp.exp(m1 - m0)
    g0 = 1.0 / (1.0 + tt)
    g1 = tt / (1.0 + tt)

    oh0 = eidx == e0
    oh1 = eidx == e1
    member = jnp.where(oh0 | oh1, 1.0, 0.0)
    before = _dot(member.astype(_BF), tri_ref[...]) + carry[:, 0:1]
    rank0 = jnp.sum(jnp.where(oh0, before, 0.0), axis=0, keepdims=True)
    rank1 = jnp.sum(jnp.where(oh1, before, 0.0), axis=0, keepdims=True)
    carry[...] = carry[...] + jnp.sum(member, axis=1, keepdims=True)

    zi = jnp.zeros((N_EXPERTS - 4, tr), jnp.int32)
    ri_ref[...] = jnp.concatenate(
        [e0, e1, rank0.astype(jnp.int32), rank1.astype(jnp.int32), zi], axis=0)
    rf_ref[...] = jnp.concatenate([g0, g1, jnp.zeros((N_EXPERTS - 2, tr), _F32)], axis=0)
    cnt_ref[...] = carry[...]


def _router(x, wr_t):
    n, d = x.shape
    tr = min(ROUTE_TILE, n)
    assert n % tr == 0
    tri = (jnp.arange(tr)[:, None] < jnp.arange(tr)[None, :]).astype(_BF)
    return pl.pallas_call(
        _router_kernel,
        grid=(n // tr,),
        in_specs=[pl.BlockSpec((tr, d), lambda i: (i, 0)),
                  _const_spec(wr_t.shape), _const_spec(tri.shape)],
        out_specs=[pl.BlockSpec((N_EXPERTS, tr), lambda i: (0, i)),
                   pl.BlockSpec((N_EXPERTS, tr), lambda i: (0, i)),
                   pl.BlockSpec((N_EXPERTS, LANES), lambda i: (0, 0))],
        out_shape=[jax.ShapeDtypeStruct((N_EXPERTS, n), jnp.int32),
                   jax.ShapeDtypeStruct((N_EXPERTS, n), _F32),
                   jax.ShapeDtypeStruct((N_EXPERTS, LANES), _F32)],
        scratch_shapes=[pltpu.VMEM((N_EXPERTS, LANES), _F32)],
        compiler_params=pltpu.CompilerParams(
            dimension_semantics=("arbitrary",), vmem_limit_bytes=VMEM_LIMIT),
        name="router",
    )(x, wr_t, tri)


def _move_tile(n):
    tm = min(MOVE_TILE, n)
    assert n % tm == 0 and tm % SUBLANES == 0
    return tm


def _expert_kernel(te_ref, na_ref, xs_ref, w1_ref, w3_ref, w2_ref, ys_ref):
    del te_ref
    active = pl.program_id(0) < na_ref[0]

    @pl.when(active)
    def _():
        ys_ref[...] = _swiglu_tile(xs_ref[...].astype(_BF), w1_ref, w3_ref, w2_ref)

    @pl.when(jnp.logical_not(active))
    def _():
        ys_ref[...] = jnp.zeros_like(ys_ref)


def _expert_ffn(xs, layer, w1, w3, w2, tile_expert, n_active):
    d = xs.shape[1]
    tm = FFN_TILE
    d_ff = w1.shape[3]
    n_tiles = tile_expert.shape[0]
    n_slots = n_tiles * tm
    expert_block = lambda i, te, na: (layer, te[i], 0, 0)
    w1_spec = pl.BlockSpec((None, None, d, d_ff), expert_block, pipeline_mode=pl.Buffered(1))
    w3_spec = pl.BlockSpec((None, None, d, d_ff), expert_block)
    w2_spec = pl.BlockSpec((None, None, d_ff, d), expert_block)
    return pl.pallas_call(
        _expert_kernel,
        grid_spec=pltpu.PrefetchScalarGridSpec(
            num_scalar_prefetch=2,
            grid=(n_tiles,),
            in_specs=[pl.BlockSpec((tm, d), lambda i, te, na: (jnp.minimum(i, na[0] - 1), 0)),
                      w1_spec, w3_spec, w2_spec],
            out_specs=pl.BlockSpec((tm, d), lambda i, te, na: (i, 0)),
        ),
        out_shape=jax.ShapeDtypeStruct((n_slots, d), _F32),
        compiler_params=pltpu.CompilerParams(
            dimension_semantics=("arbitrary",), vmem_limit_bytes=VMEM_LIMIT),
        name="expert_ffn",
    )(tile_expert, n_active, xs, w1, w3, w2)


def _sc_gather(table, idx):
    m, d = idx.shape[0], table.shape[1]
    workers = SC_CORES * SC_SUBCORES
    per_worker = m // workers
    pairs = per_worker // (2 * SC_ROWS)
    assert m % (workers * 2 * SC_ROWS) == 0

    def body(table_hbm, idx_hbm, out_hbm, idx_a, idx_b, rows_a, rows_b, gsem_a, gsem_b, wsem_a, wsem_b):
        base = (lax.axis_index("s") * SC_CORES + lax.axis_index("c")) * per_worker
        buf_a = (idx_a, rows_a, gsem_a, wsem_a)
        buf_b = (idx_b, rows_b, gsem_b, wsem_b)

        def out_rows(c):
            return out_hbm.at[pl.ds(pl.multiple_of(base + c * SC_ROWS, SC_ROWS), SC_ROWS)]

        def start_gather(c, buf):
            idx_v, rows_v, gsem, _ = buf
            off = pl.multiple_of(base + c * SC_ROWS, SC_ROWS)
            pltpu.sync_copy(idx_hbm.at[pl.ds(off, SC_ROWS)], idx_v)
            pltpu.async_copy(table_hbm.at[idx_v], rows_v, gsem)

        def start_write(c, buf):
            idx_v, rows_v, gsem, wsem = buf
            pltpu.make_async_copy(table_hbm.at[idx_v], rows_v, gsem).wait()
            pltpu.async_copy(rows_v, out_rows(c), wsem)

        def wait_write(c, buf):
            _, rows_v, _, wsem = buf
            pltpu.make_async_copy(rows_v, out_rows(c), wsem).wait()

        start_gather(0, buf_a)

        @pl.loop(0, pairs)
        def _(p):
            c = 2 * p

            @pl.when(p > 0)
            def _():
                wait_write(c - 1, buf_b)

            start_gather(c + 1, buf_b)
            start_write(c, buf_a)
            wait_write(c, buf_a)

            @pl.when(p < pairs - 1)
            def _():
                start_gather(c + 2, buf_a)

            start_write(c + 1, buf_b)

        wait_write(2 * pairs - 1, buf_b)

    rows = pltpu.VMEM((SC_ROWS, d), table.dtype)
    index = pltpu.VMEM((SC_ROWS,), jnp.int32)
    return pl.kernel(
        body, out_type=jax.ShapeDtypeStruct((m, d), table.dtype), mesh=_sc_mesh(),
        scratch_types=[index, index, rows, rows] + [pltpu.SemaphoreType.DMA] * 4,
        name="sc_gather")(table, idx)


def _sc_mesh():
    return plsc.VectorSubcoreMesh(core_axis_name="c", subcore_axis_name="s",
                                  num_cores=SC_CORES, num_subcores=SC_SUBCORES)


def _sc_scatter(x, pos, pad_slots, n_slots):
    n, d = x.shape
    workers = SC_CORES * SC_SUBCORES
    per_worker = n // workers
    pad_per_worker = pad_slots.shape[0] // workers
    assert n % (workers * SC_ROWS) == 0 and pad_slots.shape[0] % (workers * SC_ROWS) == 0
    assert pos.shape[0] == 2 * n and n_slots == 2 * n + pad_slots.shape[0]

    pairs = per_worker // (2 * SC_ROWS)
    assert per_worker % (2 * SC_ROWS) == 0

    def body(x_hbm, pos_hbm, pad_hbm, out_hbm, i0_a, i1_a, i0_b, i1_b, rows_a, rows_b,
             lsem_a, lsem_b, ssem_a, ssem_b):
        wid = lax.axis_index("s") * SC_CORES + lax.axis_index("c")
        buf_a = (i0_a, i1_a, rows_a, lsem_a, ssem_a)
        buf_b = (i0_b, i1_b, rows_b, lsem_b, ssem_b)

        def x_rows(c):
            return x_hbm.at[pl.ds(pl.multiple_of(wid * per_worker + c * SC_ROWS, SC_ROWS), SC_ROWS)]

        def start_load(c, buf):
            i0, i1, rows_v, lsem, _ = buf
            off = pl.multiple_of(wid * per_worker + c * SC_ROWS, SC_ROWS)
            pltpu.async_copy(x_rows(c), rows_v, lsem)
            pltpu.sync_copy(pos_hbm.at[pl.ds(off, SC_ROWS)], i0)
            pltpu.sync_copy(pos_hbm.at[pl.ds(n + off, SC_ROWS)], i1)

        def start_scatter(c, buf):
            i0, i1, rows_v, lsem, ssem = buf
            pltpu.make_async_copy(x_rows(c), rows_v, lsem).wait()
            pltpu.async_copy(rows_v, out_hbm.at[i0], ssem)
            pltpu.async_copy(rows_v, out_hbm.at[i1], ssem)

        def wait_scatter(buf):
            i0, i1, rows_v, _, ssem = buf
            pltpu.make_async_copy(rows_v, out_hbm.at[i0], ssem).wait()
            pltpu.make_async_copy(rows_v, out_hbm.at[i1], ssem).wait()

        start_load(0, buf_a)

        @pl.loop(0, pairs)
        def _(p):
            c = 2 * p

            @pl.when(p > 0)
            def _():
                wait_scatter(buf_b)

            start_load(c + 1, buf_b)
            start_scatter(c, buf_a)
            wait_scatter(buf_a)

            @pl.when(p < pairs - 1)
            def _():
                start_load(c + 2, buf_a)

            start_scatter(c + 1, buf_b)

        wait_scatter(buf_b)

        @pl.loop(0, pad_per_worker // SC_ROWS)
        def _(c):
            off = pl.multiple_of(wid * pad_per_worker + c * SC_ROWS, SC_ROWS)
            pltpu.sync_copy(pad_hbm.at[pl.ds(off, SC_ROWS)], i0_a)
            pltpu.sync_copy(rows_b, out_hbm.at[i0_a])

    rows = pltpu.VMEM((SC_ROWS, d), x.dtype)
    index = pltpu.VMEM((SC_ROWS,), jnp.int32)
    return pl.kernel(
        body, out_type=jax.ShapeDtypeStruct((n_slots, d), x.dtype), mesh=_sc_mesh(),
        scratch_types=[index] * 4 + [rows, rows] + [pltpu.SemaphoreType.DMA] * 4,
        name="sc_scatter")(x, pos, pad_slots)


def _combine_kernel(alpha, n_groups, *refs):
    lg_ref, lb_ref, o_ref = refs[-3:]
    steps = pl.num_programs(0) // n_groups
    i = pl.program_id(0)
    for g in range(n_groups):
        x_ref, y0_ref, y1_ref, gate_ref = refs[4 * g:4 * g + 4]

        @pl.when((i >= g * steps) & (i < (g + 1) * steps))
        def _(x_ref=x_ref, y0_ref=y0_ref, y1_ref=y1_ref, gate_ref=gate_ref):
            o_ref[...] = _moe_residual(alpha, x_ref[...], y0_ref[...], y1_ref[...], gate_ref[...],
                                       lg_ref[...], lb_ref[...])


def _combine(parts, lg, lb, alpha):
    n, d = parts[0][0].shape
    tm = _move_tile(n)
    steps = n // tm
    in_specs, args = [], []
    for g, (x, y01, gates) in enumerate(parts):
        row = lambda i, g=g: jnp.clip(i - g * steps, 0, steps - 1)
        in_specs += [pl.BlockSpec((tm, d), lambda i, row=row: (row(i), 0)),
                     pl.BlockSpec((tm, d), lambda i, row=row: (row(i), 0)),
                     pl.BlockSpec((tm, d), lambda i, row=row: (row(i) + steps, 0)),
                     pl.BlockSpec((tm, 2), lambda i, row=row: (row(i), 0))]
        args += [x, y01, y01, gates]
    return pl.pallas_call(
        functools.partial(_combine_kernel, alpha, len(parts)),
        grid=(len(parts) * steps,),
        in_specs=in_specs + [_const_spec((1, d)), _const_spec((1, d))],
        out_specs=pl.BlockSpec((tm, d), lambda i: (i, 0)),
        out_shape=jax.ShapeDtypeStruct((len(parts) * n, d), _F32),
        compiler_params=pltpu.CompilerParams(
            dimension_semantics=("arbitrary",), vmem_limit_bytes=VMEM_LIMIT),
        name="combine",
    )(*args, lg.reshape(1, -1), lb.reshape(1, -1))


def _moe_experts(x, router, layer, w1, w3, w2):
    n, d = x.shape
    tm = FFN_TILE
    n_tiles = -(-(2 * n) // tm) + N_EXPERTS
    n_slots = n_tiles * tm

    ri, rf, cnt = _router(x, router.T.astype(_BF))
    counts = cnt[:, 0].astype(jnp.int32)
    padded = (counts + tm - 1) // tm * tm
    group_end = jnp.cumsum(padded)
    group_start = group_end - padded
    experts = jnp.arange(N_EXPERTS, dtype=jnp.int32)[:, None]

    def slot(e, rank):
        return jnp.sum(jnp.where(e[None, :] == experts, group_start[:, None], 0), axis=0) + rank

    pos = jnp.stack([slot(ri[0], ri[2]), slot(ri[1], ri[3])])
    tile_start = jnp.arange(n_tiles, dtype=jnp.int32) * tm
    tile_expert = jnp.minimum(
        jnp.sum(tile_start[:, None] >= group_end[None, :], axis=1), N_EXPERTS - 1).astype(jnp.int32)
    n_active = (group_end[-1:] // tm).astype(jnp.int32)

    pad_len = jnp.concatenate([padded - counts, n_slots - group_end[-1:]])
    pad_begin = jnp.concatenate([group_start + counts, group_end[-1:]])
    pad_end = jnp.cumsum(pad_len)
    k = jnp.arange(n_slots - 2 * n, dtype=jnp.int32)
    run = jnp.sum(k[:, None] >= pad_end[None, :], axis=1)
    pad_slots = k + jnp.sum(
        jnp.where(run[:, None] == jnp.arange(N_EXPERTS + 1)[None, :],
                  (pad_begin - (pad_end - pad_len))[None, :], 0), axis=1)
    xs = _sc_scatter(x, pos.reshape(-1), pad_slots.astype(jnp.int32), n_slots)
    ys = _expert_ffn(xs, layer, w1, w3, w2, tile_expert, n_active)
    return x, _sc_gather(ys, pos.reshape(-1)), rf[:2].T


def kernel(x, w_in, w_out, conv_w, conv_b, conv_ln_g, conv_ln_b, ret_gn_g, ln1_g, ln1_b, ln2_g,
           ln2_b, dense_w1, dense_w3, dense_w2, moe_router, moe_w1, moe_w3, moe_w2):
    b, s, d = x.shape
    depth = w_in.shape[0]
    alpha = (2.0 * depth) ** 0.25
    tables = _retention_tables(s)
    ng = BATCH_GROUPS if b % BATCH_GROUPS == 0 else 1
    bg = b // ng
    groups = [(x, g * bg, None) for g in range(ng)]
    mix_w = (w_in.astype(_BF), w_out.astype(_BF))
    dense_w = (dense_w1.astype(_BF), dense_w3.astype(_BF), dense_w2.astype(_BF))
    moe_f32 = (moe_w1, moe_w3, moe_w2)
    moe_w = {}
    for l in range(depth):
        i = l // 2
        if l % 2 == 0 and all(part is None and xg is groups[0][0] for xg, _, part in groups):
            h = _mixer(groups[0][0], 0, b, l, *mix_w, conv_w[l], conv_b[l], conv_ln_g[l],
                       conv_ln_b[l], ret_gn_g[l], ln1_g[l], ln1_b[l], tables, alpha).reshape(b * s, d)
            cast = moe_f32 if l + 1 < depth else ()
            h, converted = _dense_ffn(h, i, *dense_w, ln2_g[l], ln2_b[l], alpha, cast, i)
            if cast:
                moe_w[i] = converted
            groups = [(h.reshape(b, s, d), g * bg, None) for g in range(ng)]
            continue
        for g, (xg, batch0, moe_part) in enumerate(groups):
            h = _mixer(xg, batch0, bg, l, *mix_w, conv_w[l], conv_b[l], conv_ln_g[l], conv_ln_b[l],
                       ret_gn_g[l], ln1_g[l], ln1_b[l], tables, alpha, moe_part).reshape(bg * s, d)
            if l % 2 == 0:
                cast = moe_f32 if g == 0 and l + 1 < depth else ()
                h, converted = _dense_ffn(h, i, *dense_w, ln2_g[l], ln2_b[l], alpha, cast, i)
                if cast:
                    moe_w[i] = converted
                groups[g] = (h.reshape(bg, s, d), 0, None)
            else:
                if i not in moe_w:
                    moe_w[i] = [a[i:i + 1].astype(_BF) for a in moe_f32]
                x_in, y01, gates = _moe_experts(h, moe_router[i], 0, *moe_w[i])
                groups[g] = (x_in.reshape(bg, s, d), 0, (y01, gates, ln2_g[l], ln2_b[l]))
    if groups[0][2] is None:
        return jnp.concatenate([xg for xg, _, _ in groups], axis=0)
    parts = [(xg.reshape(bg * s, d), part[0], part[1]) for xg, _, part in groups]
    return _combine(parts, ln2_g[-1], ln2_b[-1], alpha).reshape(b, s, d)
```
